```python
import math
import jax, jax.numpy as jnp
from jax import lax
import numpy as np

D_MODEL = 1024
BATCH = 2
SEQ = 8192
DEPTH = 2

W_SSM = 256
W_RWKV = 256
W_CONV = 256
W_FFT = 256
D_MIX = W_SSM + W_RWKV + W_CONV + W_FFT
D_IN_PROJ = W_SSM + 3 * W_RWKV + 3 * W_CONV + W_FFT
SSM_CH = 16
SSM_GROUPS = W_SSM // SSM_CH
SSM_STATE = 64
DT_MIN = 1e-3
DT_MAX = 1e-1
RWKV_HEAD = 64
RWKV_HEADS = W_RWKV // RWKV_HEAD
DECAY_LORA = 64
AAA_LORA = 64
GATE_LORA = 160
RWKV_DECAY_SCALE = math.exp(-0.5)
RWKV_GN_EPS = 64e-5
CONV_WIDTH = 3
FFT_GROUPS = 4
FFT_CH = W_FFT // FFT_GROUPS
D_FF = int(math.ceil(8 * D_MODEL / 3 / 256)) * 256
DEEPNORM_ALPHA = (2 * DEPTH) ** 0.25
DEEPNORM_BETA = (8 * DEPTH) ** -0.25
LN_EPS = 1e-5

kernel_name = 'hybrid_parallel_head_encoder'


def _layer_norm(x, g, b):
    xf = x.astype(jnp.float32)
    mu = jnp.mean(xf, -1, keepdims=True)
    var = jnp.mean(jnp.square(xf - mu), -1, keepdims=True)
    return ((xf - mu) * lax.rsqrt(var + LN_EPS) * g + b).astype(x.dtype)


def _shift_prev(z):
    return jnp.pad(z[:, :-1], ((0, 0), (1, 0), (0, 0)))


def _shift_next(z):
    return jnp.pad(z[:, 1:], ((0, 0), (0, 1), (0, 0)))


def _flip_backward(z):
    return jnp.concatenate([z[:1], jnp.flip(z[1:], axis=2)], axis=0)


def _cplx_affine_combine(earlier, later):
    a1r, a1i, b1r, b1i = earlier
    a2r, a2i, b2r, b2i = later
    return (a2r * a1r - a2i * a1i,
            a2r * a1i + a2i * a1r,
            a2r * b1r - a2i * b1i + b2r,
            a2r * b1i + a2i * b1r + b2i)


def _s5_mixer(u, lam_re, lam_im, log_dt, b_re, b_im, c_re, c_im, d_skip, glu_w, glu_b):
    bsz, seq, _ = u.shape
    ug = u.reshape(bsz, seq, SSM_GROUPS, SSM_CH)
    dt = jnp.exp(log_dt)[..., None]
    mag = jnp.exp(lam_re * dt)
    lb_re = mag * jnp.cos(lam_im * dt)
    lb_im = mag * jnp.sin(lam_im * dt)
    den = lam_re * lam_re + lam_im * lam_im
    nr = lb_re - 1.0
    coef_re = (nr * lam_re + lb_im * lam_im) / den
    coef_im = (lb_im * lam_re - nr * lam_im) / den
    bb_re = coef_re[..., None] * b_re - coef_im[..., None] * b_im
    bb_im = coef_re[..., None] * b_im + coef_im[..., None] * b_re
    bu_re = _flip_backward(jnp.einsum('bsgh,dgph->dbsgp', ug, bb_re))
    bu_im = _flip_backward(jnp.einsum('bsgh,dgph->dbsgp', ug, bb_im))
    a_re = jnp.broadcast_to(lb_re[:, None, None], bu_re.shape)
    a_im = jnp.broadcast_to(lb_im[:, None, None], bu_re.shape)
    _, _, x_re, x_im = lax.associative_scan(_cplx_affine_combine, (a_re, a_im, bu_re, bu_im), axis=2)
    x_re = _flip_backward(x_re)
    x_im = _flip_backward(x_im)
    y = (jnp.einsum('dbsgp,dghp->bsgh', x_re, c_re)
         - jnp.einsum('dbsgp,dghp->bsgh', x_im, c_im)
         + d_skip * ug)
    y = jax.nn.gelu(y.reshape(bsz, seq, W_SSM))
    return y * jax.nn.sigmoid(y @ glu_w + glu_b)


def _rwkv7_mixer(xn, p_rkv, mu_rkv, mu_w, mu_a, mu_g, w0, w1, w2, a0, a1, a2,
                 g1, g2, k_k, k_a, r_k, gn_g, gn_b):
    bsz, seq, _ = xn.shape
    shifted = jnp.stack([_shift_prev(p_rkv), _shift_next(p_rkv)])
    rkv = p_rkv + (shifted - p_rkv) * mu_rkv[:, None, None, :]
    r, k, v = jnp.split(rkv, 3, axis=-1)
    x_prev = _shift_prev(xn)
    x_next = _shift_next(xn)
    dx = jnp.stack([x_prev, x_next]) - xn
    xw = xn + dx * mu_w[:, None, None, :]
    xa = xn + dx * mu_a[:, None, None, :]
    xg = xn + (0.5 * (x_prev + x_next) - xn) * mu_g
    w_lora = jnp.einsum('dbsr,dre->dbse', jnp.tanh(jnp.einsum('dbsc,dcr->dbsr', xw, w1)), w2)
    decay = jnp.exp(-RWKV_DECAY_SCALE * jax.nn.sigmoid(w0[:, None, None, :] + w_lora))
    a = jax.nn.sigmoid(a0[:, None, None, :]
                       + jnp.einsum('dbsr,dre->dbse', jnp.einsum('dbsc,dcr->dbsr', xa, a1), a2))
    g = jax.nn.sigmoid(xg @ g1) @ g2

    def heads(z):
        return z.reshape(2, bsz, seq, RWKV_HEADS, RWKV_HEAD)

    r, k, v, decay, a = heads(r), heads(k), heads(v), heads(decay), heads(a)
    kk = k * k_k.reshape(RWKV_HEADS, RWKV_HEAD)
    kkf = kk.astype(jnp.float32)
    kk = (kkf * lax.rsqrt(jnp.sum(kkf * kkf, -1, keepdims=True) + 1e-12)).astype(k.dtype)
    k = k * (1.0 + (a - 1.0) * k_a.reshape(RWKV_HEADS, RWKV_HEAD))
    bonus = jnp.sum(jnp.sum(r * k * r_k, -1, keepdims=True) * v, axis=0)

    def to_time(z):
        return jnp.moveaxis(_flip_backward(z), 2, 0)

    xs = (to_time(r), to_time(decay), to_time(k), to_time(v), to_time(-kk), to_time(kk * a))

    def step(state, inp):
        r_t, w_t, k_t, v_t, ka_t, kb_t = inp
        sa = jnp.einsum('dbhvk,dbhk->dbhv', state, ka_t)
        state = (state * w_t[..., None, :] + sa[..., :, None] * kb_t[..., None, :]
                 + v_t[..., :, None] * k_t[..., None, :])
        return state, jnp.einsum('dbhvk,dbhk->dbhv', state, r_t)

    state0 = jnp.zeros((2, bsz, RWKV_HEADS, RWKV_HEAD, RWKV_HEAD), r.dtype)
    _, ys = lax.scan(step, state0, xs)
    y = jnp.sum(_flip_backward(jnp.moveaxis(ys, 0, 2)), axis=0)
    yf = y.astype(jnp.float32)
    mu = jnp.mean(yf, -1, keepdims=True)
    var = jnp.mean(jnp.square(yf - mu), -1, keepdims=True)
    yn = ((yf - mu) * lax.rsqrt(var + RWKV_GN_EPS)).reshape(bsz, seq, W_RWKV)
    yn = (yn * gn_g + gn_b).astype(xn.dtype)
    return (yn + bonus.reshape(bsz, seq, W_RWKV)) * g


def _short_conv_mixer(p_conv, conv_w):
    bgate, cgate, xc = jnp.split(p_conv, 3, axis=-1)
    z = cgate * xc
    zc = conv_w[0] * _shift_prev(z) + conv_w[1] * z + conv_w[2] * _shift_next(z)
    return bgate * zc


def _fourier_mixer(p_fft):
    bsz, seq, _ = p_fft.shape
    z = p_fft.reshape(bsz, seq, FFT_GROUPS, FFT_CH).astype(jnp.float32)
    f = jnp.fft.fft2(z, axes=(1, 3), norm='ortho').real
    return f.reshape(bsz, seq, W_FFT).astype(p_fft.dtype)


def setup_inputs(seed: int = 0) -> dict:
    key = jax.random.key(seed)
    keys = iter(jax.random.split(key, 48))
    L = DEPTH
    G, P, H = SSM_GROUPS, SSM_STATE, SSM_CH

    def nrm(shape, scale):
        return scale * jax.random.normal(next(keys), shape, jnp.float32)

    def unif(shape, lo, hi):
        return jax.random.uniform(next(keys), shape, jnp.float32, lo, hi)

    inp = {}
    inp['x'] = nrm((BATCH, SEQ, D_MODEL), 1.0)
    inp['ln0_g'] = 1.0 + nrm((D_MODEL,), 0.02)
    inp['ln0_b'] = nrm((D_MODEL,), 0.02)
    inp['w_in'] = nrm((L, D_MODEL, D_IN_PROJ), D_MODEL ** -0.5)
    inp['s5_lambda_re'] = -0.5 + nrm((L, 2, G, P), 0.01)
    inp['s5_lambda_im'] = math.pi * jnp.arange(P, dtype=jnp.float32) + nrm((L, 2, G, P), 0.01)
    inp['s5_log_dt'] = unif((L, 2, G), math.log(DT_MIN), math.log(DT_MAX))
    inp['s5_b_re'] = nrm((L, G, P, H), (2 * H) ** -0.5)
    inp['s5_b_im'] = nrm((L, G, P, H), (2 * H) ** -0.5)
    inp['s5_c_re'] = nrm((L, 2, G, H, P), P ** -0.5)
    inp['s5_c_im'] = nrm((L, 2, G, H, P), P ** -0.5)
    inp['s5_d'] = nrm((L, G, H), 1.0)
    inp['s5_glu_w'] = nrm((L, W_SSM, W_SSM), W_SSM ** -0.5)
    inp['s5_glu_b'] = nrm((L, W_SSM), 0.02)
    inp['rwkv_mu_rkv'] = unif((L, 2, 3 * W_RWKV), 0.0, 1.0)
    inp['rwkv_mu_w'] = unif((L, 2, D_MODEL), 0.0, 1.0)
    inp['rwkv_mu_a'] = unif((L, 2, D_MODEL), 0.0, 1.0)
    inp['rwkv_mu_g'] = unif((L, D_MODEL), 0.0, 1.0)
    inp['rwkv_w0'] = nrm((L, 2, W_RWKV), 0.5)
    inp['rwkv_w1'] = nrm((L, 2, D_MODEL, DECAY_LORA), D_MODEL ** -0.5)
    inp['rwkv_w2'] = nrm((L, 2, DECAY_LORA, W_RWKV), DECAY_LORA ** -0.5)
    inp['rwkv_a0'] = nrm((L, 2, W_RWKV), 0.1)
    inp['rwkv_a1'] = nrm((L, 2, D_MODEL, AAA_LORA), D_MODEL ** -0.5)
    inp['rwkv_a2'] = nrm((L, 2, AAA_LORA, W_RWKV), AAA_LORA ** -0.5)
    inp['rwkv_g1'] = nrm((L, D_MODEL, GATE_LORA), D_MODEL ** -0.5)
    inp['rwkv_g2'] = nrm((L, GATE_LORA, W_RWKV), GATE_LORA ** -0.5)
    inp['rwkv_k_k'] = 0.85 + nrm((L, W_RWKV), 0.02)
    inp['rwkv_k_a'] = 1.0 + nrm((L, W_RWKV), 0.02)
    inp['rwkv_r_k'] = nrm((L, RWKV_HEADS, RWKV_HEAD), 0.1)
    inp['rwkv_gn_g'] = 1.0 + nrm((L, W_RWKV), 0.02)
    inp['rwkv_gn_b'] = nrm((L, W_RWKV), 0.02)
    inp['conv_w'] = nrm((L, CONV_WIDTH, W_CONV), CONV_WIDTH ** -0.5)
    inp['w_out'] = nrm((L, D_MIX, D_MODEL), DEEPNORM_BETA * D_MIX ** -0.5)
    inp['ln1_g'] = 1.0 + nrm((L, D_MODEL), 0.02)
    inp['ln1_b'] = nrm((L, D_MODEL), 0.02)
    inp['ffn_w1'] = nrm((L, D_MODEL, D_FF), D_MODEL ** -0.5)
    inp['ffn_w3'] = nrm((L, D_MODEL, D_FF), D_MODEL ** -0.5)
    inp['ffn_w2'] = nrm((L, D_FF, D_MODEL), DEEPNORM_BETA * D_FF ** -0.5)
    inp['ln2_g'] = 1.0 + nrm((L, D_MODEL), 0.02)
    inp['ln2_b'] = nrm((L, D_MODEL), 0.02)
    return inp


def reference(x, ln0_g, ln0_b, w_in,
              s5_lambda_re, s5_lambda_im, s5_log_dt, s5_b_re, s5_b_im, s5_c_re, s5_c_im,
              s5_d, s5_glu_w, s5_glu_b,
              rwkv_mu_rkv, rwkv_mu_w, rwkv_mu_a, rwkv_mu_g, rwkv_w0, rwkv_w1, rwkv_w2,
              rwkv_a0, rwkv_a1, rwkv_a2, rwkv_g1, rwkv_g2, rwkv_k_k, rwkv_k_a, rwkv_r_k,
              rwkv_gn_g, rwkv_gn_b,
              conv_w, w_out, ln1_g, ln1_b, ffn_w1, ffn_w3, ffn_w2, ln2_g, ln2_b):
    h = _layer_norm(x, ln0_g, ln0_b)
    splits = [W_SSM, W_SSM + 3 * W_RWKV, W_SSM + 3 * W_RWKV + 3 * W_CONV]
    for l in range(DEPTH):
        p = h @ w_in[l]
        p_ssm, p_rkv, p_conv, p_fft = jnp.split(p, splits, axis=-1)
        y_a = _s5_mixer(p_ssm, s5_lambda_re[l], s5_lambda_im[l], s5_log_dt[l],
                        s5_b_re[l], s5_b_im[l], s5_c_re[l], s5_c_im[l], s5_d[l],
                        s5_glu_w[l], s5_glu_b[l])
        y_b = _rwkv7_mixer(h, p_rkv, rwkv_mu_rkv[l], rwkv_mu_w[l], rwkv_mu_a[l], rwkv_mu_g[l],
                           rwkv_w0[l], rwkv_w1[l], rwkv_w2[l], rwkv_a0[l], rwkv_a1[l], rwkv_a2[l],
                           rwkv_g1[l], rwkv_g2[l], rwkv_k_k[l], rwkv_k_a[l], rwkv_r_k[l],
                           rwkv_gn_g[l], rwkv_gn_b[l])
        y_c = _short_conv_mixer(p_conv, conv_w[l])
        y_d = _fourier_mixer(p_fft)
        y = jnp.concatenate([y_a, y_b, y_c, y_d], axis=-1)
        h = _layer_norm(DEEPNORM_ALPHA * h + y @ w_out[l], ln1_g[l], ln1_b[l])
        f = (jax.nn.silu(h @ ffn_w1[l]) * (h @ ffn_w3[l])) @ ffn_w2[l]
        h = _layer_norm(DEEPNORM_ALPHA * h + f, ln2_g[l], ln2_b[l])
    return h
```

```python
import functools
import math

import jax
import jax.numpy as jnp
import numpy as np
from jax import lax
from jax.experimental import pallas as pl
from jax.experimental.pallas import tpu as pltpu

W_SSM = 256
W_RWKV = 256
W_CONV = 256
W_FFT = 256
SSM_CH = 16
SSM_GROUPS = 16
SSM_STATE = 64
RWKV_HEAD = 64
RWKV_HEADS = 4
FFT_GROUPS = 4
FFT_CH = 64
RWKV_DECAY_SCALE = math.exp(-0.5)
RWKV_GN_EPS = 64e-5
LN_EPS = 1e-5
DEPTH = 2
DEEPNORM_ALPHA = (2 * DEPTH) ** 0.25

SUBLANES = 8
VMEM_LIMIT = 48 * 1024 * 1024

BF16 = jnp.bfloat16
F32 = jnp.float32


def _cparams(sem):
    return pltpu.CompilerParams(dimension_semantics=sem, vmem_limit_bytes=VMEM_LIMIT)


def _bdot(a, b):
    return jnp.dot(a.astype(BF16), b.astype(BF16), preferred_element_type=F32)


def _layer_norm(x, g, b):
    mu = jnp.mean(x, axis=-1, keepdims=True)
    xc = x - mu
    var = jnp.mean(xc * xc, axis=-1, keepdims=True)
    return xc * lax.rsqrt(var + LN_EPS) * g + b


def _shift_rows(x, prev_row, next_row):
    n = x.shape[0]
    rows = lax.broadcasted_iota(jnp.int32, x.shape, 0)
    x_prev = jnp.where(rows == 0, prev_row, pltpu.roll(x, 1, axis=0))
    x_next = jnp.where(rows == n - 1, next_row, pltpu.roll(x, n - 1, axis=0))
    return x_prev, x_next


def _halo_rows(prev_ref, next_ref, row0, n_rows, seq):
    first = (row0 % seq) == 0
    last = ((row0 + n_rows) % seq) == 0
    prev_row = jnp.where(first, 0.0, prev_ref[SUBLANES - 1:SUBLANES, :])
    next_row = jnp.where(last, 0.0, next_ref[0:1, :])
    return prev_row, next_row


def _halo_specs(tile, width, n_rows_total):
    per = tile // SUBLANES
    last_blk = n_rows_total // SUBLANES - 1
    prev = pl.BlockSpec((SUBLANES, width), lambda i: (jnp.maximum(i * per - 1, 0), 0))
    nxt = pl.BlockSpec((SUBLANES, width), lambda i: (jnp.minimum((i + 1) * per, last_blk), 0))
    return prev, nxt


def _ln0_body(x_ref, g_ref, b_ref, o_ref):
    o_ref[...] = _layer_norm(x_ref[...], g_ref[...], b_ref[...])


def _ln0(x2, g, b, tile):
    n, d = x2.shape
    return pl.pallas_call(
        _ln0_body,
        grid=(n // tile,),
        in_specs=[pl.BlockSpec((tile, d), lambda i: (i, 0)),
                  pl.BlockSpec((1, d), lambda i: (0, 0)),
                  pl.BlockSpec((1, d), lambda i: (0, 0))],
        out_specs=pl.BlockSpec((tile, d), lambda i: (i, 0)),
        out_shape=jax.ShapeDtypeStruct((n, d), F32),
        compiler_params=_cparams(("parallel",)),
        name="ln0",
    )(x2, g.reshape(1, d), b.reshape(1, d))


def _inproj_body(seq, tile, h_ref, hp_ref, hn_ref, win_ref, muw_ref, mua_ref, mug_ref,
                 w1_ref, w2_ref, w0_ref, a1_ref, a2_ref, a0_ref, g1_ref, g2_ref,
                 pssm_ref, prkv_ref, pconv_ref, pfft_ref, logw_ref, agate_ref, g_ref):
    i = pl.program_id(0)
    h = h_ref[...]
    prev_row, next_row = _halo_rows(hp_ref, hn_ref, i * tile, tile, seq)
    x_prev, x_next = _shift_rows(h, prev_row, next_row)

    p = _bdot(h, win_ref[...])
    pssm_ref[...] = p[:, :W_SSM]
    prkv_ref[...] = p[:, W_SSM:W_SSM + 3 * W_RWKV]
    pconv_ref[...] = p[:, W_SSM + 3 * W_RWKV:W_SSM + 3 * W_RWKV + 3 * W_CONV]
    pfft_ref[...] = p[:, W_SSM + 3 * W_RWKV + 3 * W_CONV:]

    for d, x_sh in enumerate((x_prev, x_next)):
        dx = x_sh - h
        xw = h + dx * muw_ref[d:d + 1, :]
        xa = h + dx * mua_ref[d:d + 1, :]
        w_lora = _bdot(jnp.tanh(_bdot(xw, w1_ref[d])), w2_ref[d])
        logw_ref[d] = -RWKV_DECAY_SCALE * jax.nn.sigmoid(w0_ref[d:d + 1, :] + w_lora)
        a_lora = _bdot(_bdot(xa, a1_ref[d]), a2_ref[d])
        agate_ref[d] = jax.nn.sigmoid(a0_ref[d:d + 1, :] + a_lora)
    xg = h + (0.5 * (x_prev + x_next) - h) * mug_ref[...]
    g_ref[...] = _bdot(jax.nn.sigmoid(_bdot(xg, g1_ref[...])), g2_ref[...])


def _inproj(h, seq, tile, w_in, mu_w, mu_a, mu_g, w1, w2, w0, a1, a2, a0, g1, g2):
    n, d = h.shape
    hp_spec, hn_spec = _halo_specs(tile, d, n)

    def full(a):
        nd = a.ndim
        return pl.BlockSpec(a.shape, lambda i: (0,) * nd)

    row = lambda w: pl.BlockSpec((tile, w), lambda i: (i, 0))
    row2 = lambda w: pl.BlockSpec((2, tile, w), lambda i: (0, i, 0))
    consts = (w_in.astype(BF16), mu_w, mu_a, mu_g.reshape(1, d), w1.astype(BF16), w2.astype(BF16), w0,
              a1.astype(BF16), a2.astype(BF16), a0, g1.astype(BF16), g2.astype(BF16))
    return pl.pallas_call(
        functools.partial(_inproj_body, seq, tile),
        grid=(n // tile,),
        in_specs=[pl.BlockSpec((tile, d), lambda i: (i, 0)), hp_spec, hn_spec] + [full(c) for c in consts],
        out_specs=[row(W_SSM), row(3 * W_RWKV), row(3 * W_CONV), row(W_FFT),
                   row2(W_RWKV), row2(W_RWKV), row(W_RWKV)],
        out_shape=[jax.ShapeDtypeStruct((n, W_SSM), F32),
                   jax.ShapeDtypeStruct((n, 3 * W_RWKV), F32),
                   jax.ShapeDtypeStruct((n, 3 * W_CONV), F32),
                   jax.ShapeDtypeStruct((n, W_FFT), F32),
                   jax.ShapeDtypeStruct((2, n, W_RWKV), F32),
                   jax.ShapeDtypeStruct((2, n, W_RWKV), F32),
                   jax.ShapeDtypeStruct((n, W_RWKV), F32)],
        compiler_params=_cparams(("parallel",)),
        name="inproj",
    )(h, h, h, *consts)


S5_SEGS = SUBLANES


def _cmul(ar, ai, br, bi):
    return ar * br - ai * bi, ar * bi + ai * br


def _s5_body(tl, n_tiles, seg_len, u_ref, bb_ref, a_ref, c_ref, y_ref, bu_ref, st_ref):
    d = pl.program_id(1)
    ps = pl.program_id(2)
    i = pl.program_id(3)
    half = SSM_GROUPS * SSM_STATE
    shape = (S5_SEGS, half)
    ar = jnp.broadcast_to(a_ref[0, :, :half], shape)
    ai = jnp.broadcast_to(a_ref[0, :, half:], shape)

    @pl.when(jnp.logical_and(i == 0, ps == 0))
    def _():
        st_ref[...] = jnp.zeros_like(st_ref)

    @pl.when(jnp.logical_and(i == 0, ps == 1))
    def _():
        pr, pi = jnp.ones(shape, F32), jnp.zeros(shape, F32)
        br, bi = ar, ai
        e = seg_len
        while e:
            if e & 1:
                pr, pi = _cmul(pr, pi, br, bi)
            br, bi = _cmul(br, bi, br, bi)
            e >>= 1
        er, ei = st_ref[:, :half], st_ref[:, half:]
        zero = jnp.zeros((1, half), F32)
        fr, fi = [zero], [zero]
        for j in range(1, S5_SEGS):
            nr, ni = _cmul(pr[0:1], pi[0:1], fr[-1], fi[-1])
            fr.append(nr + er[j - 1:j])
            fi.append(ni + ei[j - 1:j])
        rr, ri = [zero], [zero]
        for j in range(S5_SEGS - 2, -1, -1):
            nr, ni = _cmul(pr[0:1], pi[0:1], rr[0], ri[0])
            rr.insert(0, nr + er[j + 1:j + 2])
            ri.insert(0, ni + ei[j + 1:j + 2])
        fwd = d == 0
        st_ref[:, :half] = jnp.where(fwd, jnp.concatenate(fr, axis=0), jnp.concatenate(rr, axis=0))
        st_ref[:, half:] = jnp.where(fwd, jnp.concatenate(fi, axis=0), jnp.concatenate(ri, axis=0))

    bu_ref[...] = _bdot(u_ref[0], bb_ref[0])

    def scan(store):
        def step(t, carry):
            xr, xi = carry
            row = pl.multiple_of((t + d * (tl - 1 - 2 * t)) * S5_SEGS, S5_SEGS)
            nr = ar * xr - ai * xi + bu_ref[pl.ds(row, S5_SEGS), :half]
            ni = ar * xi + ai * xr + bu_ref[pl.ds(row, S5_SEGS), half:]
            if store:
                bu_ref[pl.ds(row, S5_SEGS), :half] = nr
                bu_ref[pl.ds(row, S5_SEGS), half:] = ni
            return nr, ni

        xr, xi = lax.fori_loop(0, tl, step, (st_ref[:, :half], st_ref[:, half:]))
        st_ref[:, :half] = xr
        st_ref[:, half:] = xi

    @pl.when(ps == 0)
    def _():
        scan(False)

    @pl.when(ps == 1)
    def _():
        scan(True)
        y_ref[0, 0] = _bdot(bu_ref[...], c_ref[0])


def _s5_params(lam_re, lam_im, log_dt, b_re, b_im, c_re, c_im):
    g, p, hch = SSM_GROUPS, SSM_STATE, SSM_CH
    dt = jnp.exp(log_dt)[..., None]
    mag = jnp.exp(lam_re * dt)
    lb_re = mag * jnp.cos(lam_im * dt)
    lb_im = mag * jnp.sin(lam_im * dt)
    den = lam_re * lam_re + lam_im * lam_im
    nr = lb_re - 1.0
    coef_re = (nr * lam_re + lb_im * lam_im) / den
    coef_im = (lb_im * lam_re - nr * lam_im) / den
    bb_re = coef_re[..., None] * b_re - coef_im[..., None] * b_im
    bb_im = coef_re[..., None] * b_im + coef_im[..., None] * b_re
    eye = jnp.eye(g, dtype=F32)
    bd_in = lambda m: jnp.einsum('dgph,gk->dghkp', m, eye).reshape(2, g * hch, g * p)
    bb = jnp.concatenate([bd_in(bb_re), bd_in(bb_im)], axis=-1)
    bd_out = lambda m: jnp.einsum('dghp,gk->dgpkh', m, eye).reshape(2, g * p, g * hch)
    cc = jnp.concatenate([bd_out(c_re), -bd_out(c_im)], axis=1)
    a = jnp.concatenate([lb_re.reshape(2, 1, g * p), lb_im.reshape(2, 1, g * p)], axis=-1)
    return bb, a, cc


def _s5_scan(p_ssm, bsz, seq, bb, a, cc, tl):
    seg_len = seq // S5_SEGS
    n_tiles = seg_len // tl
    rows = tl * S5_SEGS
    w2 = 2 * SSM_GROUPS * SSM_STATE
    u_r = p_ssm.reshape(bsz, S5_SEGS, seg_len, W_SSM).transpose(0, 2, 1, 3).reshape(bsz, seq, W_SSM)
    tile_of = lambda d, i: jnp.where(d == 0, i, n_tiles - 1 - i)
    y_r = pl.pallas_call(
        functools.partial(_s5_body, tl, n_tiles, seg_len),
        grid=(bsz, 2, 2, n_tiles),
        in_specs=[pl.BlockSpec((1, rows, W_SSM), lambda b, d, ps, i: (b, tile_of(d, i), 0)),
                  pl.BlockSpec((1, W_SSM, w2), lambda b, d, ps, i: (d, 0, 0)),
                  pl.BlockSpec((1, 1, w2), lambda b, d, ps, i: (d, 0, 0)),
                  pl.BlockSpec((1, w2, W_SSM), lambda b, d, ps, i: (d, 0, 0))],
        out_specs=pl.BlockSpec((1, 1, rows, W_SSM),
                               lambda b, d, ps, i: (d, b, tile_of(d, jnp.where(ps == 0, 0, i)), 0)),
        out_shape=jax.ShapeDtypeStruct((2, bsz, seq, W_SSM), F32),
        scratch_shapes=[pltpu.VMEM((rows, w2), F32), pltpu.VMEM((S5_SEGS, w2), F32)],
        compiler_params=_cparams(("arbitrary",) * 4),
        name="s5_scan",
    )(u_r, bb.astype(BF16), a, cc.astype(BF16))
    y = y_r.reshape(2, bsz, seg_len, S5_SEGS, W_SSM).transpose(0, 1, 3, 2, 4)
    return y.reshape(2, bsz * seq, W_SSM)


def _gelu_tanh(x):
    c = math.sqrt(2.0 / math.pi)
    return 0.5 * x * (1.0 + jnp.tanh(c * (x + 0.044715 * (x * x * x))))


def _s5_post(y2, u, dskip, glu_w, glu_b):
    y = _gelu_tanh(y2[0] + y2[1] + dskip * u)
    return y * jax.nn.sigmoid(_bdot(y, glu_w) + glu_b)


def _s5_test(p_ssm, bsz, seq, inp, l):
    bb, a, cc = _s5_params(inp['s5_lambda_re'][l], inp['s5_lambda_im'][l], inp['s5_log_dt'][l],
                           inp['s5_b_re'][l], inp['s5_b_im'][l], inp['s5_c_re'][l], inp['s5_c_im'][l])
    y2 = _s5_scan(p_ssm, bsz, seq, bb, a, cc, 16)
    return _s5_post(y2, p_ssm, inp['s5_d'][l].reshape(1, -1), inp['s5_glu_w'][l], inp['s5_glu_b'][l].reshape(1, -1))


def _split_dot(x, w):
    hi = x.astype(BF16)
    lo = (x - hi.astype(F32)).astype(BF16)
    return (jnp.dot(hi, w, preferred_element_type=F32) + jnp.dot(lo, w, preferred_element_type=F32))


def _dot_nt(a, b):
    return lax.dot_general(a.astype(BF16), b.astype(BF16), (((1,), (1,)), ((), ())),
                           preferred_element_type=F32)


def _rwkv_masks(chunk):
    hh = RWKV_HEADS
    r = np.arange(hh * chunk)[:, None] // chunk
    bm_feat = (r == np.arange(W_RWKV)[None, :] // RWKV_HEAD).astype(np.float32)
    bm_time = (r == np.arange(hh * chunk)[None, :] // chunk).astype(np.float32)
    f = np.arange(W_RWKV)
    bm_head = (f[:, None] // RWKV_HEAD == f[None, :] // RWKV_HEAD).astype(np.float32)
    return bm_feat, bm_time, bm_head


def _rwkv_body(chunk, n_chunks, seq, x_ref, xp_ref, xn_ref, logw_ref, a_ref, mu_ref, kk_ref, ka_ref, rk_ref,
               bmf_ref, bmt_ref, bmh_ref, y_ref, bonus_ref, h_ref):
    d = pl.program_id(0)
    b = pl.program_id(1)
    c = pl.program_id(2)
    fwd = d == 0
    ll = chunk
    w = W_RWKV
    cidx = jnp.where(fwd, c, n_chunks - 1 - c)

    @pl.when(c == 0)
    def _():
        h_ref[...] = jnp.zeros_like(h_ref)

    x = x_ref[...]
    prev_row, next_row = _halo_rows(xp_ref, xn_ref, b * seq + cidx * ll, ll, seq)
    x_prev, x_next = _shift_rows(x, prev_row, next_row)
    rkv = x + (jnp.where(fwd, x_prev, x_next) - x) * mu_ref[0]
    r, k, v = rkv[:, :w], rkv[:, w:2 * w], rkv[:, 2 * w:]
    logw = logw_ref[0]
    a = a_ref[0]
    bmf = bmf_ref[...]
    bmt = bmt_ref[...]
    bmh = bmh_ref[...]
    bmh16 = bmh.astype(BF16)

    kk = k * kk_ref[...]
    kk = kk * lax.rsqrt(_split_dot(kk * kk, bmh16) + 1e-12)
    k2 = k * (1.0 + (a - 1.0) * ka_ref[...])
    bonus_ref[0] = _split_dot(r * k2 * rk_ref[...], bmh16) * v

    ti = lax.broadcasted_iota(jnp.int32, (ll, ll), 0)
    si = lax.broadcasted_iota(jnp.int32, (ll, ll), 1)
    sgn = 1 - 2 * d
    tri = (sgn * (ti - si) >= 0).astype(F32).astype(BF16)
    cum = _split_dot_left(tri, logw)
    ctot = jnp.sum(logw, axis=0, keepdims=True)
    e_pos = jnp.exp(cum)
    e_neg = jnp.exp(-cum)
    e_rem = jnp.exp(ctot - cum)
    ah = -kk * jnp.exp(cum - logw)
    rh = r * e_pos
    bvec = kk * a
    bh, kh = bvec * e_neg, k2 * e_neg
    bt, kt = bvec * e_rem, k2 * e_rem

    def bd(m):
        return jnp.concatenate([m] * RWKV_HEADS, axis=0) * (bmf if m.shape[1] == w else bmt)

    gram = _dot_nt(jnp.concatenate([ah, rh], axis=0), jnp.concatenate([bd(bh), bd(kh)], axis=0))
    l4 = RWKV_HEADS * ll
    tt = lax.broadcasted_iota(jnp.int32, (ll, l4), 0)
    ss = lax.broadcasted_iota(jnp.int32, (ll, l4), 1) % ll
    strict = sgn * (tt - ss) > 0
    incl = sgn * (tt - ss) >= 0
    n_ab = jnp.where(strict, gram[:ll, :l4], 0.0)
    a_ak = jnp.where(strict, gram[:ll, l4:], 0.0)
    a_rb = jnp.where(incl, gram[ll:, :l4], 0.0)
    a_rk = jnp.where(incl, gram[ll:, l4:], 0.0)

    pw = n_ab
    tinv = jnp.where(ss == tt, 1.0, 0.0) + n_ab
    steps = int(math.log2(ll)) - 1
    for _ in range(steps):
        pw = _bdot(pw, bd(pw))
        tinv = tinv + _bdot(tinv, bd(pw))

    akv = _bdot(a_ak, bd(v))
    ta = _bdot(tinv, jnp.concatenate([bd(ah), bd(akv)], axis=1))
    ap, wm = ta[:, :w], ta[:, w:]
    rp = rh + _bdot(a_rb, bd(ap))
    y0 = _bdot(jnp.concatenate([a_rb, a_rk], axis=1), jnp.concatenate([bd(wm), bd(v)], axis=0))
    btk = jnp.transpose(jnp.concatenate([bt, kt], axis=0))
    rhs = jnp.concatenate([jnp.concatenate([ap, wm], axis=1),
                           jnp.concatenate([jnp.zeros_like(v), v], axis=1)], axis=0)
    pq = _bdot(btk, rhs)
    eye = (lax.broadcasted_iota(jnp.int32, (w, w), 0) == lax.broadcasted_iota(jnp.int32, (w, w), 1))
    pm = pq[:, :w] * bmh + jnp.where(eye, jnp.exp(ctot), 0.0)
    qm = pq[:, w:] * bmh

    h_old = h_ref[...]
    y_ref[0] = _bdot(rp, h_old) + y0
    h_ref[...] = _bdot(pm, h_old) + qm


def _split_dot_left(w, x):
    hi = x.astype(BF16)
    lo = (x - hi.astype(F32)).astype(BF16)
    return (jnp.dot(w, hi, preferred_element_type=F32) + jnp.dot(w, lo, preferred_element_type=F32))


def _rwkv_scan(p_rkv, logw, agate, bsz, seq, chunk, mu_rkv, k_k, k_a, r_k):
    n = bsz * seq
    n_chunks = seq // chunk
    per = chunk // SUBLANES
    last_blk = n // SUBLANES - 1
    w3 = 3 * W_RWKV
    blk = lambda d, b, c: b * n_chunks + jnp.where(d == 0, c, n_chunks - 1 - c)
    masks = [jnp.asarray(m) for m in _rwkv_masks(chunk)]
    row = pl.BlockSpec((1, chunk, W_RWKV), lambda d, b, c: (d, blk(d, b, c), 0))
    vec = pl.BlockSpec((1, W_RWKV), lambda d, b, c: (0, 0))
    return pl.pallas_call(
        functools.partial(_rwkv_body, chunk, n_chunks, seq),
        grid=(2, bsz, n_chunks),
        in_specs=[pl.BlockSpec((chunk, w3), lambda d, b, c: (blk(d, b, c), 0)),
                  pl.BlockSpec((SUBLANES, w3), lambda d, b, c: (jnp.maximum(blk(d, b, c) * per - 1, 0), 0)),
                  pl.BlockSpec((SUBLANES, w3), lambda d, b, c: (jnp.minimum((blk(d, b, c) + 1) * per, last_blk), 0)),
                  row, row,
                  pl.BlockSpec((1, 1, w3), lambda d, b, c: (d, 0, 0)),
                  vec, vec, vec] + [pl.BlockSpec(m.shape, lambda d, b, c: (0, 0)) for m in masks],
        out_specs=[row, row],
        out_shape=[jax.ShapeDtypeStruct((2, n, W_RWKV), F32)] * 2,
        scratch_shapes=[pltpu.VMEM((W_RWKV, W_RWKV), F32)],
        compiler_params=_cparams(("arbitrary",) * 3),
        name="rwkv_scan",
    )(p_rkv, p_rkv, p_rkv, logw, agate, mu_rkv.reshape(2, 1, w3), k_k.reshape(1, -1), k_a.reshape(1, -1),
      r_k.reshape(1, -1), *masks)


def _head_mean(x, bmh16):
    return _split_dot(x, bmh16) * (1.0 / RWKV_HEAD)


def _rwkv_post(y2, bonus2, g, gn_g, gn_b, bmh16):
    y = y2[0] + y2[1]
    mu = _head_mean(y, bmh16)
    yc = y - mu
    var = _head_mean(yc * yc, bmh16)
    yn = yc * lax.rsqrt(var + RWKV_GN_EPS) * gn_g + gn_b
    return (yn + bonus2[0] + bonus2[1]) * g


FFT_R1 = 64


def _fft_tables(seq):
    r1, r2 = FFT_R1, seq // FFT_R1
    i1 = jnp.arange(r1, dtype=jnp.int32)
    ang1 = ((i1[:, None] * i1[None, :]) % r1).astype(F32) * (2.0 * math.pi / r1)
    stage1 = jnp.concatenate([jnp.cos(ang1), -jnp.sin(ang1)], axis=0)
    k1 = i1[:, None, None]
    k2 = jnp.arange(r2, dtype=jnp.int32)[None, :, None]
    n2 = jnp.arange(r2, dtype=jnp.int32)[None, None, :]
    ang2 = ((n2 * (k1 + r1 * k2)) % seq).astype(F32) * (2.0 * math.pi / seq)
    mr, mi = jnp.cos(ang2), -jnp.sin(ang2)
    stage2 = jnp.concatenate([jnp.concatenate([mr, -mi], axis=2),
                              jnp.concatenate([mi, mr], axis=2)], axis=1)
    c = jnp.arange(W_FFT, dtype=jnp.int32)
    same = (c[:, None] // FFT_CH) == (c[None, :] // FFT_CH)
    angc = ((c[:, None] * c[None, :]) % FFT_CH).astype(F32) * (2.0 * math.pi / FFT_CH)
    scale = 1.0 / math.sqrt(seq * FFT_CH)
    chan = jnp.concatenate([jnp.where(same, jnp.cos(angc), 0.0), jnp.where(same, jnp.sin(angc), 0.0)],
                           axis=0) * scale
    return stage1, stage2, chan


def _fft1_body(z_ref, m_ref, are_ref, aim_ref):
    r1 = FFT_R1
    a = _bdot(m_ref[...], z_ref[0])
    are_ref[0] = a[:r1]
    aim_ref[0] = a[r1:]


def _fft2_body(kb, r2, are_ref, aim_ref, m_ref, chan_ref, o_ref):
    for j in range(kb):
        x = jnp.concatenate([are_ref[0, j], aim_ref[0, j]], axis=0)
        f = _bdot(m_ref[j], x)
        fri = jnp.concatenate([f[:r2], f[r2:]], axis=1)
        o_ref[0, :, j * W_FFT:(j + 1) * W_FFT] = _bdot(fri, chan_ref[...])


def _fourier(p_fft, bsz, seq, tables=None):
    r1, r2 = FFT_R1, seq // FFT_R1
    stage1, stage2, chan = tables if tables is not None else _fft_tables(seq)
    cols = r2 * W_FFT
    tc = min(cols, 4096)
    z = p_fft.reshape(bsz, r1, cols)
    blk = pl.BlockSpec((1, r1, tc), lambda b, j: (b, 0, j))
    a_re, a_im = pl.pallas_call(
        _fft1_body,
        grid=(bsz, cols // tc),
        in_specs=[blk, pl.BlockSpec((2 * r1, r1), lambda b, j: (0, 0))],
        out_specs=[blk, blk],
        out_shape=[jax.ShapeDtypeStruct((bsz, r1, cols), F32)] * 2,
        compiler_params=_cparams(("parallel", "parallel")),
        name="fft_stage1",
    )(z, stage1.astype(BF16))
    kb = 8
    ablk = pl.BlockSpec((1, kb, r2, W_FFT), lambda b, j: (b, j, 0, 0))
    out = pl.pallas_call(
        functools.partial(_fft2_body, kb, r2),
        grid=(bsz, r1 // kb),
        in_specs=[ablk, ablk,
                  pl.BlockSpec((kb, 2 * r2, 2 * r2), lambda b, j: (j, 0, 0)),
                  pl.BlockSpec((2 * W_FFT, W_FFT), lambda b, j: (0, 0))],
        out_specs=pl.BlockSpec((1, r2, kb * W_FFT), lambda b, j: (b, 0, j)),
        out_shape=jax.ShapeDtypeStruct((bsz, r2, r1 * W_FFT), F32),
        compiler_params=_cparams(("parallel", "parallel")),
        name="fft_stage2",
    )(a_re.reshape(bsz, r1, r2, W_FFT), a_im.reshape(bsz, r1, r2, W_FFT), stage2.astype(BF16), chan.astype(BF16))
    return out.reshape(bsz * seq, W_FFT)


def _mixout_body(seq, tile, h_ref, ys5_ref, pssm_ref, yrw_ref, bonus_ref, g_ref, pc_ref, pcp_ref, pcn_ref, yf_ref,
                 wout_ref, dskip_ref, gluw_ref, glub_ref, gng_ref, gnb_ref, convw_ref, bmh_ref, lng_ref, lnb_ref,
                 o_ref):
    i = pl.program_id(0)
    y_a = _s5_post(ys5_ref, pssm_ref[...], dskip_ref[...], gluw_ref[...], glub_ref[...])
    y_b = _rwkv_post(yrw_ref, bonus_ref, g_ref[...], gng_ref[...], gnb_ref[...], bmh_ref[...])
    pc = pc_ref[...]
    wc = W_CONV
    prev_row, next_row = _halo_rows(pcp_ref, pcn_ref, i * tile, tile, seq)
    z = pc[:, wc:2 * wc] * pc[:, 2 * wc:]
    z_prev, z_next = _shift_rows(z, prev_row[:, wc:2 * wc] * prev_row[:, 2 * wc:],
                                 next_row[:, wc:2 * wc] * next_row[:, 2 * wc:])
    y_c = pc[:, :wc] * (convw_ref[0:1, :] * z_prev + convw_ref[1:2, :] * z + convw_ref[2:3, :] * z_next)
    mix = (_bdot(y_a, wout_ref[0:W_SSM, :]) + _bdot(y_b, wout_ref[W_SSM:W_SSM + W_RWKV, :])
           + _bdot(y_c, wout_ref[W_SSM + W_RWKV:W_SSM + W_RWKV + W_CONV, :])
           + _bdot(yf_ref[...], wout_ref[W_SSM + W_RWKV + W_CONV:, :]))
    o_ref[...] = _layer_norm(DEEPNORM_ALPHA * h_ref[...] + mix, lng_ref[...], lnb_ref[...])


def _mixout(h, seq, tile, y_s5, p_ssm, y_rw, bonus, g, p_conv, y_fft, w_out, dskip, glu_w, glu_b, gn_g, gn_b,
            conv_w, ln_g, ln_b):
    n, d = h.shape
    row = lambda w: pl.BlockSpec((tile, w), lambda i: (i, 0))
    row2 = lambda w: pl.BlockSpec((2, tile, w), lambda i: (0, i, 0))
    pcp, pcn = _halo_specs(tile, 3 * W_CONV, n)
    bmh16 = jnp.asarray(_rwkv_masks(RWKV_HEAD)[2]).astype(BF16)
    consts = (w_out.astype(BF16), dskip.reshape(1, -1), glu_w.astype(BF16), glu_b.reshape(1, -1),
              gn_g.reshape(1, -1), gn_b.reshape(1, -1), conv_w, bmh16, ln_g.reshape(1, -1), ln_b.reshape(1, -1))

    def full(a):
        nd = a.ndim
        return pl.BlockSpec(a.shape, lambda i: (0,) * nd)

    return pl.pallas_call(
        functools.partial(_mixout_body, seq, tile),
        grid=(n // tile,),
        in_specs=[row(d), row2(W_SSM), row(W_SSM), row2(W_RWKV), row2(W_RWKV), row(W_RWKV),
                  row(3 * W_CONV), pcp, pcn, row(W_FFT)] + [full(c) for c in consts],
        out_specs=row(d),
        out_shape=jax.ShapeDtypeStruct((n, d), F32),
        compiler_params=_cparams(("parallel",)),
        name="mix_out",
    )(h, y_s5, p_ssm, y_rw, bonus, g, p_conv, p_conv, p_conv, y_fft, *consts)


def _ffn_body(n_ff, h_ref, w1_ref, w3_ref, w2_ref, lng_ref, lnb_ref, o_ref, acc_ref):
    j = pl.program_id(1)

    @pl.when(j == 0)
    def _():
        acc_ref[...] = jnp.zeros_like(acc_ref)

    x = h_ref[...].astype(BF16)
    u = jax.nn.silu(jnp.dot(x, w1_ref[...], preferred_element_type=F32)) * jnp.dot(
        x, w3_ref[...], preferred_element_type=F32)
    acc_ref[...] += _bdot(u, w2_ref[...])

    @pl.when(j == n_ff - 1)
    def _():
        o_ref[...] = _layer_norm(DEEPNORM_ALPHA * h_ref[...] + acc_ref[...], lng_ref[...], lnb_ref[...])


def _ffn(h, tile, tf, w1, w3, w2, ln_g, ln_b):
    n, d = h.shape
    dff = w1.shape[1]
    n_ff = dff // tf
    return pl.pallas_call(
        functools.partial(_ffn_body, n_ff),
        grid=(n // tile, n_ff),
        in_specs=[pl.BlockSpec((tile, d), lambda i, j: (i, 0)),
                  pl.BlockSpec((d, tf), lambda i, j: (0, j)),
                  pl.BlockSpec((d, tf), lambda i, j: (0, j)),
                  pl.BlockSpec((tf, d), lambda i, j: (j, 0)),
                  pl.BlockSpec((1, d), lambda i, j: (0, 0)),
                  pl.BlockSpec((1, d), lambda i, j: (0, 0))],
        out_specs=pl.BlockSpec((tile, d), lambda i, j: (i, 0)),
        out_shape=jax.ShapeDtypeStruct((n, d), F32),
        scratch_shapes=[pltpu.VMEM((tile, d), F32)],
        compiler_params=_cparams(("parallel", "arbitrary")),
        name="ffn",
    )(h, w1.astype(BF16), w3.astype(BF16), w2.astype(BF16), ln_g.reshape(1, d), ln_b.reshape(1, d))


ROW_TILE = 512
S5_TILE_STEPS = 64
RWKV_CHUNK = 64
FFN_COLS = 1408


def kernel(x, ln0_g, ln0_b, w_in, s5_lambda_re, s5_lambda_im, s5_log_dt, s5_b_re, s5_b_im, s5_c_re, s5_c_im, s5_d,
           s5_glu_w, s5_glu_b, rwkv_mu_rkv, rwkv_mu_w, rwkv_mu_a, rwkv_mu_g, rwkv_w0, rwkv_w1, rwkv_w2, rwkv_a0,
           rwkv_a1, rwkv_a2, rwkv_g1, rwkv_g2, rwkv_k_k, rwkv_k_a, rwkv_r_k, rwkv_gn_g, rwkv_gn_b, conv_w, w_out,
           ln1_g, ln1_b, ffn_w1, ffn_w3, ffn_w2, ln2_g, ln2_b):
    bsz, seq, d = x.shape
    n = bsz * seq
    tile = min(ROW_TILE, seq)
    s5_tl = min(S5_TILE_STEPS, seq // S5_SEGS)
    fft_tables = _fft_tables(seq)
    h = _ln0(x.reshape(n, d), ln0_g, ln0_b, tile)
    for l in range(w_in.shape[0]):
        p_ssm, p_rkv, p_conv, p_fft, logw, agate, g = _inproj(
            h, seq, tile, w_in[l], rwkv_mu_w[l], rwkv_mu_a[l], rwkv_mu_g[l], rwkv_w1[l], rwkv_w2[l], rwkv_w0[l],
            rwkv_a1[l], rwkv_a2[l], rwkv_a0[l], rwkv_g1[l], rwkv_g2[l])
        bb, a, cc = _s5_params(s5_lambda_re[l], s5_lambda_im[l], s5_log_dt[l], s5_b_re[l], s5_b_im[l],
                               s5_c_re[l], s5_c_im[l])
        y_s5 = _s5_scan(p_ssm, bsz, seq, bb, a, cc, s5_tl)
        y_rw, bonus = _rwkv_scan(p_rkv, logw, agate, bsz, seq, RWKV_CHUNK, rwkv_mu_rkv[l], rwkv_k_k[l],
                                 rwkv_k_a[l], rwkv_r_k[l].reshape(-1))
        y_fft = _fourier(p_fft, bsz, seq, fft_tables)
        h = _mixout(h, seq, tile, y_s5, p_ssm, y_rw, bonus, g, p_conv, y_fft, w_out[l], s5_d[l], s5_glu_w[l],
                    s5_glu_b[l], rwkv_gn_g[l], rwkv_gn_b[l], conv_w[l], ln1_g[l], ln1_b[l])
        h = _ffn(h, tile, FFN_COLS, ffn_w1[l], ffn_w3[l], ffn_w2[l], ln2_g[l], ln2_b[l])
    return h.reshape(bsz, seq, d)


def _rwkv_test(h, p_rkv, bsz, seq, inp, l):
    outs = _inproj(h, seq, 256, inp['w_in'][l], inp['rwkv_mu_w'][l], inp['rwkv_mu_a'][l],
                   inp['rwkv_mu_g'][l], inp['rwkv_w1'][l], inp['rwkv_w2'][l], inp['rwkv_w0'][l],
                   inp['rwkv_a1'][l], inp['rwkv_a2'][l], inp['rwkv_a0'][l], inp['rwkv_g1'][l], inp['rwkv_g2'][l])
    _, prkv, _, _, logw, agate, g = outs
    y2, bonus2 = _rwkv_scan(prkv, logw, agate, bsz, seq, 64, inp['rwkv_mu_rkv'][l], inp['rwkv_k_k'][l],
                            inp['rwkv_k_a'][l], inp['rwkv_r_k'][l])
    bmh16 = jnp.asarray(_rwkv_masks(64)[2]).astype(BF16)
    return _rwkv_post(y2, bonus2, g, inp['rwkv_gn_g'][l].reshape(1, -1), inp['rwkv_gn_b'][l].reshape(1, -1), bmh16)
```

```python
import functools
import math

import jax
import jax.numpy as jnp
import numpy as np
from jax import lax
from jax.experimental import pallas as pl
from jax.experimental.pallas import tpu as pltpu

W_SSM = 256
W_RWKV = 256
W_CONV = 256
W_FFT = 256
SSM_CH = 16
SSM_GROUPS = 16
SSM_STATE = 64
RWKV_HEAD = 64
RWKV_HEADS = 4
FFT_GROUPS = 4
FFT_CH = 64
RWKV_DECAY_SCALE = math.exp(-0.5)
RWKV_GN_EPS = 64e-5
LN_EPS = 1e-5
DEPTH = 2
DEEPNORM_ALPHA = (2 * DEPTH) ** 0.25

SUBLANES = 8
VMEM_LIMIT = 48 * 1024 * 1024

BF16 = jnp.bfloat16
F32 = jnp.float32


def _cparams(sem):
    return pltpu.CompilerParams(dimension_semantics=sem, vmem_limit_bytes=VMEM_LIMIT)


def _bdot(a, b):
    return jnp.dot(a.astype(BF16), b.astype(BF16), preferred_element_type=F32)


def _layer_norm(x, g, b):
    mu = jnp.mean(x, axis=-1, keepdims=True)
    xc = x - mu
    var = jnp.mean(xc * xc, axis=-1, keepdims=True)
    return xc * lax.rsqrt(var + LN_EPS) * g + b


def _shift_rows(x, prev_row, next_row):
    n = x.shape[0]
    rows = lax.broadcasted_iota(jnp.int32, x.shape, 0)
    x_prev = jnp.where(rows == 0, prev_row, pltpu.roll(x, 1, axis=0))
    x_next = jnp.where(rows == n - 1, next_row, pltpu.roll(x, n - 1, axis=0))
    return x_prev, x_next


def _halo_rows(prev_ref, next_ref, row0, n_rows, seq):
    first = (row0 % seq) == 0
    last = ((row0 + n_rows) % seq) == 0
    prev_row = jnp.where(first, 0.0, prev_ref[SUBLANES - 1:SUBLANES, :])
    next_row = jnp.where(last, 0.0, next_ref[0:1, :])
    return prev_row, next_row


def _halo_specs(tile, width, n_rows_total):
    per = tile // SUBLANES
    last_blk = n_rows_total // SUBLANES - 1
    prev = pl.BlockSpec((SUBLANES, width), lambda i: (jnp.maximum(i * per - 1, 0), 0))
    nxt = pl.BlockSpec((SUBLANES, width), lambda i: (jnp.minimum((i + 1) * per, last_blk), 0))
    return prev, nxt


def _ln0_body(x_ref, g_ref, b_ref, o_ref):
    o_ref[...] = _layer_norm(x_ref[...], g_ref[...], b_ref[...])


def _ln0(x2, g, b, tile):
    n, d = x2.shape
    return pl.pallas_call(
        _ln0_body,
        grid=(n // tile,),
        in_specs=[pl.BlockSpec((tile, d), lambda i: (i, 0)),
                  pl.BlockSpec((1, d), lambda i: (0, 0)),
                  pl.BlockSpec((1, d), lambda i: (0, 0))],
        out_specs=pl.BlockSpec((tile, d), lambda i: (i, 0)),
        out_shape=jax.ShapeDtypeStruct((n, d), F32),
        compiler_params=_cparams(("parallel",)),
        name="ln0",
    )(x2, g.reshape(1, d), b.reshape(1, d))


def _inproj_body(seq, tile, h_ref, hp_ref, hn_ref, win_ref, muw_ref, mua_ref, mug_ref,
                 w1_ref, w2_ref, w0_ref, a1_ref, a2_ref, a0_ref, g1_ref, g2_ref,
                 pssm_ref, prkv_ref, pconv_ref, pfft_ref, logw_ref, agate_ref, g_ref):
    i = pl.program_id(0)
    h = h_ref[...]
    prev_row, next_row = _halo_rows(hp_ref, hn_ref, i * tile, tile, seq)
    x_prev, x_next = _shift_rows(h, prev_row, next_row)

    p = _bdot(h, win_ref[...])
    pssm_ref[...] = p[:, :W_SSM]
    prkv_ref[...] = p[:, W_SSM:W_SSM + 3 * W_RWKV]
    pconv_ref[...] = p[:, W_SSM + 3 * W_RWKV:W_SSM + 3 * W_RWKV + 3 * W_CONV]
    pfft_ref[...] = p[:, W_SSM + 3 * W_RWKV + 3 * W_CONV:]

    for d, x_sh in enumerate((x_prev, x_next)):
        dx = x_sh - h
        xw = h + dx * muw_ref[d:d + 1, :]
        xa = h + dx * mua_ref[d:d + 1, :]
        w_lora = _bdot(jnp.tanh(_bdot(xw, w1_ref[d])), w2_ref[d])
        logw_ref[d] = -RWKV_DECAY_SCALE * jax.nn.sigmoid(w0_ref[d:d + 1, :] + w_lora)
        a_lora = _bdot(_bdot(xa, a1_ref[d]), a2_ref[d])
        agate_ref[d] = jax.nn.sigmoid(a0_ref[d:d + 1, :] + a_lora)
    xg = h + (0.5 * (x_prev + x_next) - h) * mug_ref[...]
    g_ref[...] = _bdot(jax.nn.sigmoid(_bdot(xg, g1_ref[...])), g2_ref[...])


def _inproj(h, seq, tile, w_in, mu_w, mu_a, mu_g, w1, w2, w0, a1, a2, a0, g1, g2):
    n, d = h.shape
    hp_spec, hn_spec = _halo_specs(tile, d, n)

    def full(a):
        nd = a.ndim
        return pl.BlockSpec(a.shape, lambda i: (0,) * nd)

    row = lambda w: pl.BlockSpec((tile, w), lambda i: (i, 0))
    row2 = lambda w: pl.BlockSpec((2, tile, w), lambda i: (0, i, 0))
    consts = (w_in.astype(BF16), mu_w, mu_a, mu_g.reshape(1, d), w1.astype(BF16), w2.astype(BF16), w0,
              a1.astype(BF16), a2.astype(BF16), a0, g1.astype(BF16), g2.astype(BF16))
    return pl.pallas_call(
        functools.partial(_inproj_body, seq, tile),
        grid=(n // tile,),
        in_specs=[pl.BlockSpec((tile, d), lambda i: (i, 0)), hp_spec, hn_spec] + [full(c) for c in consts],
        out_specs=[row(W_SSM), row(3 * W_RWKV), row(3 * W_CONV), row(W_FFT),
                   row2(W_RWKV), row2(W_RWKV), row(W_RWKV)],
        out_shape=[jax.ShapeDtypeStruct((n, W_SSM), F32),
                   jax.ShapeDtypeStruct((n, 3 * W_RWKV), F32),
                   jax.ShapeDtypeStruct((n, 3 * W_CONV), F32),
                   jax.ShapeDtypeStruct((n, W_FFT), F32),
                   jax.ShapeDtypeStruct((2, n, W_RWKV), F32),
                   jax.ShapeDtypeStruct((2, n, W_RWKV), F32),
                   jax.ShapeDtypeStruct((n, W_RWKV), F32)],
        compiler_params=_cparams(("parallel",)),
        name="inproj",
    )(h, h, h, *consts)


S5_SEGS = SUBLANES


def _cmul(ar, ai, br, bi):
    return ar * br - ai * bi, ar * bi + ai * br


def _s5_body(tl, n_tiles, seg_len, u_ref, bb_ref, a_ref, c_ref, y_ref, bu_ref, st_ref):
    d = pl.program_id(1)
    ps = pl.program_id(2)
    i = pl.program_id(3)
    half = SSM_GROUPS * SSM_STATE
    shape = (S5_SEGS, half)
    ar = jnp.broadcast_to(a_ref[0, :, :half], shape)
    ai = jnp.broadcast_to(a_ref[0, :, half:], shape)

    @pl.when(jnp.logical_and(i == 0, ps == 0))
    def _():
        st_ref[...] = jnp.zeros_like(st_ref)

    @pl.when(jnp.logical_and(i == 0, ps == 1))
    def _():
        pr, pi = jnp.ones(shape, F32), jnp.zeros(shape, F32)
        br, bi = ar, ai
        e = seg_len
        while e:
            if e & 1:
                pr, pi = _cmul(pr, pi, br, bi)
            br, bi = _cmul(br, bi, br, bi)
            e >>= 1
        er, ei = st_ref[:, :half], st_ref[:, half:]
        zero = jnp.zeros((1, half), F32)
        fr, fi = [zero], [zero]
        for j in range(1, S5_SEGS):
            nr, ni = _cmul(pr[0:1], pi[0:1], fr[-1], fi[-1])
            fr.append(nr + er[j - 1:j])
            fi.append(ni + ei[j - 1:j])
        rr, ri = [zero], [zero]
        for j in range(S5_SEGS - 2, -1, -1):
            nr, ni = _cmul(pr[0:1], pi[0:1], rr[0], ri[0])
            rr.insert(0, nr + er[j + 1:j + 2])
            ri.insert(0, ni + ei[j + 1:j + 2])
        fwd = d == 0
        st_ref[:, :half] = jnp.where(fwd, jnp.concatenate(fr, axis=0), jnp.concatenate(rr, axis=0))
        st_ref[:, half:] = jnp.where(fwd, jnp.concatenate(fi, axis=0), jnp.concatenate(ri, axis=0))

    bu_ref[...] = _bdot(u_ref[0], bb_ref[0])

    def scan(store):
        def step(t, carry):
            xr, xi = carry
            row = pl.multiple_of((t + d * (tl - 1 - 2 * t)) * S5_SEGS, S5_SEGS)
            nr = ar * xr - ai * xi + bu_ref[pl.ds(row, S5_SEGS), :half]
            ni = ar * xi + ai * xr + bu_ref[pl.ds(row, S5_SEGS), half:]
            if store:
                bu_ref[pl.ds(row, S5_SEGS), :half] = nr
                bu_ref[pl.ds(row, S5_SEGS), half:] = ni
            return nr, ni

        xr, xi = lax.fori_loop(0, tl, step, (st_ref[:, :half], st_ref[:, half:]))
        st_ref[:, :half] = xr
        st_ref[:, half:] = xi

    @pl.when(ps == 0)
    def _():
        scan(False)

    @pl.when(ps == 1)
    def _():
        scan(True)
        y_ref[0, 0] = _bdot(bu_ref[...], c_ref[0])


def _s5_params(lam_re, lam_im, log_dt, b_re, b_im, c_re, c_im):
    g, p, hch = SSM_GROUPS, SSM_STATE, SSM_CH
    dt = jnp.exp(log_dt)[..., None]
    mag = jnp.exp(lam_re * dt)
    lb_re = mag * jnp.cos(lam_im * dt)
    lb_im = mag * jnp.sin(lam_im * dt)
    den = lam_re * lam_re + lam_im * lam_im
    nr = lb_re - 1.0
    coef_re = (nr * lam_re + lb_im * lam_im) / den
    coef_im = (lb_im * lam_re - nr * lam_im) / den
    bb_re = coef_re[..., None] * b_re - coef_im[..., None] * b_im
    bb_im = coef_re[..., None] * b_im + coef_im[..., None] * b_re
    eye = jnp.eye(g, dtype=F32)
    bd_in = lambda m: jnp.einsum('dgph,gk->dghkp', m, eye).reshape(2, g * hch, g * p)
    bb = jnp.concatenate([bd_in(bb_re), bd_in(bb_im)], axis=-1)
    bd_out = lambda m: jnp.einsum('dghp,gk->dgpkh', m, eye).reshape(2, g * p, g * hch)
    cc = jnp.concatenate([bd_out(c_re), -bd_out(c_im)], axis=1)
    a = jnp.concatenate([lb_re.reshape(2, 1, g * p), lb_im.reshape(2, 1, g * p)], axis=-1)
    return bb, a, cc


def _s5_scan(p_ssm, bsz, seq, bb, a, cc, tl):
    seg_len = seq // S5_SEGS
    n_tiles = seg_len // tl
    rows = tl * S5_SEGS
    w2 = 2 * SSM_GROUPS * SSM_STATE
    u_r = p_ssm.reshape(bsz, S5_SEGS, seg_len, W_SSM).transpose(0, 2, 1, 3).reshape(bsz, seq, W_SSM)
    tile_of = lambda d, i: jnp.where(d == 0, i, n_tiles - 1 - i)
    y_r = pl.pallas_call(
        functools.partial(_s5_body, tl, n_tiles, seg_len),
        grid=(bsz, 2, 2, n_tiles),
        in_specs=[pl.BlockSpec((1, rows, W_SSM), lambda b, d, ps, i: (b, tile_of(d, i), 0)),
                  pl.BlockSpec((1, W_SSM, w2), lambda b, d, ps, i: (d, 0, 0)),
                  pl.BlockSpec((1, 1, w2), lambda b, d, ps, i: (d, 0, 0)),
                  pl.BlockSpec((1, w2, W_SSM), lambda b, d, ps, i: (d, 0, 0))],
        out_specs=pl.BlockSpec((1, 1, rows, W_SSM),
                               lambda b, d, ps, i: (d, b, tile_of(d, jnp.where(ps == 0, 0, i)), 0)),
        out_shape=jax.ShapeDtypeStruct((2, bsz, seq, W_SSM), F32),
        scratch_shapes=[pltpu.VMEM((rows, w2), F32), pltpu.VMEM((S5_SEGS, w2), F32)],
        compiler_params=_cparams(("arbitrary",) * 4),
        name="s5_scan",
    )(u_r, bb.astype(BF16), a, cc.astype(BF16))
    y = y_r.reshape(2, bsz, seg_len, S5_SEGS, W_SSM).transpose(0, 1, 3, 2, 4)
    return y.reshape(2, bsz * seq, W_SSM)


def _gelu_tanh(x):
    c = math.sqrt(2.0 / math.pi)
    return 0.5 * x * (1.0 + jnp.tanh(c * (x + 0.044715 * (x * x * x))))


def _s5_post(y2, u, dskip, glu_w, glu_b):
    y = _gelu_tanh(y2[0] + y2[1] + dskip * u)
    return y * jax.nn.sigmoid(_bdot(y, glu_w) + glu_b)


def _split_dot(x, w):
    hi = x.astype(BF16)
    lo = (x - hi.astype(F32)).astype(BF16)
    return (jnp.dot(hi, w, preferred_element_type=F32) + jnp.dot(lo, w, preferred_element_type=F32))


def _dot_nt(a, b):
    return lax.dot_general(a.astype(BF16), b.astype(BF16), (((1,), (1,)), ((), ())),
                           preferred_element_type=F32)


def _rwkv_masks(chunk):
    hh = RWKV_HEADS
    r = np.arange(hh * chunk)[:, None] // chunk
    bm_feat = (r == np.arange(W_RWKV)[None, :] // RWKV_HEAD).astype(np.float32)
    bm_time = (r == np.arange(hh * chunk)[None, :] // chunk).astype(np.float32)
    f = np.arange(W_RWKV)
    bm_head = (f[:, None] // RWKV_HEAD == f[None, :] // RWKV_HEAD).astype(np.float32)
    return bm_feat, bm_time, bm_head


def _rwkv_chunk(fwd, x, edge_row, logw, a, mu, k_k, k_a, r_k, bmf16, bmt16, bmh, h_old):
    ll = x.shape[0]
    w = W_RWKV
    rows = lax.broadcasted_iota(jnp.int32, x.shape, 0)
    if fwd:
        shifted = jnp.where(rows == 0, edge_row, pltpu.roll(x, 1, axis=0))
    else:
        shifted = jnp.where(rows == ll - 1, edge_row, pltpu.roll(x, ll - 1, axis=0))
    rkv = x + (shifted - x) * mu
    r, k, v = rkv[:, :w], rkv[:, w:2 * w], rkv[:, 2 * w:]
    bmh16 = bmh.astype(BF16)

    kk = k * k_k
    ksq = _split_dot(kk * kk, bmh16)
    k2 = k * (1.0 + (a - 1.0) * k_a)
    bonus = _split_dot(r * k2 * r_k, bmh16) * v

    ti = lax.broadcasted_iota(jnp.int32, (ll, ll), 0)
    si = lax.broadcasted_iota(jnp.int32, (ll, ll), 1)
    tri = ((si <= ti) if fwd else (si >= ti)).astype(F32).astype(BF16)
    cum = _split_dot_left(tri, logw)
    yield
    kk = kk * lax.rsqrt(ksq + 1e-12)
    ctot = jnp.sum(logw, axis=0, keepdims=True)
    e_neg = jnp.exp(-cum)
    e_rem = jnp.exp(ctot - cum)
    ah = -kk * jnp.exp(cum - logw)
    rh = r * jnp.exp(cum)
    bvec = kk * a
    bh, kh = bvec * e_neg, k2 * e_neg
    bt, kt = bvec * e_rem, k2 * e_rem

    def bd(m):
        m16 = m.astype(BF16)
        return jnp.concatenate([m16] * RWKV_HEADS, axis=0) * (bmf16 if m.shape[1] == w else bmt16)

    gram = _dot_nt(jnp.concatenate([ah, rh], axis=0), jnp.concatenate([bd(bh), bd(kh)], axis=0))
    yield
    l4 = RWKV_HEADS * ll
    tt = lax.broadcasted_iota(jnp.int32, (ll, l4), 0)
    ss = lax.broadcasted_iota(jnp.int32, (ll, l4), 1) % ll
    strict = (ss < tt) if fwd else (ss > tt)
    incl = (ss <= tt) if fwd else (ss >= tt)
    n_ab = jnp.where(strict, gram[:ll, :l4], 0.0)
    a_ak = jnp.where(strict, gram[:ll, l4:], 0.0)
    a_rb = jnp.where(incl, gram[ll:, :l4], 0.0)
    a_rk = jnp.where(incl, gram[ll:, l4:], 0.0)

    pw = n_ab
    tinv = jnp.where(ss == tt, 1.0, 0.0) + n_ab
    akv = _bdot(a_ak, bd(v))
    for _ in range(int(math.log2(ll)) - 1):
        pw = _bdot(pw, bd(pw))
        yield
        tinv = tinv + _bdot(tinv, bd(pw))
        yield

    ta = _bdot(tinv, jnp.concatenate([bd(ah), bd(akv)], axis=1))
    yield
    ap, wm = ta[:, :w], ta[:, w:]
    rp = rh + _bdot(a_rb, bd(ap))
    y0 = _bdot(jnp.concatenate([a_rb, a_rk], axis=1), jnp.concatenate([bd(wm), bd(v)], axis=0))
    btk = jnp.transpose(jnp.concatenate([bt, kt], axis=0))
    rhs = jnp.concatenate([jnp.concatenate([ap, wm], axis=1),
                           jnp.concatenate([jnp.zeros_like(v), v], axis=1)], axis=0)
    pq = _bdot(btk, rhs)
    yield
    eye = (lax.broadcasted_iota(jnp.int32, (w, w), 0) == lax.broadcasted_iota(jnp.int32, (w, w), 1))
    pm = pq[:, :w] * bmh + jnp.where(eye, jnp.exp(ctot), 0.0)
    qm = pq[:, w:] * bmh
    yield _bdot(rp, h_old) + y0, bonus, _bdot(pm, h_old) + qm


def _run_lockstep(gens):
    results = [None] * len(gens)
    live = list(range(len(gens)))
    while live:
        for s in list(live):
            try:
                out = next(gens[s])
            except StopIteration:
                live.remove(s)
            else:
                if out is not None:
                    results[s] = out
    return results


def _rwkv_body(n_chunks, nb, *refs):
    dir_refs = [refs[0:4], refs[4:8]]
    mu_ref, kk_ref, ka_ref, rk_ref, bmf_ref, bmt_ref, bmh_ref = refs[8:15]
    y_refs, bonus_refs, h_ref = refs[15:17], refs[17:19], refs[19]
    c = pl.program_id(0)

    @pl.when(c == 0)
    def _():
        h_ref[...] = jnp.zeros_like(h_ref)

    gens = []
    for d in range(2):
        x_ref, edge_ref, logw_ref, a_ref = dir_refs[d]
        for b in range(nb):
            edge = edge_ref[b, SUBLANES - 1:SUBLANES, :] if d == 0 else edge_ref[b, 0:1, :]
            edge = jnp.where(c == 0, 0.0, edge)
            gens.append(_rwkv_chunk(d == 0, x_ref[b], edge, logw_ref[0, b], a_ref[0, b], mu_ref[d], kk_ref[...],
                                    ka_ref[...], rk_ref[...], bmf_ref[...], bmt_ref[...], bmh_ref[...],
                                    h_ref[d * nb + b]))
    for s, (y, bonus, h_new) in enumerate(_run_lockstep(gens)):
        y_refs[s // nb][s % nb] = y
        bonus_refs[s // nb][s % nb] = bonus
        h_ref[s] = h_new


def _split_dot_left(w, x):
    hi = x.astype(BF16)
    lo = (x - hi.astype(F32)).astype(BF16)
    return (jnp.dot(w, hi, preferred_element_type=F32) + jnp.dot(w, lo, preferred_element_type=F32))


def _rwkv_scan(p_rkv, logw, agate, bsz, seq, chunk, mu_rkv, k_k, k_a, r_k):
    n = bsz * seq
    n_chunks = seq // chunk
    per = chunk // SUBLANES
    w3 = 3 * W_RWKV
    bmf, bmt, bmh = _rwkv_masks(chunk)
    consts = (mu_rkv.reshape(2, 1, w3), k_k.reshape(1, -1), k_a.reshape(1, -1), r_k.reshape(1, -1),
              jnp.asarray(bmf).astype(BF16), jnp.asarray(bmt).astype(BF16), jnp.asarray(bmh))
    x3 = p_rkv.reshape(bsz, seq, w3)
    logw4 = logw.reshape(2, bsz, seq, W_RWKV)
    a4 = agate.reshape(2, bsz, seq, W_RWKV)
    args, in_specs, out_specs = [], [], []
    for d in range(2):
        cidx = (lambda c: c) if d == 0 else (lambda c: n_chunks - 1 - c)
        if d == 0:
            edge = lambda c: (0, jnp.maximum(c * per - 1, 0), 0)
        else:
            edge = lambda c: (0, jnp.minimum((n_chunks - c) * per, seq // SUBLANES - 1), 0)
        args += [x3, x3, logw4, a4]
        in_specs += [pl.BlockSpec((bsz, chunk, w3), lambda c, cidx=cidx: (0, cidx(c), 0)),
                     pl.BlockSpec((bsz, SUBLANES, w3), edge),
                     pl.BlockSpec((1, bsz, chunk, W_RWKV), lambda c, d=d, cidx=cidx: (d, 0, cidx(c), 0)),
                     pl.BlockSpec((1, bsz, chunk, W_RWKV), lambda c, d=d, cidx=cidx: (d, 0, cidx(c), 0))]
        out_specs.append(pl.BlockSpec((bsz, chunk, W_RWKV), lambda c, cidx=cidx: (0, cidx(c), 0)))

    def full(a):
        nd = a.ndim
        return pl.BlockSpec(a.shape, lambda c: (0,) * nd)

    y_f, y_b, bonus_f, bonus_b = pl.pallas_call(
        functools.partial(_rwkv_body, n_chunks, bsz),
        grid=(n_chunks,),
        in_specs=in_specs + [full(a) for a in consts],
        out_specs=out_specs + out_specs,
        out_shape=[jax.ShapeDtypeStruct((bsz, seq, W_RWKV), F32)] * 4,
        scratch_shapes=[pltpu.VMEM((2 * bsz, W_RWKV, W_RWKV), F32)],
        compiler_params=_cparams(("arbitrary",)),
        name="rwkv_scan",
    )(*args, *consts)
    return [a.reshape(n, W_RWKV) for a in (y_f, y_b, bonus_f, bonus_b)]


def _head_mean(x, bmh16):
    return _split_dot(x, bmh16) * (1.0 / RWKV_HEAD)


def _rwkv_post(y, bonus, g, gn_g, gn_b, bmh16):
    mu = _head_mean(y, bmh16)
    yc = y - mu
    var = _head_mean(yc * yc, bmh16)
    yn = yc * lax.rsqrt(var + RWKV_GN_EPS) * gn_g + gn_b
    return (yn + bonus) * g


FFT_R1 = 64


def _fft_tables(seq):
    r1, r2 = FFT_R1, seq // FFT_R1
    i1 = jnp.arange(r1, dtype=jnp.int32)
    ang1 = ((i1[:, None] * i1[None, :]) % r1).astype(F32) * (2.0 * math.pi / r1)
    stage1 = jnp.concatenate([jnp.cos(ang1), -jnp.sin(ang1)], axis=0)
    k1 = i1[:, None, None]
    k2 = jnp.arange(r2, dtype=jnp.int32)[None, :, None]
    n2 = jnp.arange(r2, dtype=jnp.int32)[None, None, :]
    ang2 = ((n2 * (k1 + r1 * k2)) % seq).astype(F32) * (2.0 * math.pi / seq)
    mr, mi = jnp.cos(ang2), -jnp.sin(ang2)
    stage2 = jnp.concatenate([jnp.concatenate([mr, -mi], axis=2),
                              jnp.concatenate([mi, mr], axis=2)], axis=1)
    c = jnp.arange(W_FFT, dtype=jnp.int32)
    same = (c[:, None] // FFT_CH) == (c[None, :] // FFT_CH)
    angc = ((c[:, None] * c[None, :]) % FFT_CH).astype(F32) * (2.0 * math.pi / FFT_CH)
    scale = 1.0 / math.sqrt(seq * FFT_CH)
    chan = jnp.concatenate([jnp.where(same, jnp.cos(angc), 0.0), jnp.where(same, jnp.sin(angc), 0.0)],
                           axis=0) * scale
    return stage1, stage2, chan


def _fft1_body(z_ref, m_ref, are_ref, aim_ref):
    r1 = FFT_R1
    a = _bdot(m_ref[...], z_ref[0])
    are_ref[0] = a[:r1]
    aim_ref[0] = a[r1:]


def _fft2_body(kb, r2, are_ref, aim_ref, m_ref, chan_ref, o_ref):
    for j in range(kb):
        x = jnp.concatenate([are_ref[0, j], aim_ref[0, j]], axis=0)
        f = _bdot(m_ref[j], x)
        fri = jnp.concatenate([f[:r2], f[r2:]], axis=1)
        o_ref[0, :, j * W_FFT:(j + 1) * W_FFT] = _bdot(fri, chan_ref[...])


def _fourier(p_fft, bsz, seq, tables=None):
    r1, r2 = FFT_R1, seq // FFT_R1
    stage1, stage2, chan = tables if tables is not None else _fft_tables(seq)
    cols = r2 * W_FFT
    tc = min(cols, 4096)
    z = p_fft.reshape(bsz, r1, cols)
    blk = pl.BlockSpec((1, r1, tc), lambda b, j: (b, 0, j))
    a_re, a_im = pl.pallas_call(
        _fft1_body,
        grid=(bsz, cols // tc),
        in_specs=[blk, pl.BlockSpec((2 * r1, r1), lambda b, j: (0, 0))],
        out_specs=[blk, blk],
        out_shape=[jax.ShapeDtypeStruct((bsz, r1, cols), F32)] * 2,
        compiler_params=_cparams(("parallel", "parallel")),
        name="fft_stage1",
    )(z, stage1.astype(BF16))
    kb = 8
    ablk = pl.BlockSpec((1, kb, r2, W_FFT), lambda b, j: (b, j, 0, 0))
    out = pl.pallas_call(
        functools.partial(_fft2_body, kb, r2),
        grid=(bsz, r1 // kb),
        in_specs=[ablk, ablk,
                  pl.BlockSpec((kb, 2 * r2, 2 * r2), lambda b, j: (j, 0, 0)),
                  pl.BlockSpec((2 * W_FFT, W_FFT), lambda b, j: (0, 0))],
        out_specs=pl.BlockSpec((1, r2, kb * W_FFT), lambda b, j: (b, 0, j)),
        out_shape=jax.ShapeDtypeStruct((bsz, r2, r1 * W_FFT), F32),
        compiler_params=_cparams(("parallel", "parallel")),
        name="fft_stage2",
    )(a_re.reshape(bsz, r1, r2, W_FFT), a_im.reshape(bsz, r1, r2, W_FFT), stage2.astype(BF16), chan.astype(BF16))
    return out.reshape(bsz * seq, W_FFT)


def _mixout_body(seq, tile, h_ref, ys5_ref, pssm_ref, yrf_ref, yrb_ref, bnf_ref, bnb_ref, g_ref, pc_ref, pcp_ref,
                 pcn_ref, yf_ref, wout_ref, dskip_ref, gluw_ref, glub_ref, gng_ref, gnb_ref, convw_ref, bmh_ref,
                 lng_ref, lnb_ref, o_ref):
    i = pl.program_id(0)
    y_a = _s5_post(ys5_ref, pssm_ref[...], dskip_ref[...], gluw_ref[...], glub_ref[...])
    y_b = _rwkv_post(yrf_ref[...] + yrb_ref[...], bnf_ref[...] + bnb_ref[...], g_ref[...], gng_ref[...],
                     gnb_ref[...], bmh_ref[...])
    pc = pc_ref[...]
    wc = W_CONV
    prev_row, next_row = _halo_rows(pcp_ref, pcn_ref, i * tile, tile, seq)
    z = pc[:, wc:2 * wc] * pc[:, 2 * wc:]
    z_prev, z_next = _shift_rows(z, prev_row[:, wc:2 * wc] * prev_row[:, 2 * wc:],
                                 next_row[:, wc:2 * wc] * next_row[:, 2 * wc:])
    y_c = pc[:, :wc] * (convw_ref[0:1, :] * z_prev + convw_ref[1:2, :] * z + convw_ref[2:3, :] * z_next)
    mix = (_bdot(y_a, wout_ref[0:W_SSM, :]) + _bdot(y_b, wout_ref[W_SSM:W_SSM + W_RWKV, :])
           + _bdot(y_c, wout_ref[W_SSM + W_RWKV:W_SSM + W_RWKV + W_CONV, :])
           + _bdot(yf_ref[...], wout_ref[W_SSM + W_RWKV + W_CONV:, :]))
    o_ref[...] = _layer_norm(DEEPNORM_ALPHA * h_ref[...] + mix, lng_ref[...], lnb_ref[...])


def _mixout(h, seq, tile, y_s5, p_ssm, rwkv_outs, g, p_conv, y_fft, w_out, dskip, glu_w, glu_b, gn_g, gn_b,
            conv_w, ln_g, ln_b):
    n, d = h.shape
    row = lambda w: pl.BlockSpec((tile, w), lambda i: (i, 0))
    row2 = lambda w: pl.BlockSpec((2, tile, w), lambda i: (0, i, 0))
    pcp, pcn = _halo_specs(tile, 3 * W_CONV, n)
    bmh16 = jnp.asarray(_rwkv_masks(RWKV_HEAD)[2]).astype(BF16)
    consts = (w_out.astype(BF16), dskip.reshape(1, -1), glu_w.astype(BF16), glu_b.reshape(1, -1),
              gn_g.reshape(1, -1), gn_b.reshape(1, -1), conv_w, bmh16, ln_g.reshape(1, -1), ln_b.reshape(1, -1))

    def full(a):
        nd = a.ndim
        return pl.BlockSpec(a.shape, lambda i: (0,) * nd)

    return pl.pallas_call(
        functools.partial(_mixout_body, seq, tile),
        grid=(n // tile,),
        in_specs=[row(d), row2(W_SSM), row(W_SSM)] + [row(W_RWKV)] * 5
                 + [row(3 * W_CONV), pcp, pcn, row(W_FFT)] + [full(c) for c in consts],
        out_specs=row(d),
        out_shape=jax.ShapeDtypeStruct((n, d), F32),
        compiler_params=_cparams(("parallel",)),
        name="mix_out",
    )(h, y_s5, p_ssm, *rwkv_outs, g, p_conv, p_conv, p_conv, y_fft, *consts)


def _ffn_body(n_ff, h_ref, w1_ref, w3_ref, w2_ref, lng_ref, lnb_ref, o_ref, acc_ref):
    j = pl.program_id(1)

    @pl.when(j == 0)
    def _():
        acc_ref[...] = jnp.zeros_like(acc_ref)

    x = h_ref[...].astype(BF16)
    u = jax.nn.silu(jnp.dot(x, w1_ref[...], preferred_element_type=F32)) * jnp.dot(
        x, w3_ref[...], preferred_element_type=F32)
    acc_ref[...] += _bdot(u, w2_ref[...])

    @pl.when(j == n_ff - 1)
    def _():
        o_ref[...] = _layer_norm(DEEPNORM_ALPHA * h_ref[...] + acc_ref[...], lng_ref[...], lnb_ref[...])


def _ffn(h, tile, tf, w1, w3, w2, ln_g, ln_b):
    n, d = h.shape
    dff = w1.shape[1]
    n_ff = dff // tf
    return pl.pallas_call(
        functools.partial(_ffn_body, n_ff),
        grid=(n // tile, n_ff),
        in_specs=[pl.BlockSpec((tile, d), lambda i, j: (i, 0)),
                  pl.BlockSpec((d, tf), lambda i, j: (0, j)),
                  pl.BlockSpec((d, tf), lambda i, j: (0, j)),
                  pl.BlockSpec((tf, d), lambda i, j: (j, 0)),
                  pl.BlockSpec((1, d), lambda i, j: (0, 0)),
                  pl.BlockSpec((1, d), lambda i, j: (0, 0))],
        out_specs=pl.BlockSpec((tile, d), lambda i, j: (i, 0)),
        out_shape=jax.ShapeDtypeStruct((n, d), F32),
        scratch_shapes=[pltpu.VMEM((tile, d), F32)],
        compiler_params=_cparams(("parallel", "arbitrary")),
        name="ffn",
    )(h, w1.astype(BF16), w3.astype(BF16), w2.astype(BF16), ln_g.reshape(1, d), ln_b.reshape(1, d))


ROW_TILE = 512
S5_TILE_STEPS = 64
RWKV_CHUNK = 64
FFN_COLS = 1408


def kernel(x, ln0_g, ln0_b, w_in, s5_lambda_re, s5_lambda_im, s5_log_dt, s5_b_re, s5_b_im, s5_c_re, s5_c_im, s5_d,
           s5_glu_w, s5_glu_b, rwkv_mu_rkv, rwkv_mu_w, rwkv_mu_a, rwkv_mu_g, rwkv_w0, rwkv_w1, rwkv_w2, rwkv_a0,
           rwkv_a1, rwkv_a2, rwkv_g1, rwkv_g2, rwkv_k_k, rwkv_k_a, rwkv_r_k, rwkv_gn_g, rwkv_gn_b, conv_w, w_out,
           ln1_g, ln1_b, ffn_w1, ffn_w3, ffn_w2, ln2_g, ln2_b):
    bsz, seq, d = x.shape
    n = bsz * seq
    tile = min(ROW_TILE, seq)
    s5_tl = min(S5_TILE_STEPS, seq // S5_SEGS)
    fft_tables = _fft_tables(seq)
    h = _ln0(x.reshape(n, d), ln0_g, ln0_b, tile)
    for l in range(w_in.shape[0]):
        p_ssm, p_rkv, p_conv, p_fft, logw, agate, g = _inproj(
            h, seq, tile, w_in[l], rwkv_mu_w[l], rwkv_mu_a[l], rwkv_mu_g[l], rwkv_w1[l], rwkv_w2[l], rwkv_w0[l],
            rwkv_a1[l], rwkv_a2[l], rwkv_a0[l], rwkv_g1[l], rwkv_g2[l])
        bb, a, cc = _s5_params(s5_lambda_re[l], s5_lambda_im[l], s5_log_dt[l], s5_b_re[l], s5_b_im[l],
                               s5_c_re[l], s5_c_im[l])
        y_s5 = _s5_scan(p_ssm, bsz, seq, bb, a, cc, s5_tl)
        rwkv_outs = _rwkv_scan(p_rkv, logw, agate, bsz, seq, RWKV_CHUNK, rwkv_mu_rkv[l], rwkv_k_k[l],
                               rwkv_k_a[l], rwkv_r_k[l].reshape(-1))
        y_fft = _fourier(p_fft, bsz, seq, fft_tables)
        h = _mixout(h, seq, tile, y_s5, p_ssm, rwkv_outs, g, p_conv, y_fft, w_out[l], s5_d[l], s5_glu_w[l],
                    s5_glu_b[l], rwkv_gn_g[l], rwkv_gn_b[l], conv_w[l], ln1_g[l], ln1_b[l])
        h = _ffn(h, tile, FFN_COLS, ffn_w1[l], ffn_w3[l], ffn_w2[l], ln2_g[l], ln2_b[l])
    return h.reshape(bsz, seq, d)
```

```python
import functools
import math

import jax
import jax.numpy as jnp
import numpy as np
from jax import lax
from jax.experimental import pallas as pl
from jax.experimental.pallas import tpu as pltpu

W_SSM = 256
W_RWKV = 256
W_CONV = 256
W_FFT = 256
SSM_CH = 16
SSM_GROUPS = 16
SSM_STATE = 64
RWKV_HEAD = 64
RWKV_HEADS = 4
FFT_GROUPS = 4
FFT_CH = 64
RWKV_DECAY_SCALE = math.exp(-0.5)
RWKV_GN_EPS = 64e-5
LN_EPS = 1e-5
DEPTH = 2
DEEPNORM_ALPHA = (2 * DEPTH) ** 0.25

SUBLANES = 8
VMEM_LIMIT = 48 * 1024 * 1024

BF16 = jnp.bfloat16
F32 = jnp.float32


def _cparams(sem):
    return pltpu.CompilerParams(dimension_semantics=sem, vmem_limit_bytes=VMEM_LIMIT)


def _bdot(a, b):
    return jnp.dot(a.astype(BF16), b.astype(BF16), preferred_element_type=F32)


def _layer_norm(x, g, b):
    mu = jnp.mean(x, axis=-1, keepdims=True)
    xc = x - mu
    var = jnp.mean(xc * xc, axis=-1, keepdims=True)
    return xc * lax.rsqrt(var + LN_EPS) * g + b


def _shift_rows(x, prev_row, next_row):
    n = x.shape[0]
    rows = lax.broadcasted_iota(jnp.int32, x.shape, 0)
    x_prev = jnp.where(rows == 0, prev_row, pltpu.roll(x, 1, axis=0))
    x_next = jnp.where(rows == n - 1, next_row, pltpu.roll(x, n - 1, axis=0))
    return x_prev, x_next


def _halo_rows(prev_ref, next_ref, row0, n_rows, seq):
    first = (row0 % seq) == 0
    last = ((row0 + n_rows) % seq) == 0
    prev_row = jnp.where(first, 0.0, prev_ref[SUBLANES - 1:SUBLANES, :])
    next_row = jnp.where(last, 0.0, next_ref[0:1, :])
    return prev_row, next_row


def _halo_specs(tile, width, n_rows_total):
    per = tile // SUBLANES
    last_blk = n_rows_total // SUBLANES - 1
    prev = pl.BlockSpec((SUBLANES, width), lambda i: (jnp.maximum(i * per - 1, 0), 0))
    nxt = pl.BlockSpec((SUBLANES, width), lambda i: (jnp.minimum((i + 1) * per, last_blk), 0))
    return prev, nxt


def _ln0_body(x_ref, g_ref, b_ref, o_ref):
    o_ref[...] = _layer_norm(x_ref[...], g_ref[...], b_ref[...])


def _ln0(x2, g, b, tile):
    n, d = x2.shape
    return pl.pallas_call(
        _ln0_body,
        grid=(n // tile,),
        in_specs=[pl.BlockSpec((tile, d), lambda i: (i, 0)),
                  pl.BlockSpec((1, d), lambda i: (0, 0)),
                  pl.BlockSpec((1, d), lambda i: (0, 0))],
        out_specs=pl.BlockSpec((tile, d), lambda i: (i, 0)),
        out_shape=jax.ShapeDtypeStruct((n, d), F32),
        compiler_params=_cparams(("parallel",)),
        name="ln0",
    )(x2, g.reshape(1, d), b.reshape(1, d))


def _inproj_body(seq, tile, h_ref, hp_ref, hn_ref, win_ref, muw_ref, mua_ref, mug_ref,
                 w1_ref, w2_ref, w0_ref, a1_ref, a2_ref, a0_ref, g1_ref, g2_ref,
                 pssm_ref, prkv_ref, pconv_ref, pfft_ref, logw_ref, agate_ref, g_ref):
    i = pl.program_id(0)
    h = h_ref[...]
    prev_row, next_row = _halo_rows(hp_ref, hn_ref, i * tile, tile, seq)
    x_prev, x_next = _shift_rows(h, prev_row, next_row)

    p = _bdot(h, win_ref[...])
    pssm_ref[...] = p[:, :W_SSM]
    prkv_ref[...] = p[:, W_SSM:W_SSM + 3 * W_RWKV]
    pconv_ref[...] = p[:, W_SSM + 3 * W_RWKV:W_SSM + 3 * W_RWKV + 3 * W_CONV]
    pfft_ref[...] = p[:, W_SSM + 3 * W_RWKV + 3 * W_CONV:]

    for d, x_sh in enumerate((x_prev, x_next)):
        dx = x_sh - h
        xw = h + dx * muw_ref[d:d + 1, :]
        xa = h + dx * mua_ref[d:d + 1, :]
        w_lora = _bdot(jnp.tanh(_bdot(xw, w1_ref[d])), w2_ref[d])
        logw_ref[d] = -RWKV_DECAY_SCALE * jax.nn.sigmoid(w0_ref[d:d + 1, :] + w_lora)
        a_lora = _bdot(_bdot(xa, a1_ref[d]), a2_ref[d])
        agate_ref[d] = jax.nn.sigmoid(a0_ref[d:d + 1, :] + a_lora)
    xg = h + (0.5 * (x_prev + x_next) - h) * mug_ref[...]
    g_ref[...] = _bdot(jax.nn.sigmoid(_bdot(xg, g1_ref[...])), g2_ref[...])


def _inproj(h, seq, tile, w_in, mu_w, mu_a, mu_g, w1, w2, w0, a1, a2, a0, g1, g2):
    n, d = h.shape
    hp_spec, hn_spec = _halo_specs(tile, d, n)

    def full(a):
        nd = a.ndim
        return pl.BlockSpec(a.shape, lambda i: (0,) * nd)

    row = lambda w: pl.BlockSpec((tile, w), lambda i: (i, 0))
    row2 = lambda w: pl.BlockSpec((2, tile, w), lambda i: (0, i, 0))
    consts = (w_in.astype(BF16), mu_w, mu_a, mu_g.reshape(1, d), w1.astype(BF16), w2.astype(BF16), w0,
              a1.astype(BF16), a2.astype(BF16), a0, g1.astype(BF16), g2.astype(BF16))
    return pl.pallas_call(
        functools.partial(_inproj_body, seq, tile),
        grid=(n // tile,),
        in_specs=[pl.BlockSpec((tile, d), lambda i: (i, 0)), hp_spec, hn_spec] + [full(c) for c in consts],
        out_specs=[row(W_SSM), row(3 * W_RWKV), row(3 * W_CONV), row(W_FFT),
                   row2(W_RWKV), row2(W_RWKV), row(W_RWKV)],
        out_shape=[jax.ShapeDtypeStruct((n, W_SSM), F32),
                   jax.ShapeDtypeStruct((n, 3 * W_RWKV), F32),
                   jax.ShapeDtypeStruct((n, 3 * W_CONV), F32),
                   jax.ShapeDtypeStruct((n, W_FFT), F32),
                   jax.ShapeDtypeStruct((2, n, W_RWKV), F32),
                   jax.ShapeDtypeStruct((2, n, W_RWKV), F32),
                   jax.ShapeDtypeStruct((n, W_RWKV), F32)],
        compiler_params=_cparams(("parallel",)),
        name="inproj",
    )(h, h, h, *consts)


S5_SEGS = SUBLANES


def _cmul(ar, ai, br, bi):
    return ar * br - ai * bi, ar * bi + ai * br


class _S5Dir:
    def __init__(self, fwd, tl, ucur_ref, unext_ref, y_ref, bua_ref, bub_ref, st_ref, perm_ref, permt_ref, bb_ref,
                 a_ref, c_ref):
        self.fwd, self.tl = fwd, tl
        self.ucur_ref, self.unext_ref, self.y_ref = ucur_ref, unext_ref, y_ref
        self.bua_ref, self.bub_ref, self.st_ref = bua_ref, bub_ref, st_ref
        self.perm_ref, self.permt_ref, self.bb_ref, self.c_ref = perm_ref, permt_ref, bb_ref, c_ref
        self.half = SSM_GROUPS * SSM_STATE
        self.rows = tl * S5_SEGS
        self.n_parts = S5_SEGS
        self.part = 2 * self.half // self.n_parts
        shape = (S5_SEGS, self.half)
        self.ar = jnp.broadcast_to(a_ref[:, :self.half], shape)
        self.ai = jnp.broadcast_to(a_ref[:, self.half:], shape)
        self.off_a, self.off_b = (0, tl) if fwd else (tl, 0)

    def project(self, u_ref, off, buf_ref):
        u = u_ref[0, :, off:off + self.tl, :].reshape(self.rows, W_SSM).astype(BF16)
        up = jnp.dot(self.perm_ref[...], u, preferred_element_type=F32).astype(BF16)
        for k in range(self.n_parts):
            cols = slice(k * self.part, (k + 1) * self.part)
            buf_ref[:, cols] = jnp.dot(up, self.bb_ref[:, cols], preferred_element_type=F32)
            yield

    def scan(self, buf_ref, store):
        half, tl = self.half, self.tl
        xr, xi = self.st_ref[:, :half], self.st_ref[:, half:]
        for t in range(tl):
            row = (t if self.fwd else tl - 1 - t) * S5_SEGS
            nr = self.ar * xr - self.ai * xi + buf_ref[row:row + S5_SEGS, :half]
            ni = self.ar * xi + self.ai * xr + buf_ref[row:row + S5_SEGS, half:]
            if store:
                buf_ref[row:row + S5_SEGS, :half] = nr
                buf_ref[row:row + S5_SEGS, half:] = ni
            xr, xi = nr, ni
            if (t + 1) % (tl // self.n_parts) == 0:
                yield
        self.st_ref[:, :half] = xr
        self.st_ref[:, half:] = xi

    def emit(self, buf_ref, off):
        y = None
        for k in range(self.n_parts):
            cols = slice(k * self.part, (k + 1) * self.part)
            yk = _bdot(buf_ref[:, cols], self.c_ref[cols, :])
            y = yk if y is None else y + yk
            yield
        y = jnp.dot(self.permt_ref[...], y.astype(BF16), preferred_element_type=F32)
        self.y_ref[0, :, off:off + self.tl, :] = y.reshape(S5_SEGS, self.tl, W_SSM).astype(self.y_ref.dtype)

    def init_state(self, seg_len):
        half = self.half
        pr, pi = jnp.ones_like(self.ar), jnp.zeros_like(self.ar)
        br, bi = self.ar, self.ai
        e = seg_len
        while e:
            if e & 1:
                pr, pi = _cmul(pr, pi, br, bi)
            br, bi = _cmul(br, bi, br, bi)
            e >>= 1
        er, ei = self.st_ref[:, :half], self.st_ref[:, half:]
        zero = jnp.zeros((1, half), F32)
        order = range(S5_SEGS) if self.fwd else range(S5_SEGS - 1, -1, -1)
        cr, ci, out_r, out_i = zero, zero, {}, {}
        for j in order:
            out_r[j], out_i[j] = cr, ci
            nr, ni = _cmul(pr[0:1], pi[0:1], cr, ci)
            cr, ci = nr + er[j:j + 1], ni + ei[j:j + 1]
        self.st_ref[:, :half] = jnp.concatenate([out_r[j] for j in range(S5_SEGS)], axis=0)
        self.st_ref[:, half:] = jnp.concatenate([out_i[j] for j in range(S5_SEGS)], axis=0)


def _s5_body(tl, seg_len, ucf_ref, unf_ref, ucb_ref, unb_ref, perm_ref, permt_ref, bb_ref, a_ref, c_ref, yf_ref,
             yb_ref, baf_ref, bbf_ref, bab_ref, bbb_ref, st_ref):
    ps = pl.program_id(1)
    i = pl.program_id(2)
    dirs = [_S5Dir(True, tl, ucf_ref, unf_ref, yf_ref, baf_ref, bbf_ref, st_ref.at[0], perm_ref, permt_ref,
                   bb_ref.at[0], a_ref.at[0], c_ref.at[0]),
            _S5Dir(False, tl, ucb_ref, unb_ref, yb_ref, bab_ref, bbb_ref, st_ref.at[1], perm_ref, permt_ref,
                   bb_ref.at[1], a_ref.at[1], c_ref.at[1])]

    @pl.when(jnp.logical_and(i == 0, ps == 0))
    def _():
        st_ref[...] = jnp.zeros_like(st_ref)

    @pl.when(jnp.logical_and(i == 0, ps == 1))
    def _():
        for z in dirs:
            z.init_state(seg_len)

    @pl.when(i == 0)
    def _():
        _run_lockstep([z.project(z.ucur_ref, z.off_a, z.bua_ref) for z in dirs])

    @pl.when(ps == 0)
    def _():
        _run_lockstep([g for z in dirs for g in (z.scan(z.bua_ref, False), z.project(z.ucur_ref, z.off_b, z.bub_ref))])
        _run_lockstep([g for z in dirs for g in (z.scan(z.bub_ref, False), z.project(z.unext_ref, z.off_a, z.bua_ref))])

    @pl.when(ps == 1)
    def _():
        _run_lockstep([g for z in dirs for g in (z.scan(z.bua_ref, True), z.project(z.ucur_ref, z.off_b, z.bub_ref))])
        _run_lockstep([g for z in dirs for g in (z.scan(z.bub_ref, True), z.emit(z.bua_ref, z.off_a),
                                                 z.project(z.unext_ref, z.off_a, z.bua_ref))])
        _run_lockstep([z.emit(z.bub_ref, z.off_b) for z in dirs])


def _s5_params(lam_re, lam_im, log_dt, b_re, b_im, c_re, c_im):
    g, p, hch = SSM_GROUPS, SSM_STATE, SSM_CH
    dt = jnp.exp(log_dt)[..., None]
    mag = jnp.exp(lam_re * dt)
    lb_re = mag * jnp.cos(lam_im * dt)
    lb_im = mag * jnp.sin(lam_im * dt)
    den = lam_re * lam_re + lam_im * lam_im
    nr = lb_re - 1.0
    coef_re = (nr * lam_re + lb_im * lam_im) / den
    coef_im = (lb_im * lam_re - nr * lam_im) / den
    bb_re = coef_re[..., None] * b_re - coef_im[..., None] * b_im
    bb_im = coef_re[..., None] * b_im + coef_im[..., None] * b_re
    eye = jnp.eye(g, dtype=F32)
    bd_in = lambda m: jnp.einsum('dgph,gk->dghkp', m, eye).reshape(2, g * hch, g * p)
    bb = jnp.concatenate([bd_in(bb_re), bd_in(bb_im)], axis=-1)
    bd_out = lambda m: jnp.einsum('dghp,gk->dgpkh', m, eye).reshape(2, g * p, g * hch)
    cc = jnp.concatenate([bd_out(c_re), -bd_out(c_im)], axis=1)
    a = jnp.concatenate([lb_re.reshape(2, 1, g * p), lb_im.reshape(2, 1, g * p)], axis=-1)
    return bb, a, cc


def _s5_scan(p_ssm, bsz, seq, bb, a, cc, tl):
    seg_len = seq // S5_SEGS
    n_pairs = seg_len // (2 * tl)
    rows = tl * S5_SEGS
    w2 = 2 * SSM_GROUPS * SSM_STATE
    src = (np.arange(rows) % S5_SEGS) * tl + np.arange(rows) // S5_SEGS
    perm = np.zeros((rows, rows), np.float32)
    perm[np.arange(rows), src] = 1.0
    last = n_pairs - 1
    u4 = p_ssm.reshape(bsz, S5_SEGS, seg_len, W_SSM)
    ublk = (1, S5_SEGS, 2 * tl, W_SSM)
    const = lambda shape: pl.BlockSpec(shape, lambda b, ps, i: (0,) * len(shape))
    y_f, y_b = pl.pallas_call(
        functools.partial(_s5_body, tl, seg_len),
        grid=(bsz, 2, n_pairs),
        in_specs=[pl.BlockSpec(ublk, lambda b, ps, i: (b, 0, i, 0)),
                  pl.BlockSpec(ublk, lambda b, ps, i: (b, 0, jnp.minimum(i + 1, last), 0)),
                  pl.BlockSpec(ublk, lambda b, ps, i: (b, 0, last - i, 0)),
                  pl.BlockSpec(ublk, lambda b, ps, i: (b, 0, jnp.maximum(last - i - 1, 0), 0)),
                  const((rows, rows)), const((rows, rows)), const((2, W_SSM, w2)), const((2, 1, w2)),
                  const((2, w2, W_SSM))],
        out_specs=[pl.BlockSpec(ublk, lambda b, ps, i: (b, 0, jnp.where(ps == 0, 0, i), 0)),
                   pl.BlockSpec(ublk, lambda b, ps, i: (b, 0, jnp.where(ps == 0, last, last - i), 0))],
        out_shape=[jax.ShapeDtypeStruct((bsz, S5_SEGS, seg_len, W_SSM), BF16)] * 2,
        scratch_shapes=[pltpu.VMEM((rows, w2), F32)] * 4 + [pltpu.VMEM((2, S5_SEGS, w2), F32)],
        compiler_params=_cparams(("arbitrary",) * 3),
        name="s5_scan",
    )(u4, u4, u4, u4, jnp.asarray(perm).astype(BF16), jnp.asarray(perm.T).astype(BF16), bb.astype(BF16), a,
      cc.astype(BF16))
    return y_f.reshape(bsz * seq, W_SSM), y_b.reshape(bsz * seq, W_SSM)


def _gelu_tanh(x):
    c = math.sqrt(2.0 / math.pi)
    return 0.5 * x * (1.0 + jnp.tanh(c * (x + 0.044715 * (x * x * x))))


def _s5_post(y_f, y_b, u, dskip, glu_w, glu_b):
    y = _gelu_tanh(y_f.astype(F32) + y_b.astype(F32) + dskip * u)
    return y * jax.nn.sigmoid(_bdot(y, glu_w) + glu_b)


def _split_dot(x, w):
    hi = x.astype(BF16)
    lo = (x - hi.astype(F32)).astype(BF16)
    return (jnp.dot(hi, w, preferred_element_type=F32) + jnp.dot(lo, w, preferred_element_type=F32))


def _dot_nt(a, b):
    return lax.dot_general(a.astype(BF16), b.astype(BF16), (((1,), (1,)), ((), ())),
                           preferred_element_type=F32)


def _rwkv_masks(chunk):
    hh = RWKV_HEADS
    r = np.arange(hh * chunk)[:, None] // chunk
    bm_feat = (r == np.arange(W_RWKV)[None, :] // RWKV_HEAD).astype(np.float32)
    bm_time = (r == np.arange(hh * chunk)[None, :] // chunk).astype(np.float32)
    f = np.arange(W_RWKV)
    bm_head = (f[:, None] // RWKV_HEAD == f[None, :] // RWKV_HEAD).astype(np.float32)
    return bm_feat, bm_time, bm_head


def _rwkv_chunk(fwd, x, edge_row, logw, a, mu, k_k, k_a, r_k, bmf16, bmt16, bmh, h_old):
    ll = x.shape[0]
    w = W_RWKV
    rows = lax.broadcasted_iota(jnp.int32, x.shape, 0)
    if fwd:
        shifted = jnp.where(rows == 0, edge_row, pltpu.roll(x, 1, axis=0))
    else:
        shifted = jnp.where(rows == ll - 1, edge_row, pltpu.roll(x, ll - 1, axis=0))
    rkv = x + (shifted - x) * mu
    r, k, v = rkv[:, :w], rkv[:, w:2 * w], rkv[:, 2 * w:]
    bmh16 = bmh.astype(BF16)

    kk = k * k_k
    ksq = _split_dot(kk * kk, bmh16)
    k2 = k * (1.0 + (a - 1.0) * k_a)
    bonus = _split_dot(r * k2 * r_k, bmh16) * v

    ti = lax.broadcasted_iota(jnp.int32, (ll, ll), 0)
    si = lax.broadcasted_iota(jnp.int32, (ll, ll), 1)
    tri = ((si <= ti) if fwd else (si >= ti)).astype(F32).astype(BF16)
    cum = _split_dot_left(tri, logw)
    yield
    kk = kk * lax.rsqrt(ksq + 1e-12)
    ctot = jnp.sum(logw, axis=0, keepdims=True)
    e_neg = jnp.exp(-cum)
    e_rem = jnp.exp(ctot - cum)
    ah = -kk * jnp.exp(cum - logw)
    rh = r * jnp.exp(cum)
    bvec = kk * a
    bh, kh = bvec * e_neg, k2 * e_neg
    bt, kt = bvec * e_rem, k2 * e_rem

    def bd(m):
        m16 = m.astype(BF16)
        return jnp.concatenate([m16] * RWKV_HEADS, axis=0) * (bmf16 if m.shape[1] == w else bmt16)

    gram = _dot_nt(jnp.concatenate([ah, rh], axis=0), jnp.concatenate([bd(bh), bd(kh)], axis=0))
    yield
    l4 = RWKV_HEADS * ll
    tt = lax.broadcasted_iota(jnp.int32, (ll, l4), 0)
    ss = lax.broadcasted_iota(jnp.int32, (ll, l4), 1) % ll
    strict = (ss < tt) if fwd else (ss > tt)
    incl = (ss <= tt) if fwd else (ss >= tt)
    n_ab = jnp.where(strict, gram[:ll, :l4], 0.0)
    a_ak = jnp.where(strict, gram[:ll, l4:], 0.0)
    a_rb = jnp.where(incl, gram[ll:, :l4], 0.0)
    a_rk = jnp.where(incl, gram[ll:, l4:], 0.0)

    pw = n_ab
    tinv = jnp.where(ss == tt, 1.0, 0.0) + n_ab
    akv = _bdot(a_ak, bd(v))
    for _ in range(int(math.log2(ll)) - 1):
        pw = _bdot(pw, bd(pw))
        yield
        tinv = tinv + _bdot(tinv, bd(pw))
        yield

    ta = _bdot(tinv, jnp.concatenate([bd(ah), bd(akv)], axis=1))
    yield
    ap, wm = ta[:, :w], ta[:, w:]
    rp = rh + _bdot(a_rb, bd(ap))
    y0 = _bdot(jnp.concatenate([a_rb, a_rk], axis=1), jnp.concatenate([bd(wm), bd(v)], axis=0))
    btk = jnp.transpose(jnp.concatenate([bt, kt], axis=0))
    rhs = jnp.concatenate([jnp.concatenate([ap, wm], axis=1),
                           jnp.concatenate([jnp.zeros_like(v), v], axis=1)], axis=0)
    pq = _bdot(btk, rhs)
    yield
    eye = (lax.broadcasted_iota(jnp.int32, (w, w), 0) == lax.broadcasted_iota(jnp.int32, (w, w), 1))
    pm = pq[:, :w] * bmh + jnp.where(eye, jnp.exp(ctot), 0.0)
    qm = pq[:, w:] * bmh
    yield _bdot(rp, h_old) + y0, bonus, _bdot(pm, h_old) + qm


def _run_lockstep(gens):
    results = [None] * len(gens)
    live = list(range(len(gens)))
    while live:
        for s in list(live):
            try:
                out = next(gens[s])
            except StopIteration:
                live.remove(s)
            else:
                if out is not None:
                    results[s] = out
    return results


def _rwkv_body(n_chunks, nb, *refs):
    dir_refs = [refs[0:4], refs[4:8]]
    mu_ref, kk_ref, ka_ref, rk_ref, bmf_ref, bmt_ref, bmh_ref = refs[8:15]
    y_refs, bonus_refs, h_ref = refs[15:17], refs[17:19], refs[19]
    c = pl.program_id(0)

    @pl.when(c == 0)
    def _():
        h_ref[...] = jnp.zeros_like(h_ref)

    gens = []
    for d in range(2):
        x_ref, edge_ref, logw_ref, a_ref = dir_refs[d]
        for b in range(nb):
            edge = edge_ref[b, SUBLANES - 1:SUBLANES, :] if d == 0 else edge_ref[b, 0:1, :]
            edge = jnp.where(c == 0, 0.0, edge)
            gens.append(_rwkv_chunk(d == 0, x_ref[b], edge, logw_ref[0, b], a_ref[0, b], mu_ref[d], kk_ref[...],
                                    ka_ref[...], rk_ref[...], bmf_ref[...], bmt_ref[...], bmh_ref[...],
                                    h_ref[d * nb + b]))
    for s, (y, bonus, h_new) in enumerate(_run_lockstep(gens)):
        y_refs[s // nb][s % nb] = y
        bonus_refs[s // nb][s % nb] = bonus
        h_ref[s] = h_new


def _split_dot_left(w, x):
    hi = x.astype(BF16)
    lo = (x - hi.astype(F32)).astype(BF16)
    return (jnp.dot(w, hi, preferred_element_type=F32) + jnp.dot(w, lo, preferred_element_type=F32))


def _rwkv_scan(p_rkv, logw, agate, bsz, seq, chunk, mu_rkv, k_k, k_a, r_k):
    n = bsz * seq
    n_chunks = seq // chunk
    per = chunk // SUBLANES
    w3 = 3 * W_RWKV
    bmf, bmt, bmh = _rwkv_masks(chunk)
    consts = (mu_rkv.reshape(2, 1, w3), k_k.reshape(1, -1), k_a.reshape(1, -1), r_k.reshape(1, -1),
              jnp.asarray(bmf).astype(BF16), jnp.asarray(bmt).astype(BF16), jnp.asarray(bmh))
    x3 = p_rkv.reshape(bsz, seq, w3)
    logw4 = logw.reshape(2, bsz, seq, W_RWKV)
    a4 = agate.reshape(2, bsz, seq, W_RWKV)
    args, in_specs, out_specs = [], [], []
    for d in range(2):
        cidx = (lambda c: c) if d == 0 else (lambda c: n_chunks - 1 - c)
        if d == 0:
            edge = lambda c: (0, jnp.maximum(c * per - 1, 0), 0)
        else:
            edge = lambda c: (0, jnp.minimum((n_chunks - c) * per, seq // SUBLANES - 1), 0)
        args += [x3, x3, logw4, a4]
        in_specs += [pl.BlockSpec((bsz, chunk, w3), lambda c, cidx=cidx: (0, cidx(c), 0)),
                     pl.BlockSpec((bsz, SUBLANES, w3), edge),
                     pl.BlockSpec((1, bsz, chunk, W_RWKV), lambda c, d=d, cidx=cidx: (d, 0, cidx(c), 0)),
                     pl.BlockSpec((1, bsz, chunk, W_RWKV), lambda c, d=d, cidx=cidx: (d, 0, cidx(c), 0))]
        out_specs.append(pl.BlockSpec((bsz, chunk, W_RWKV), lambda c, cidx=cidx: (0, cidx(c), 0)))

    def full(a):
        nd = a.ndim
        return pl.BlockSpec(a.shape, lambda c: (0,) * nd)

    y_f, y_b, bonus_f, bonus_b = pl.pallas_call(
        functools.partial(_rwkv_body, n_chunks, bsz),
        grid=(n_chunks,),
        in_specs=in_specs + [full(a) for a in consts],
        out_specs=out_specs + out_specs,
        out_shape=[jax.ShapeDtypeStruct((bsz, seq, W_RWKV), F32)] * 4,
        scratch_shapes=[pltpu.VMEM((2 * bsz, W_RWKV, W_RWKV), F32)],
        compiler_params=_cparams(("arbitrary",)),
        name="rwkv_scan",
    )(*args, *consts)
    return [a.reshape(n, W_RWKV) for a in (y_f, y_b, bonus_f, bonus_b)]


def _head_mean(x, bmh16):
    return _split_dot(x, bmh16) * (1.0 / RWKV_HEAD)


def _rwkv_post(y, bonus, g, gn_g, gn_b, bmh16):
    mu = _head_mean(y, bmh16)
    yc = y - mu
    var = _head_mean(yc * yc, bmh16)
    yn = yc * lax.rsqrt(var + RWKV_GN_EPS) * gn_g + gn_b
    return (yn + bonus) * g


FFT_R1 = 64


def _fft_tables(seq):
    r1, r2 = FFT_R1, seq // FFT_R1
    i1 = jnp.arange(r1, dtype=jnp.int32)
    ang1 = ((i1[:, None] * i1[None, :]) % r1).astype(F32) * (2.0 * math.pi / r1)
    stage1 = jnp.concatenate([jnp.cos(ang1), -jnp.sin(ang1)], axis=0)
    k1 = i1[:, None, None]
    k2 = jnp.arange(r2, dtype=jnp.int32)[None, :, None]
    n2 = jnp.arange(r2, dtype=jnp.int32)[None, None, :]
    ang2 = ((n2 * (k1 + r1 * k2)) % seq).astype(F32) * (2.0 * math.pi / seq)
    mr, mi = jnp.cos(ang2), -jnp.sin(ang2)
    stage2 = jnp.concatenate([jnp.concatenate([mr, -mi], axis=2),
                              jnp.concatenate([mi, mr], axis=2)], axis=1)
    c = jnp.arange(W_FFT, dtype=jnp.int32)
    same = (c[:, None] // FFT_CH) == (c[None, :] // FFT_CH)
    angc = ((c[:, None] * c[None, :]) % FFT_CH).astype(F32) * (2.0 * math.pi / FFT_CH)
    scale = 1.0 / math.sqrt(seq * FFT_CH)
    chan = jnp.concatenate([jnp.where(same, jnp.cos(angc), 0.0), jnp.where(same, jnp.sin(angc), 0.0)],
                           axis=0) * scale
    return stage1, stage2, chan


def _fft1_body(z_ref, m_ref, are_ref, aim_ref):
    r1 = FFT_R1
    a = _bdot(m_ref[...], z_ref[0])
    are_ref[0] = a[:r1]
    aim_ref[0] = a[r1:]


def _fft2_body(kb, r2, are_ref, aim_ref, m_ref, chan_ref, o_ref):
    for j in range(kb):
        x = jnp.concatenate([are_ref[0, j], aim_ref[0, j]], axis=0)
        f = _bdot(m_ref[j], x)
        fri = jnp.concatenate([f[:r2], f[r2:]], axis=1)
        o_ref[0, :, j * W_FFT:(j + 1) * W_FFT] = _bdot(fri, chan_ref[...])


def _fourier(p_fft, bsz, seq, tables=None):
    r1, r2 = FFT_R1, seq // FFT_R1
    stage1, stage2, chan = tables if tables is not None else _fft_tables(seq)
    cols = r2 * W_FFT
    tc = min(cols, 4096)
    z = p_fft.reshape(bsz, r1, cols)
    blk = pl.BlockSpec((1, r1, tc), lambda b, j: (b, 0, j))
    a_re, a_im = pl.pallas_call(
        _fft1_body,
        grid=(bsz, cols // tc),
        in_specs=[blk, pl.BlockSpec((2 * r1, r1), lambda b, j: (0, 0))],
        out_specs=[blk, blk],
        out_shape=[jax.ShapeDtypeStruct((bsz, r1, cols), F32)] * 2,
        compiler_params=_cparams(("parallel", "parallel")),
        name="fft_stage1",
    )(z, stage1.astype(BF16))
    kb = 8
    ablk = pl.BlockSpec((1, kb, r2, W_FFT), lambda b, j: (b, j, 0, 0))
    out = pl.pallas_call(
        functools.partial(_fft2_body, kb, r2),
        grid=(bsz, r1 // kb),
        in_specs=[ablk, ablk,
                  pl.BlockSpec((kb, 2 * r2, 2 * r2), lambda b, j: (j, 0, 0)),
                  pl.BlockSpec((2 * W_FFT, W_FFT), lambda b, j: (0, 0))],
        out_specs=pl.BlockSpec((1, r2, kb * W_FFT), lambda b, j: (b, 0, j)),
        out_shape=jax.ShapeDtypeStruct((bsz, r2, r1 * W_FFT), F32),
        compiler_params=_cparams(("parallel", "parallel")),
        name="fft_stage2",
    )(a_re.reshape(bsz, r1, r2, W_FFT), a_im.reshape(bsz, r1, r2, W_FFT), stage2.astype(BF16), chan.astype(BF16))
    return out.reshape(bsz * seq, W_FFT)


def _mixout_body(seq, tile, h_ref, ysf_ref, ysb_ref, pssm_ref, yrf_ref, yrb_ref, bnf_ref, bnb_ref, g_ref, pc_ref,
                 pcp_ref, pcn_ref, yf_ref, wout_ref, dskip_ref, gluw_ref, glub_ref, gng_ref, gnb_ref, convw_ref,
                 bmh_ref, lng_ref, lnb_ref, o_ref):
    i = pl.program_id(0)
    y_a = _s5_post(ysf_ref[...], ysb_ref[...], pssm_ref[...], dskip_ref[...], gluw_ref[...], glub_ref[...])
    y_b = _rwkv_post(yrf_ref[...] + yrb_ref[...], bnf_ref[...] + bnb_ref[...], g_ref[...], gng_ref[...],
                     gnb_ref[...], bmh_ref[...])
    pc = pc_ref[...]
    wc = W_CONV
    prev_row, next_row = _halo_rows(pcp_ref, pcn_ref, i * tile, tile, seq)
    z = pc[:, wc:2 * wc] * pc[:, 2 * wc:]
    z_prev, z_next = _shift_rows(z, prev_row[:, wc:2 * wc] * prev_row[:, 2 * wc:],
                                 next_row[:, wc:2 * wc] * next_row[:, 2 * wc:])
    y_c = pc[:, :wc] * (convw_ref[0:1, :] * z_prev + convw_ref[1:2, :] * z + convw_ref[2:3, :] * z_next)
    mix = (_bdot(y_a, wout_ref[0:W_SSM, :]) + _bdot(y_b, wout_ref[W_SSM:W_SSM + W_RWKV, :])
           + _bdot(y_c, wout_ref[W_SSM + W_RWKV:W_SSM + W_RWKV + W_CONV, :])
           + _bdot(yf_ref[...], wout_ref[W_SSM + W_RWKV + W_CONV:, :]))
    o_ref[...] = _layer_norm(DEEPNORM_ALPHA * h_ref[...] + mix, lng_ref[...], lnb_ref[...])


def _mixout(h, seq, tile, y_s5, p_ssm, rwkv_outs, g, p_conv, y_fft, w_out, dskip, glu_w, glu_b, gn_g, gn_b,
            conv_w, ln_g, ln_b):
    n, d = h.shape
    row = lambda w: pl.BlockSpec((tile, w), lambda i: (i, 0))
    pcp, pcn = _halo_specs(tile, 3 * W_CONV, n)
    bmh16 = jnp.asarray(_rwkv_masks(RWKV_HEAD)[2]).astype(BF16)
    consts = (w_out.astype(BF16), dskip.reshape(1, -1), glu_w.astype(BF16), glu_b.reshape(1, -1),
              gn_g.reshape(1, -1), gn_b.reshape(1, -1), conv_w, bmh16, ln_g.reshape(1, -1), ln_b.reshape(1, -1))

    def full(a):
        nd = a.ndim
        return pl.BlockSpec(a.shape, lambda i: (0,) * nd)

    return pl.pallas_call(
        functools.partial(_mixout_body, seq, tile),
        grid=(n // tile,),
        in_specs=[row(d)] + [row(W_SSM)] * 3 + [row(W_RWKV)] * 5
                 + [row(3 * W_CONV), pcp, pcn, row(W_FFT)] + [full(c) for c in consts],
        out_specs=row(d),
        out_shape=jax.ShapeDtypeStruct((n, d), F32),
        compiler_params=_cparams(("parallel",)),
        name="mix_out",
    )(h, *y_s5, p_ssm, *rwkv_outs, g, p_conv, p_conv, p_conv, y_fft, *consts)


def _ffn_body(n_ff, h_ref, w1_ref, w3_ref, w2_ref, lng_ref, lnb_ref, o_ref, acc_ref):
    j = pl.program_id(1)

    @pl.when(j == 0)
    def _():
        acc_ref[...] = jnp.zeros_like(acc_ref)

    x = h_ref[...].astype(BF16)
    u = jax.nn.silu(jnp.dot(x, w1_ref[...], preferred_element_type=F32)) * jnp.dot(
        x, w3_ref[...], preferred_element_type=F32)
    acc_ref[...] += _bdot(u, w2_ref[...])

    @pl.when(j == n_ff - 1)
    def _():
        o_ref[...] = _layer_norm(DEEPNORM_ALPHA * h_ref[...] + acc_ref[...], lng_ref[...], lnb_ref[...])


def _ffn(h, tile, tf, w1, w3, w2, ln_g, ln_b):
    n, d = h.shape
    dff = w1.shape[1]
    n_ff = dff // tf
    return pl.pallas_call(
        functools.partial(_ffn_body, n_ff),
        grid=(n // tile, n_ff),
        in_specs=[pl.BlockSpec((tile, d), lambda i, j: (i, 0)),
                  pl.BlockSpec((d, tf), lambda i, j: (0, j)),
                  pl.BlockSpec((d, tf), lambda i, j: (0, j)),
                  pl.BlockSpec((tf, d), lambda i, j: (j, 0)),
                  pl.BlockSpec((1, d), lambda i, j: (0, 0)),
                  pl.BlockSpec((1, d), lambda i, j: (0, 0))],
        out_specs=pl.BlockSpec((tile, d), lambda i, j: (i, 0)),
        out_shape=jax.ShapeDtypeStruct((n, d), F32),
        scratch_shapes=[pltpu.VMEM((tile, d), F32)],
        compiler_params=_cparams(("parallel", "arbitrary")),
        name="ffn",
    )(h, w1.astype(BF16), w3.astype(BF16), w2.astype(BF16), ln_g.reshape(1, d), ln_b.reshape(1, d))


ROW_TILE = 512
S5_TILE_STEPS = 64
RWKV_CHUNK = 64
FFN_COLS = 1408


def kernel(x, ln0_g, ln0_b, w_in, s5_lambda_re, s5_lambda_im, s5_log_dt, s5_b_re, s5_b_im, s5_c_re, s5_c_im, s5_d,
           s5_glu_w, s5_glu_b, rwkv_mu_rkv, rwkv_mu_w, rwkv_mu_a, rwkv_mu_g, rwkv_w0, rwkv_w1, rwkv_w2, rwkv_a0,
           rwkv_a1, rwkv_a2, rwkv_g1, rwkv_g2, rwkv_k_k, rwkv_k_a, rwkv_r_k, rwkv_gn_g, rwkv_gn_b, conv_w, w_out,
           ln1_g, ln1_b, ffn_w1, ffn_w3, ffn_w2, ln2_g, ln2_b):
    bsz, seq, d = x.shape
    n = bsz * seq
    tile = min(ROW_TILE, seq)
    s5_tl = min(S5_TILE_STEPS, seq // S5_SEGS // 2)
    fft_tables = _fft_tables(seq)
    h = _ln0(x.reshape(n, d), ln0_g, ln0_b, tile)
    for l in range(w_in.shape[0]):
        p_ssm, p_rkv, p_conv, p_fft, logw, agate, g = _inproj(
            h, seq, tile, w_in[l], rwkv_mu_w[l], rwkv_mu_a[l], rwkv_mu_g[l], rwkv_w1[l], rwkv_w2[l], rwkv_w0[l],
            rwkv_a1[l], rwkv_a2[l], rwkv_a0[l], rwkv_g1[l], rwkv_g2[l])
        bb, a, cc = _s5_params(s5_lambda_re[l], s5_lambda_im[l], s5_log_dt[l], s5_b_re[l], s5_b_im[l],
                               s5_c_re[l], s5_c_im[l])
        y_s5 = _s5_scan(p_ssm, bsz, seq, bb, a, cc, s5_tl)
        rwkv_outs = _rwkv_scan(p_rkv, logw, agate, bsz, seq, RWKV_CHUNK, rwkv_mu_rkv[l], rwkv_k_k[l],
                               rwkv_k_a[l], rwkv_r_k[l].reshape(-1))
        y_fft = _fourier(p_fft, bsz, seq, fft_tables)
        h = _mixout(h, seq, tile, y_s5, p_ssm, rwkv_outs, g, p_conv, y_fft, w_out[l], s5_d[l], s5_glu_w[l],
                    s5_glu_b[l], rwkv_gn_g[l], rwkv_gn_b[l], conv_w[l], ln1_g[l], ln1_b[l])
        h = _ffn(h, tile, FFN_COLS, ffn_w1[l], ffn_w3[l], ffn_w2[l], ln2_g[l], ln2_b[l])
    return h.reshape(bsz, seq, d)
```

```python
import functools
import math

import jax
import jax.numpy as jnp
import numpy as np
from jax import lax
from jax.experimental import pallas as pl
from jax.experimental.pallas import tpu as pltpu

W_SSM = 256
W_RWKV = 256
W_CONV = 256
W_FFT = 256
SSM_CH = 16
SSM_GROUPS = 16
SSM_STATE = 64
RWKV_HEAD = 64
RWKV_HEADS = 4
FFT_GROUPS = 4
FFT_CH = 64
RWKV_DECAY_SCALE = math.exp(-0.5)
RWKV_GN_EPS = 64e-5
LN_EPS = 1e-5
DEPTH = 2
DEEPNORM_ALPHA = (2 * DEPTH) ** 0.25

SUBLANES = 8
PACKED_ROWS = 16
VMEM_LIMIT = 48 * 1024 * 1024

BF16 = jnp.bfloat16
F32 = jnp.float32


def _cparams(sem):
    return pltpu.CompilerParams(dimension_semantics=sem, vmem_limit_bytes=VMEM_LIMIT)


def _bdot(a, b):
    return jnp.dot(a.astype(BF16), b.astype(BF16), preferred_element_type=F32)


def _layer_norm(x, g, b):
    mu = jnp.mean(x, axis=-1, keepdims=True)
    xc = x - mu
    var = jnp.mean(xc * xc, axis=-1, keepdims=True)
    return xc * lax.rsqrt(var + LN_EPS) * g + b


def _shift_rows(x, prev_row, next_row):
    n = x.shape[0]
    rows = lax.broadcasted_iota(jnp.int32, x.shape, 0)
    x_prev = jnp.where(rows == 0, prev_row, pltpu.roll(x, 1, axis=0))
    x_next = jnp.where(rows == n - 1, next_row, pltpu.roll(x, n - 1, axis=0))
    return x_prev, x_next


def _halo_rows(prev_ref, next_ref, row0, n_rows, seq):
    first = (row0 % seq) == 0
    last = ((row0 + n_rows) % seq) == 0
    hb = prev_ref.shape[0]
    prev_row = jnp.where(first, 0.0, prev_ref[hb - 1:hb, :].astype(F32))
    next_row = jnp.where(last, 0.0, next_ref[0:1, :].astype(F32))
    return prev_row, next_row


def _halo_specs(tile, width, n_rows_total, hb):
    per = tile // hb
    last_blk = n_rows_total // hb - 1
    prev = pl.BlockSpec((hb, width), lambda i: (jnp.maximum(i * per - 1, 0), 0))
    nxt = pl.BlockSpec((hb, width), lambda i: (jnp.minimum((i + 1) * per, last_blk), 0))
    return prev, nxt


def _ln0_body(x_ref, g_ref, b_ref, o_ref):
    o_ref[...] = _layer_norm(x_ref[...], g_ref[...], b_ref[...])


def _ln0(x2, g, b, tile):
    n, d = x2.shape
    return pl.pallas_call(
        _ln0_body,
        grid=(n // tile,),
        in_specs=[pl.BlockSpec((tile, d), lambda i: (i, 0)),
                  pl.BlockSpec((1, d), lambda i: (0, 0)),
                  pl.BlockSpec((1, d), lambda i: (0, 0))],
        out_specs=pl.BlockSpec((tile, d), lambda i: (i, 0)),
        out_shape=jax.ShapeDtypeStruct((n, d), F32),
        compiler_params=_cparams(("parallel",)),
        name="ln0",
    )(x2, g.reshape(1, d), b.reshape(1, d))


def _inproj_body(seq, tile, h_ref, hp_ref, hn_ref, win_ref, muw_ref, mua_ref, mug_ref,
                 w1_ref, w2_ref, w0_ref, a1_ref, a2_ref, a0_ref, g1_ref, g2_ref,
                 pssm_ref, prkv_ref, pconv_ref, pfft_ref, logw_ref, agate_ref, g_ref):
    i = pl.program_id(0)
    h = h_ref[...]
    prev_row, next_row = _halo_rows(hp_ref, hn_ref, i * tile, tile, seq)
    x_prev, x_next = _shift_rows(h, prev_row, next_row)

    h16 = h.astype(BF16)

    def main():
        col = 0
        for ref in (pssm_ref, prkv_ref, pconv_ref, pfft_ref):
            width = ref.shape[1]
            ref[...] = jnp.dot(h16, win_ref[:, col:col + width], preferred_element_type=F32).astype(ref.dtype)
            col += width
            yield

    def lora(d, x_sh):
        dx = x_sh - h
        t1 = _bdot(h + dx * muw_ref[d:d + 1, :], w1_ref[d])
        yield
        w_lora = _bdot(jnp.tanh(t1), w2_ref[d])
        t2 = _bdot(h + dx * mua_ref[d:d + 1, :], a1_ref[d])
        yield
        logw_ref[d] = -RWKV_DECAY_SCALE * jax.nn.sigmoid(w0_ref[d:d + 1, :] + w_lora)
        a_lora = _bdot(t2, a2_ref[d])
        yield
        agate_ref[d] = jax.nn.sigmoid(a0_ref[d:d + 1, :] + a_lora).astype(agate_ref.dtype)

    def gate():
        xg = h + (0.5 * (x_prev + x_next) - h) * mug_ref[...]
        t3 = _bdot(xg, g1_ref[...])
        yield
        g_ref[...] = _bdot(jax.nn.sigmoid(t3), g2_ref[...]).astype(g_ref.dtype)

    _run_lockstep([main(), lora(0, x_prev), lora(1, x_next), gate()])


def _inproj(h, seq, tile, w_in, mu_w, mu_a, mu_g, w1, w2, w0, a1, a2, a0, g1, g2):
    n, d = h.shape
    hp_spec, hn_spec = _halo_specs(tile, d, n, SUBLANES)

    def full(a):
        nd = a.ndim
        return pl.BlockSpec(a.shape, lambda i: (0,) * nd)

    row = lambda w: pl.BlockSpec((tile, w), lambda i: (i, 0))
    row2 = lambda w: pl.BlockSpec((2, tile, w), lambda i: (0, i, 0))
    consts = (w_in.astype(BF16), mu_w, mu_a, mu_g.reshape(1, d), w1.astype(BF16), w2.astype(BF16), w0,
              a1.astype(BF16), a2.astype(BF16), a0, g1.astype(BF16), g2.astype(BF16))
    return pl.pallas_call(
        functools.partial(_inproj_body, seq, tile),
        grid=(n // tile,),
        in_specs=[pl.BlockSpec((tile, d), lambda i: (i, 0)), hp_spec, hn_spec] + [full(c) for c in consts],
        out_specs=[row(W_SSM), row(3 * W_RWKV), row(3 * W_CONV), row(W_FFT),
                   row2(W_RWKV), row2(W_RWKV), row(W_RWKV)],
        out_shape=[jax.ShapeDtypeStruct((n, W_SSM), BF16),
                   jax.ShapeDtypeStruct((n, 3 * W_RWKV), BF16),
                   jax.ShapeDtypeStruct((n, 3 * W_CONV), BF16),
                   jax.ShapeDtypeStruct((n, W_FFT), BF16),
                   jax.ShapeDtypeStruct((2, n, W_RWKV), F32),
                   jax.ShapeDtypeStruct((2, n, W_RWKV), BF16),
                   jax.ShapeDtypeStruct((n, W_RWKV), BF16)],
        compiler_params=_cparams(("parallel",)),
        name="inproj",
    )(h, h, h, *consts)


S5_SEGS = SUBLANES


def _cmul(ar, ai, br, bi):
    return ar * br - ai * bi, ar * bi + ai * br


class _S5Dir:
    def __init__(self, fwd, tl, ucur_ref, unext_ref, y_ref, bua_ref, bub_ref, st_ref, perm_ref, permt_ref, bb_ref,
                 a_ref, c_ref):
        self.fwd, self.tl = fwd, tl
        self.ucur_ref, self.unext_ref, self.y_ref = ucur_ref, unext_ref, y_ref
        self.bua_ref, self.bub_ref, self.st_ref = bua_ref, bub_ref, st_ref
        self.perm_ref, self.permt_ref, self.bb_ref, self.c_ref = perm_ref, permt_ref, bb_ref, c_ref
        self.half = SSM_GROUPS * SSM_STATE
        self.rows = tl * S5_SEGS
        self.n_parts = S5_SEGS
        self.part = 2 * self.half // self.n_parts
        shape = (S5_SEGS, self.half)
        self.ar = jnp.broadcast_to(a_ref[:, :self.half], shape)
        self.ai = jnp.broadcast_to(a_ref[:, self.half:], shape)
        self.off_a, self.off_b = (0, tl) if fwd else (tl, 0)

    def project(self, u_ref, off, buf_ref):
        u = u_ref[0, :, off:off + self.tl, :].reshape(self.rows, W_SSM).astype(BF16)
        up = jnp.dot(self.perm_ref[...], u, preferred_element_type=F32).astype(BF16)
        for k in range(self.n_parts):
            cols = slice(k * self.part, (k + 1) * self.part)
            buf_ref[:, cols] = jnp.dot(up, self.bb_ref[:, cols], preferred_element_type=F32)
            yield

    def scan(self, buf_ref, store):
        half, tl = self.half, self.tl
        xr, xi = self.st_ref[:, :half], self.st_ref[:, half:]
        for t in range(tl):
            row = (t if self.fwd else tl - 1 - t) * S5_SEGS
            nr = self.ar * xr - self.ai * xi + buf_ref[row:row + S5_SEGS, :half]
            ni = self.ar * xi + self.ai * xr + buf_ref[row:row + S5_SEGS, half:]
            if store:
                buf_ref[row:row + S5_SEGS, :half] = nr
                buf_ref[row:row + S5_SEGS, half:] = ni
            xr, xi = nr, ni
            if (t + 1) % (tl // self.n_parts) == 0:
                yield
        self.st_ref[:, :half] = xr
        self.st_ref[:, half:] = xi

    def emit(self, buf_ref, off):
        y = None
        for k in range(self.n_parts):
            cols = slice(k * self.part, (k + 1) * self.part)
            yk = _bdot(buf_ref[:, cols], self.c_ref[cols, :])
            y = yk if y is None else y + yk
            yield
        y = jnp.dot(self.permt_ref[...], y.astype(BF16), preferred_element_type=F32)
        self.y_ref[0, :, off:off + self.tl, :] = y.reshape(S5_SEGS, self.tl, W_SSM).astype(self.y_ref.dtype)

    def init_state(self, seg_len):
        half = self.half
        pr, pi = jnp.ones_like(self.ar), jnp.zeros_like(self.ar)
        br, bi = self.ar, self.ai
        e = seg_len
        while e:
            if e & 1:
                pr, pi = _cmul(pr, pi, br, bi)
            br, bi = _cmul(br, bi, br, bi)
            e >>= 1
        er, ei = self.st_ref[:, :half], self.st_ref[:, half:]
        zero = jnp.zeros((1, half), F32)
        order = range(S5_SEGS) if self.fwd else range(S5_SEGS - 1, -1, -1)
        cr, ci, out_r, out_i = zero, zero, {}, {}
        for j in order:
            out_r[j], out_i[j] = cr, ci
            nr, ni = _cmul(pr[0:1], pi[0:1], cr, ci)
            cr, ci = nr + er[j:j + 1], ni + ei[j:j + 1]
        self.st_ref[:, :half] = jnp.concatenate([out_r[j] for j in range(S5_SEGS)], axis=0)
        self.st_ref[:, half:] = jnp.concatenate([out_i[j] for j in range(S5_SEGS)], axis=0)


def _s5_body(tl, seg_len, ucf_ref, unf_ref, ucb_ref, unb_ref, perm_ref, permt_ref, bb_ref, a_ref, c_ref, yf_ref,
             yb_ref, baf_ref, bbf_ref, bab_ref, bbb_ref, st_ref):
    ps = pl.program_id(1)
    i = pl.program_id(2)
    dirs = [_S5Dir(True, tl, ucf_ref, unf_ref, yf_ref, baf_ref, bbf_ref, st_ref.at[0], perm_ref, permt_ref,
                   bb_ref.at[0], a_ref.at[0], c_ref.at[0]),
            _S5Dir(False, tl, ucb_ref, unb_ref, yb_ref, bab_ref, bbb_ref, st_ref.at[1], perm_ref, permt_ref,
                   bb_ref.at[1], a_ref.at[1], c_ref.at[1])]

    @pl.when(jnp.logical_and(i == 0, ps == 0))
    def _():
        st_ref[...] = jnp.zeros_like(st_ref)

    @pl.when(jnp.logical_and(i == 0, ps == 1))
    def _():
        for z in dirs:
            z.init_state(seg_len)

    @pl.when(i == 0)
    def _():
        _run_lockstep([z.project(z.ucur_ref, z.off_a, z.bua_ref) for z in dirs])

    @pl.when(ps == 0)
    def _():
        _run_lockstep([g for z in dirs for g in (z.scan(z.bua_ref, False), z.project(z.ucur_ref, z.off_b, z.bub_ref))])
        _run_lockstep([g for z in dirs for g in (z.scan(z.bub_ref, False), z.project(z.unext_ref, z.off_a, z.bua_ref))])

    @pl.when(ps == 1)
    def _():
        _run_lockstep([g for z in dirs for g in (z.scan(z.bua_ref, True), z.project(z.ucur_ref, z.off_b, z.bub_ref))])
        _run_lockstep([g for z in dirs for g in (z.scan(z.bub_ref, True), z.emit(z.bua_ref, z.off_a),
                                                 z.project(z.unext_ref, z.off_a, z.bua_ref))])
        _run_lockstep([z.emit(z.bub_ref, z.off_b) for z in dirs])


def _s5_params(lam_re, lam_im, log_dt, b_re, b_im, c_re, c_im):
    g, p, hch = SSM_GROUPS, SSM_STATE, SSM_CH
    dt = jnp.exp(log_dt)[..., None]
    mag = jnp.exp(lam_re * dt)
    lb_re = mag * jnp.cos(lam_im * dt)
    lb_im = mag * jnp.sin(lam_im * dt)
    den = lam_re * lam_re + lam_im * lam_im
    nr = lb_re - 1.0
    coef_re = (nr * lam_re + lb_im * lam_im) / den
    coef_im = (lb_im * lam_re - nr * lam_im) / den
    bb_re = coef_re[..., None] * b_re - coef_im[..., None] * b_im
    bb_im = coef_re[..., None] * b_im + coef_im[..., None] * b_re
    eye = jnp.eye(g, dtype=F32)
    bd_in = lambda m: jnp.einsum('dgph,gk->dghkp', m, eye).reshape(2, g * hch, g * p)
    bb = jnp.concatenate([bd_in(bb_re), bd_in(bb_im)], axis=-1)
    bd_out = lambda m: jnp.einsum('dghp,gk->dgpkh', m, eye).reshape(2, g * p, g * hch)
    cc = jnp.concatenate([bd_out(c_re), -bd_out(c_im)], axis=1)
    a = jnp.concatenate([lb_re.reshape(2, 1, g * p), lb_im.reshape(2, 1, g * p)], axis=-1)
    return bb, a, cc


def _s5_scan(p_ssm, bsz, seq, bb, a, cc, tl):
    seg_len = seq // S5_SEGS
    n_pairs = seg_len // (2 * tl)
    rows = tl * S5_SEGS
    w2 = 2 * SSM_GROUPS * SSM_STATE
    src = (np.arange(rows) % S5_SEGS) * tl + np.arange(rows) // S5_SEGS
    perm = np.zeros((rows, rows), np.float32)
    perm[np.arange(rows), src] = 1.0
    last = n_pairs - 1
    u4 = p_ssm.reshape(bsz, S5_SEGS, seg_len, W_SSM)
    ublk = (1, S5_SEGS, 2 * tl, W_SSM)
    const = lambda shape: pl.BlockSpec(shape, lambda b, ps, i: (0,) * len(shape))
    y_f, y_b = pl.pallas_call(
        functools.partial(_s5_body, tl, seg_len),
        grid=(bsz, 2, n_pairs),
        in_specs=[pl.BlockSpec(ublk, lambda b, ps, i: (b, 0, i, 0)),
                  pl.BlockSpec(ublk, lambda b, ps, i: (b, 0, jnp.minimum(i + 1, last), 0)),
                  pl.BlockSpec(ublk, lambda b, ps, i: (b, 0, last - i, 0)),
                  pl.BlockSpec(ublk, lambda b, ps, i: (b, 0, jnp.maximum(last - i - 1, 0), 0)),
                  const((rows, rows)), const((rows, rows)), const((2, W_SSM, w2)), const((2, 1, w2)),
                  const((2, w2, W_SSM))],
        out_specs=[pl.BlockSpec(ublk, lambda b, ps, i: (b, 0, jnp.where(ps == 0, 0, i), 0)),
                   pl.BlockSpec(ublk, lambda b, ps, i: (b, 0, jnp.where(ps == 0, last, last - i), 0))],
        out_shape=[jax.ShapeDtypeStruct((bsz, S5_SEGS, seg_len, W_SSM), BF16)] * 2,
        scratch_shapes=[pltpu.VMEM((rows, w2), F32)] * 4 + [pltpu.VMEM((2, S5_SEGS, w2), F32)],
        compiler_params=_cparams(("arbitrary",) * 3),
        name="s5_scan",
    )(u4, u4, u4, u4, jnp.asarray(perm).astype(BF16), jnp.asarray(perm.T).astype(BF16), bb.astype(BF16), a,
      cc.astype(BF16))
    return y_f.reshape(bsz * seq, W_SSM), y_b.reshape(bsz * seq, W_SSM)


def _gelu_tanh(x):
    c = math.sqrt(2.0 / math.pi)
    return 0.5 * x * (1.0 + jnp.tanh(c * (x + 0.044715 * (x * x * x))))


def _s5_post(y_f, y_b, u, dskip, glu_w, glu_b):
    y = _gelu_tanh(y_f.astype(F32) + y_b.astype(F32) + dskip * u)
    return y * jax.nn.sigmoid(_bdot(y, glu_w) + glu_b)


def _split_dot(x, w):
    hi = x.astype(BF16)
    lo = (x - hi.astype(F32)).astype(BF16)
    return (jnp.dot(hi, w, preferred_element_type=F32) + jnp.dot(lo, w, preferred_element_type=F32))


def _dot_nt(a, b):
    return lax.dot_general(a.astype(BF16), b.astype(BF16), (((1,), (1,)), ((), ())),
                           preferred_element_type=F32)


def _rwkv_masks(chunk):
    hh = RWKV_HEADS
    r = np.arange(hh * chunk)[:, None] // chunk
    bm_feat = (r == np.arange(W_RWKV)[None, :] // RWKV_HEAD).astype(np.float32)
    bm_time = (r == np.arange(hh * chunk)[None, :] // chunk).astype(np.float32)
    f = np.arange(W_RWKV)
    bm_head = (f[:, None] // RWKV_HEAD == f[None, :] // RWKV_HEAD).astype(np.float32)
    return bm_feat, bm_time, bm_head


def _rwkv_chunk(fwd, x, edge_row, logw, a, mu, k_k, k_a, r_k, bmf16, bmt16, bmh, h_old):
    ll = x.shape[0]
    w = W_RWKV
    rows = lax.broadcasted_iota(jnp.int32, x.shape, 0)
    if fwd:
        shifted = jnp.where(rows == 0, edge_row, pltpu.roll(x, 1, axis=0))
    else:
        shifted = jnp.where(rows == ll - 1, edge_row, pltpu.roll(x, ll - 1, axis=0))
    rkv = x + (shifted - x) * mu
    r, k, v = rkv[:, :w], rkv[:, w:2 * w], rkv[:, 2 * w:]
    bmh16 = bmh.astype(BF16)

    kk = k * k_k
    ksq = _bdot(kk * kk, bmh16)
    k2 = k * (1.0 + (a - 1.0) * k_a)
    bonus = _bdot(r * k2 * r_k, bmh16) * v

    ti = lax.broadcasted_iota(jnp.int32, (ll, ll), 0)
    si = lax.broadcasted_iota(jnp.int32, (ll, ll), 1)
    tri = ((si <= ti) if fwd else (si >= ti)).astype(F32).astype(BF16)
    cum = _split_dot_left(tri, logw)
    yield
    kk = kk * lax.rsqrt(ksq + 1e-12)
    ctot = jnp.sum(logw, axis=0, keepdims=True)
    e_neg = jnp.exp(-cum)
    e_rem = jnp.exp(ctot - cum)
    ah = -kk * jnp.exp(cum - logw)
    rh = r * jnp.exp(cum)
    bvec = kk * a
    bh, kh = bvec * e_neg, k2 * e_neg
    bt, kt = bvec * e_rem, k2 * e_rem

    def bd(m):
        m16 = m.astype(BF16)
        return jnp.concatenate([m16] * RWKV_HEADS, axis=0) * (bmf16 if m.shape[1] == w else bmt16)

    gram = _dot_nt(jnp.concatenate([ah, rh], axis=0), jnp.concatenate([bd(bh), bd(kh)], axis=0))
    yield
    l4 = RWKV_HEADS * ll
    tt = lax.broadcasted_iota(jnp.int32, (ll, l4), 0)
    ss = lax.broadcasted_iota(jnp.int32, (ll, l4), 1) % ll
    strict = (ss < tt) if fwd else (ss > tt)
    incl = (ss <= tt) if fwd else (ss >= tt)
    n_ab = jnp.where(strict, gram[:ll, :l4], 0.0)
    a_ak = jnp.where(strict, gram[:ll, l4:], 0.0)
    a_rb = jnp.where(incl, gram[ll:, :l4], 0.0)
    a_rk = jnp.where(incl, gram[ll:, l4:], 0.0)

    pw = n_ab
    tinv = jnp.where(ss == tt, 1.0, 0.0) + n_ab
    akv = _bdot(a_ak, bd(v))
    for _ in range(int(math.log2(ll)) - 1):
        pw = _bdot(pw, bd(pw))
        yield
        tinv = tinv + _bdot(tinv, bd(pw))
        yield

    ta = _bdot(tinv, jnp.concatenate([bd(ah), bd(akv)], axis=1))
    yield
    ap, wm = ta[:, :w], ta[:, w:]
    rp = rh + _bdot(a_rb, bd(ap))
    y0 = _bdot(jnp.concatenate([a_rb, a_rk], axis=1), jnp.concatenate([bd(wm), bd(v)], axis=0))
    btk = jnp.transpose(jnp.concatenate([bt, kt], axis=0))
    rhs = jnp.concatenate([jnp.concatenate([ap, wm], axis=1),
                           jnp.concatenate([jnp.zeros_like(v), v], axis=1)], axis=0)
    pq = _bdot(btk, rhs)
    yield
    eye = (lax.broadcasted_iota(jnp.int32, (w, w), 0) == lax.broadcasted_iota(jnp.int32, (w, w), 1))
    pm = pq[:, :w] * bmh + jnp.where(eye, jnp.exp(ctot), 0.0)
    qm = pq[:, w:] * bmh
    yield _bdot(rp, h_old) + y0, bonus, _bdot(pm, h_old) + qm


def _run_lockstep(gens):
    results = [None] * len(gens)
    live = list(range(len(gens)))
    while live:
        for s in list(live):
            try:
                out = next(gens[s])
            except StopIteration:
                live.remove(s)
            else:
                if out is not None:
                    results[s] = out
    return results


def _rwkv_body(n_chunks, nb, *refs):
    dir_refs = [refs[0:4], refs[4:8]]
    mu_ref, kk_ref, ka_ref, rk_ref, bmf_ref, bmt_ref, bmh_ref = refs[8:15]
    y_refs, bonus_refs, h_ref = refs[15:17], refs[17:19], refs[19]
    c = pl.program_id(0)

    @pl.when(c == 0)
    def _():
        h_ref[...] = jnp.zeros_like(h_ref)

    gens = []
    for d in range(2):
        x_ref, edge_ref, logw_ref, a_ref = dir_refs[d]
        for b in range(nb):
            hb = edge_ref.shape[1]
            edge = edge_ref[b, hb - 1:hb, :] if d == 0 else edge_ref[b, 0:1, :]
            edge = jnp.where(c == 0, 0.0, edge.astype(F32))
            gens.append(_rwkv_chunk(d == 0, x_ref[b].astype(F32), edge, logw_ref[0, b], a_ref[0, b].astype(F32),
                                    mu_ref[d], kk_ref[...], ka_ref[...], rk_ref[...], bmf_ref[...], bmt_ref[...],
                                    bmh_ref[...], h_ref[d * nb + b]))
    for s, (y, bonus, h_new) in enumerate(_run_lockstep(gens)):
        y_refs[s // nb][s % nb] = y.astype(y_refs[0].dtype)
        bonus_refs[s // nb][s % nb] = bonus.astype(bonus_refs[0].dtype)
        h_ref[s] = h_new


def _split_dot_left(w, x):
    hi = x.astype(BF16)
    lo = (x - hi.astype(F32)).astype(BF16)
    return (jnp.dot(w, hi, preferred_element_type=F32) + jnp.dot(w, lo, preferred_element_type=F32))


def _rwkv_scan(p_rkv, logw, agate, bsz, seq, chunk, mu_rkv, k_k, k_a, r_k):
    n = bsz * seq
    n_chunks = seq // chunk
    hb = PACKED_ROWS
    per = chunk // hb
    w3 = 3 * W_RWKV
    bmf, bmt, bmh = _rwkv_masks(chunk)
    consts = (mu_rkv.reshape(2, 1, w3), k_k.reshape(1, -1), k_a.reshape(1, -1), r_k.reshape(1, -1),
              jnp.asarray(bmf).astype(BF16), jnp.asarray(bmt).astype(BF16), jnp.asarray(bmh))
    x3 = p_rkv.reshape(bsz, seq, w3)
    logw4 = logw.reshape(2, bsz, seq, W_RWKV)
    a4 = agate.reshape(2, bsz, seq, W_RWKV)
    args, in_specs, out_specs = [], [], []
    for d in range(2):
        cidx = (lambda c: c) if d == 0 else (lambda c: n_chunks - 1 - c)
        if d == 0:
            edge = lambda c: (0, jnp.maximum(c * per - 1, 0), 0)
        else:
            edge = lambda c: (0, jnp.minimum((n_chunks - c) * per, seq // hb - 1), 0)
        args += [x3, x3, logw4, a4]
        in_specs += [pl.BlockSpec((bsz, chunk, w3), lambda c, cidx=cidx: (0, cidx(c), 0)),
                     pl.BlockSpec((bsz, hb, w3), edge),
                     pl.BlockSpec((1, bsz, chunk, W_RWKV), lambda c, d=d, cidx=cidx: (d, 0, cidx(c), 0)),
                     pl.BlockSpec((1, bsz, chunk, W_RWKV), lambda c, d=d, cidx=cidx: (d, 0, cidx(c), 0))]
        out_specs.append(pl.BlockSpec((bsz, chunk, W_RWKV), lambda c, cidx=cidx: (0, cidx(c), 0)))

    def full(a):
        nd = a.ndim
        return pl.BlockSpec(a.shape, lambda c: (0,) * nd)

    y_f, y_b, bonus_f, bonus_b = pl.pallas_call(
        functools.partial(_rwkv_body, n_chunks, bsz),
        grid=(n_chunks,),
        in_specs=in_specs + [full(a) for a in consts],
        out_specs=out_specs + out_specs,
        out_shape=[jax.ShapeDtypeStruct((bsz, seq, W_RWKV), BF16)] * 4,
        scratch_shapes=[pltpu.VMEM((2 * bsz, W_RWKV, W_RWKV), F32)],
        compiler_params=_cparams(("arbitrary",)),
        name="rwkv_scan",
    )(*args, *consts)
    return [a.reshape(n, W_RWKV) for a in (y_f, y_b, bonus_f, bonus_b)]


def _head_mean(x, bmh16):
    return _split_dot(x, bmh16) * (1.0 / RWKV_HEAD)


def _rwkv_post(y, bonus, g, gn_g, gn_b, bmh16):
    mu = _head_mean(y, bmh16)
    yc = y - mu
    var = _head_mean(yc * yc, bmh16)
    yn = yc * lax.rsqrt(var + RWKV_GN_EPS) * gn_g + gn_b
    return (yn + bonus) * g


FFT_R1 = 64


def _fft_tables(seq):
    r1, r2 = FFT_R1, seq // FFT_R1
    i1 = jnp.arange(r1, dtype=jnp.int32)
    ang1 = ((i1[:, None] * i1[None, :]) % r1).astype(F32) * (2.0 * math.pi / r1)
    stage1 = jnp.concatenate([jnp.cos(ang1), -jnp.sin(ang1)], axis=0)
    k1 = i1[:, None, None]
    k2 = jnp.arange(r2, dtype=jnp.int32)[None, :, None]
    n2 = jnp.arange(r2, dtype=jnp.int32)[None, None, :]
    ang2 = ((n2 * (k1 + r1 * k2)) % seq).astype(F32) * (2.0 * math.pi / seq)
    mr, mi = jnp.cos(ang2), -jnp.sin(ang2)
    stage2 = jnp.concatenate([jnp.concatenate([mr, -mi], axis=2),
                              jnp.concatenate([mi, mr], axis=2)], axis=1)
    c = jnp.arange(W_FFT, dtype=jnp.int32)
    same = (c[:, None] // FFT_CH) == (c[None, :] // FFT_CH)
    angc = ((c[:, None] * c[None, :]) % FFT_CH).astype(F32) * (2.0 * math.pi / FFT_CH)
    scale = 1.0 / math.sqrt(seq * FFT_CH)
    chan = jnp.concatenate([jnp.where(same, jnp.cos(angc), 0.0), jnp.where(same, jnp.sin(angc), 0.0)],
                           axis=0) * scale
    return stage1, stage2, chan


def _fft1_body(z_ref, m_ref, are_ref, aim_ref):
    r1 = FFT_R1
    a = _bdot(m_ref[...], z_ref[0])
    are_ref[0] = a[:r1].astype(are_ref.dtype)
    aim_ref[0] = a[r1:].astype(aim_ref.dtype)


def _fft2_body(kb, r2, are_ref, aim_ref, m_ref, chan_ref, o_ref):
    for j in range(kb):
        x = jnp.concatenate([are_ref[0, j], aim_ref[0, j]], axis=0)
        f = _bdot(m_ref[j], x)
        fri = jnp.concatenate([f[:r2], f[r2:]], axis=1)
        o_ref[0, :, j * W_FFT:(j + 1) * W_FFT] = _bdot(fri, chan_ref[...]).astype(o_ref.dtype)


def _fourier(p_fft, bsz, seq, tables=None):
    r1, r2 = FFT_R1, seq // FFT_R1
    stage1, stage2, chan = tables if tables is not None else _fft_tables(seq)
    cols = r2 * W_FFT
    tc = min(cols, 4096)
    z = p_fft.reshape(bsz, r1, cols)
    blk = pl.BlockSpec((1, r1, tc), lambda b, j: (b, 0, j))
    a_re, a_im = pl.pallas_call(
        _fft1_body,
        grid=(bsz, cols // tc),
        in_specs=[blk, pl.BlockSpec((2 * r1, r1), lambda b, j: (0, 0))],
        out_specs=[blk, blk],
        out_shape=[jax.ShapeDtypeStruct((bsz, r1, cols), BF16)] * 2,
        compiler_params=_cparams(("parallel", "parallel")),
        name="fft_stage1",
    )(z, stage1.astype(BF16))
    kb = 8
    ablk = pl.BlockSpec((1, kb, r2, W_FFT), lambda b, j: (b, j, 0, 0))
    out = pl.pallas_call(
        functools.partial(_fft2_body, kb, r2),
        grid=(bsz, r1 // kb),
        in_specs=[ablk, ablk,
                  pl.BlockSpec((kb, 2 * r2, 2 * r2), lambda b, j: (j, 0, 0)),
                  pl.BlockSpec((2 * W_FFT, W_FFT), lambda b, j: (0, 0))],
        out_specs=pl.BlockSpec((1, r2, kb * W_FFT), lambda b, j: (b, 0, j)),
        out_shape=jax.ShapeDtypeStruct((bsz, r2, r1 * W_FFT), BF16),
        compiler_params=_cparams(("parallel", "parallel")),
        name="fft_stage2",
    )(a_re.reshape(bsz, r1, r2, W_FFT), a_im.reshape(bsz, r1, r2, W_FFT), stage2.astype(BF16), chan.astype(BF16))
    return out.reshape(bsz * seq, W_FFT)


def _mixout_body(seq, tile, h_ref, ysf_ref, ysb_ref, pssm_ref, yrf_ref, yrb_ref, bnf_ref, bnb_ref, g_ref, pc_ref,
                 pcp_ref, pcn_ref, yf_ref, wout_ref, dskip_ref, gluw_ref, glub_ref, gng_ref, gnb_ref, convw_ref,
                 bmh_ref, lng_ref, lnb_ref, o_ref):
    i = pl.program_id(0)
    f32 = lambda ref: ref[...].astype(F32)
    y_a = _s5_post(f32(ysf_ref), f32(ysb_ref), f32(pssm_ref), dskip_ref[...], gluw_ref[...], glub_ref[...])
    y_b = _rwkv_post(f32(yrf_ref) + f32(yrb_ref), f32(bnf_ref) + f32(bnb_ref), f32(g_ref), gng_ref[...],
                     gnb_ref[...], bmh_ref[...])
    pc = f32(pc_ref)
    wc = W_CONV
    prev_row, next_row = _halo_rows(pcp_ref, pcn_ref, i * tile, tile, seq)
    z = pc[:, wc:2 * wc] * pc[:, 2 * wc:]
    z_prev, z_next = _shift_rows(z, prev_row[:, wc:2 * wc] * prev_row[:, 2 * wc:],
                                 next_row[:, wc:2 * wc] * next_row[:, 2 * wc:])
    y_c = pc[:, :wc] * (convw_ref[0:1, :] * z_prev + convw_ref[1:2, :] * z + convw_ref[2:3, :] * z_next)
    mix = (_bdot(y_a, wout_ref[0:W_SSM, :]) + _bdot(y_b, wout_ref[W_SSM:W_SSM + W_RWKV, :])
           + _bdot(y_c, wout_ref[W_SSM + W_RWKV:W_SSM + W_RWKV + W_CONV, :])
           + _bdot(yf_ref[...], wout_ref[W_SSM + W_RWKV + W_CONV:, :]))
    o_ref[...] = _layer_norm(DEEPNORM_ALPHA * h_ref[...] + mix, lng_ref[...], lnb_ref[...])


def _mixout(h, seq, tile, y_s5, p_ssm, rwkv_outs, g, p_conv, y_fft, w_out, dskip, glu_w, glu_b, gn_g, gn_b,
            conv_w, ln_g, ln_b):
    n, d = h.shape
    row = lambda w: pl.BlockSpec((tile, w), lambda i: (i, 0))
    pcp, pcn = _halo_specs(tile, 3 * W_CONV, n, PACKED_ROWS)
    bmh16 = jnp.asarray(_rwkv_masks(RWKV_HEAD)[2]).astype(BF16)
    consts = (w_out.astype(BF16), dskip.reshape(1, -1), glu_w.astype(BF16), glu_b.reshape(1, -1),
              gn_g.reshape(1, -1), gn_b.reshape(1, -1), conv_w, bmh16, ln_g.reshape(1, -1), ln_b.reshape(1, -1))

    def full(a):
        nd = a.ndim
        return pl.BlockSpec(a.shape, lambda i: (0,) * nd)

    return pl.pallas_call(
        functools.partial(_mixout_body, seq, tile),
        grid=(n // tile,),
        in_specs=[row(d)] + [row(W_SSM)] * 3 + [row(W_RWKV)] * 5
                 + [row(3 * W_CONV), pcp, pcn, row(W_FFT)] + [full(c) for c in consts],
        out_specs=row(d),
        out_shape=jax.ShapeDtypeStruct((n, d), F32),
        compiler_params=_cparams(("parallel",)),
        name="mix_out",
    )(h, *y_s5, p_ssm, *rwkv_outs, g, p_conv, p_conv, p_conv, y_fft, *consts)


def _ffn_body(n_ff, h_ref, w1_ref, w3_ref, w2_ref, lng_ref, lnb_ref, o_ref, acc_ref):
    j = pl.program_id(1)

    @pl.when(j == 0)
    def _():
        acc_ref[...] = jnp.zeros_like(acc_ref)

    x = h_ref[...].astype(BF16)
    u = jax.nn.silu(jnp.dot(x, w1_ref[...], preferred_element_type=F32)) * jnp.dot(
        x, w3_ref[...], preferred_element_type=F32)
    acc_ref[...] += _bdot(u, w2_ref[...])

    @pl.when(j == n_ff - 1)
    def _():
        o_ref[...] = _layer_norm(DEEPNORM_ALPHA * h_ref[...] + acc_ref[...], lng_ref[...], lnb_ref[...])


def _ffn(h, tile, tf, w1, w3, w2, ln_g, ln_b):
    n, d = h.shape
    dff = w1.shape[1]
    n_ff = dff // tf
    return pl.pallas_call(
        functools.partial(_ffn_body, n_ff),
        grid=(n // tile, n_ff),
        in_specs=[pl.BlockSpec((tile, d), lambda i, j: (i, 0)),
                  pl.BlockSpec((d, tf), lambda i, j: (0, j)),
                  pl.BlockSpec((d, tf), lambda i, j: (0, j)),
                  pl.BlockSpec((tf, d), lambda i, j: (j, 0)),
                  pl.BlockSpec((1, d), lambda i, j: (0, 0)),
                  pl.BlockSpec((1, d), lambda i, j: (0, 0))],
        out_specs=pl.BlockSpec((tile, d), lambda i, j: (i, 0)),
        out_shape=jax.ShapeDtypeStruct((n, d), F32),
        scratch_shapes=[pltpu.VMEM((tile, d), F32)],
        compiler_params=_cparams(("parallel", "arbitrary")),
        name="ffn",
    )(h, w1.astype(BF16), w3.astype(BF16), w2.astype(BF16), ln_g.reshape(1, d), ln_b.reshape(1, d))


ROW_TILE = 512
S5_TILE_STEPS = 64
RWKV_CHUNK = 64
FFN_COLS = 1408


def kernel(x, ln0_g, ln0_b, w_in, s5_lambda_re, s5_lambda_im, s5_log_dt, s5_b_re, s5_b_im, s5_c_re, s5_c_im, s5_d,
           s5_glu_w, s5_glu_b, rwkv_mu_rkv, rwkv_mu_w, rwkv_mu_a, rwkv_mu_g, rwkv_w0, rwkv_w1, rwkv_w2, rwkv_a0,
           rwkv_a1, rwkv_a2, rwkv_g1, rwkv_g2, rwkv_k_k, rwkv_k_a, rwkv_r_k, rwkv_gn_g, rwkv_gn_b, conv_w, w_out,
           ln1_g, ln1_b, ffn_w1, ffn_w3, ffn_w2, ln2_g, ln2_b):
    bsz, seq, d = x.shape
    n = bsz * seq
    tile = min(ROW_TILE, seq)
    s5_tl = min(S5_TILE_STEPS, seq // S5_SEGS // 2)
    fft_tables = _fft_tables(seq)
    h = _ln0(x.reshape(n, d), ln0_g, ln0_b, tile)
    for l in range(w_in.shape[0]):
        p_ssm, p_rkv, p_conv, p_fft, logw, agate, g = _inproj(
            h, seq, tile, w_in[l], rwkv_mu_w[l], rwkv_mu_a[l], rwkv_mu_g[l], rwkv_w1[l], rwkv_w2[l], rwkv_w0[l],
            rwkv_a1[l], rwkv_a2[l], rwkv_a0[l], rwkv_g1[l], rwkv_g2[l])
        bb, a, cc = _s5_params(s5_lambda_re[l], s5_lambda_im[l], s5_log_dt[l], s5_b_re[l], s5_b_im[l],
                               s5_c_re[l], s5_c_im[l])
        y_s5 = _s5_scan(p_ssm, bsz, seq, bb, a, cc, s5_tl)
        rwkv_outs = _rwkv_scan(p_rkv, logw, agate, bsz, seq, RWKV_CHUNK, rwkv_mu_rkv[l], rwkv_k_k[l],
                               rwkv_k_a[l], rwkv_r_k[l].reshape(-1))
        y_fft = _fourier(p_fft, bsz, seq, fft_tables)
        h = _mixout(h, seq, tile, y_s5, p_ssm, rwkv_outs, g, p_conv, y_fft, w_out[l], s5_d[l], s5_glu_w[l],
                    s5_glu_b[l], rwkv_gn_g[l], rwkv_gn_b[l], conv_w[l], ln1_g[l], ln1_b[l])
        h = _ffn(h, tile, FFN_COLS, ffn_w1[l], ffn_w3[l], ffn_w2[l], ln2_g[l], ln2_b[l])
    return h.reshape(bsz, seq, d)
```

```python
import functools
import math

import jax
import jax.numpy as jnp
import numpy as np
from jax import lax
from jax.experimental import pallas as pl
from jax.experimental.pallas import tpu as pltpu

W_SSM = 256
W_RWKV = 256
W_CONV = 256
W_FFT = 256
SSM_CH = 16
SSM_GROUPS = 16
SSM_STATE = 64
RWKV_HEAD = 64
RWKV_HEADS = 4
FFT_GROUPS = 4
FFT_CH = 64
RWKV_DECAY_SCALE = math.exp(-0.5)
RWKV_GN_EPS = 64e-5
LN_EPS = 1e-5
DEPTH = 2
DEEPNORM_ALPHA = (2 * DEPTH) ** 0.25

SUBLANES = 8
PACKED_ROWS = 16
VMEM_LIMIT = 48 * 1024 * 1024

BF16 = jnp.bfloat16
F32 = jnp.float32


def _cparams(sem):
    return pltpu.CompilerParams(dimension_semantics=sem, vmem_limit_bytes=VMEM_LIMIT)


def _bdot(a, b):
    return jnp.dot(a.astype(BF16), b.astype(BF16), preferred_element_type=F32)


def _layer_norm(x, g, b):
    mu = jnp.mean(x, axis=-1, keepdims=True)
    xc = x - mu
    var = jnp.mean(xc * xc, axis=-1, keepdims=True)
    return xc * lax.rsqrt(var + LN_EPS) * g + b


def _shift_rows(x, prev_row, next_row):
    n = x.shape[0]
    rows = lax.broadcasted_iota(jnp.int32, x.shape, 0)
    x_prev = jnp.where(rows == 0, prev_row, pltpu.roll(x, 1, axis=0))
    x_next = jnp.where(rows == n - 1, next_row, pltpu.roll(x, n - 1, axis=0))
    return x_prev, x_next


def _halo_rows(prev_ref, next_ref, row0, n_rows, seq):
    first = (row0 % seq) == 0
    last = ((row0 + n_rows) % seq) == 0
    hb = prev_ref.shape[0]
    prev_row = jnp.where(first, 0.0, prev_ref[hb - 1:hb, :].astype(F32))
    next_row = jnp.where(last, 0.0, next_ref[0:1, :].astype(F32))
    return prev_row, next_row


def _halo_specs(tile, width, n_rows_total, hb):
    per = tile // hb
    last_blk = n_rows_total // hb - 1
    prev = pl.BlockSpec((hb, width), lambda i: (jnp.maximum(i * per - 1, 0), 0))
    nxt = pl.BlockSpec((hb, width), lambda i: (jnp.minimum((i + 1) * per, last_blk), 0))
    return prev, nxt


def _ln0_body(x_ref, g_ref, b_ref, o_ref):
    o_ref[...] = _layer_norm(x_ref[...], g_ref[...], b_ref[...])


def _ln0(x2, g, b, tile):
    n, d = x2.shape
    return pl.pallas_call(
        _ln0_body,
        grid=(n // tile,),
        in_specs=[pl.BlockSpec((tile, d), lambda i: (i, 0)),
                  pl.BlockSpec((1, d), lambda i: (0, 0)),
                  pl.BlockSpec((1, d), lambda i: (0, 0))],
        out_specs=pl.BlockSpec((tile, d), lambda i: (i, 0)),
        out_shape=jax.ShapeDtypeStruct((n, d), F32),
        compiler_params=_cparams(("parallel",)),
        name="ln0",
    )(x2, g.reshape(1, d), b.reshape(1, d))


def _inproj_body(seq, tile, h_ref, hp_ref, hn_ref, win_ref, muw_ref, mua_ref, mug_ref,
                 w1_ref, w2_ref, w0_ref, a1_ref, a2_ref, a0_ref, g1_ref, g2_ref,
                 pssm_ref, prkv_ref, pconv_ref, pfft_ref, logw_ref, agate_ref, g_ref):
    i = pl.program_id(0)
    h = h_ref[...]
    prev_row, next_row = _halo_rows(hp_ref, hn_ref, i * tile, tile, seq)
    x_prev, x_next = _shift_rows(h, prev_row, next_row)

    h16 = h.astype(BF16)

    def main():
        col = 0
        for ref in (pssm_ref, prkv_ref, pconv_ref, pfft_ref):
            width = ref.shape[1]
            ref[...] = jnp.dot(h16, win_ref[:, col:col + width], preferred_element_type=F32).astype(ref.dtype)
            col += width
            yield

    def lora(d, x_sh):
        dx = x_sh - h
        t1 = _bdot(h + dx * muw_ref[d:d + 1, :], w1_ref[d])
        yield
        w_lora = _bdot(jnp.tanh(t1), w2_ref[d])
        t2 = _bdot(h + dx * mua_ref[d:d + 1, :], a1_ref[d])
        yield
        logw_ref[d] = -RWKV_DECAY_SCALE * jax.nn.sigmoid(w0_ref[d:d + 1, :] + w_lora)
        a_lora = _bdot(t2, a2_ref[d])
        yield
        agate_ref[d] = jax.nn.sigmoid(a0_ref[d:d + 1, :] + a_lora).astype(agate_ref.dtype)

    def gate():
        xg = h + (0.5 * (x_prev + x_next) - h) * mug_ref[...]
        t3 = _bdot(xg, g1_ref[...])
        yield
        g_ref[...] = _bdot(jax.nn.sigmoid(t3), g2_ref[...]).astype(g_ref.dtype)

    _run_lockstep([main(), lora(0, x_prev), lora(1, x_next), gate()])


def _inproj(h, seq, tile, w_in, mu_w, mu_a, mu_g, w1, w2, w0, a1, a2, a0, g1, g2):
    n, d = h.shape
    hp_spec, hn_spec = _halo_specs(tile, d, n, SUBLANES)

    def full(a):
        nd = a.ndim
        return pl.BlockSpec(a.shape, lambda i: (0,) * nd)

    row = lambda w: pl.BlockSpec((tile, w), lambda i: (i, 0))
    row2 = lambda w: pl.BlockSpec((2, tile, w), lambda i: (0, i, 0))
    consts = (w_in.astype(BF16), mu_w, mu_a, mu_g.reshape(1, d), w1.astype(BF16), w2.astype(BF16), w0,
              a1.astype(BF16), a2.astype(BF16), a0, g1.astype(BF16), g2.astype(BF16))
    return pl.pallas_call(
        functools.partial(_inproj_body, seq, tile),
        grid=(n // tile,),
        in_specs=[pl.BlockSpec((tile, d), lambda i: (i, 0)), hp_spec, hn_spec] + [full(c) for c in consts],
        out_specs=[row(W_SSM), row(3 * W_RWKV), row(3 * W_CONV), row(W_FFT),
                   row2(W_RWKV), row2(W_RWKV), row(W_RWKV)],
        out_shape=[jax.ShapeDtypeStruct((n, W_SSM), BF16),
                   jax.ShapeDtypeStruct((n, 3 * W_RWKV), BF16),
                   jax.ShapeDtypeStruct((n, 3 * W_CONV), BF16),
                   jax.ShapeDtypeStruct((n, W_FFT), BF16),
                   jax.ShapeDtypeStruct((2, n, W_RWKV), F32),
                   jax.ShapeDtypeStruct((2, n, W_RWKV), BF16),
                   jax.ShapeDtypeStruct((n, W_RWKV), BF16)],
        compiler_params=_cparams(("parallel",)),
        name="inproj",
    )(h, h, h, *consts)


S5_SEGS = SUBLANES


def _cmul(ar, ai, br, bi):
    return ar * br - ai * bi, ar * bi + ai * br


class _S5Dir:
    def __init__(self, fwd, tl, ucur_ref, unext_ref, y_ref, bua_ref, bub_ref, st_ref, perm_ref, permt_ref, bb_ref,
                 a_ref, c_ref):
        self.fwd, self.tl = fwd, tl
        self.ucur_ref, self.unext_ref, self.y_ref = ucur_ref, unext_ref, y_ref
        self.bua_ref, self.bub_ref, self.st_ref = bua_ref, bub_ref, st_ref
        self.perm_ref, self.permt_ref, self.bb_ref, self.c_ref = perm_ref, permt_ref, bb_ref, c_ref
        self.half = SSM_GROUPS * SSM_STATE
        self.rows = tl * S5_SEGS
        self.n_parts = S5_SEGS
        self.part = 2 * self.half // self.n_parts
        shape = (S5_SEGS, self.half)
        self.ar = jnp.broadcast_to(a_ref[:, :self.half], shape)
        self.ai = jnp.broadcast_to(a_ref[:, self.half:], shape)
        self.off_a, self.off_b = (0, tl) if fwd else (tl, 0)

    def project(self, u_ref, off, buf_ref):
        u = u_ref[0, :, off:off + self.tl, :].reshape(self.rows, W_SSM).astype(BF16)
        up = jnp.dot(self.perm_ref[...], u, preferred_element_type=F32).astype(BF16)
        for k in range(self.n_parts):
            cols = slice(k * self.part, (k + 1) * self.part)
            buf_ref[:, cols] = jnp.dot(up, self.bb_ref[:, cols], preferred_element_type=F32)
            yield

    def scan(self, buf_ref, store):
        half, tl = self.half, self.tl
        xr, xi = self.st_ref[:, :half], self.st_ref[:, half:]
        for t in range(tl):
            row = (t if self.fwd else tl - 1 - t) * S5_SEGS
            nr = self.ar * xr - self.ai * xi + buf_ref[row:row + S5_SEGS, :half]
            ni = self.ar * xi + self.ai * xr + buf_ref[row:row + S5_SEGS, half:]
            if store:
                buf_ref[row:row + S5_SEGS, :half] = nr
                buf_ref[row:row + S5_SEGS, half:] = ni
            xr, xi = nr, ni
            if (t + 1) % (tl // self.n_parts) == 0:
                yield
        self.st_ref[:, :half] = xr
        self.st_ref[:, half:] = xi

    def emit(self, buf_ref, off):
        y = None
        for k in range(self.n_parts):
            cols = slice(k * self.part, (k + 1) * self.part)
            yk = _bdot(buf_ref[:, cols], self.c_ref[cols, :])
            y = yk if y is None else y + yk
            yield
        y = jnp.dot(self.permt_ref[...], y.astype(BF16), preferred_element_type=F32)
        self.y_ref[0, :, off:off + self.tl, :] = y.reshape(S5_SEGS, self.tl, W_SSM).astype(self.y_ref.dtype)

    def init_state(self, seg_len):
        half = self.half
        pr, pi = jnp.ones_like(self.ar), jnp.zeros_like(self.ar)
        br, bi = self.ar, self.ai
        e = seg_len
        while e:
            if e & 1:
                pr, pi = _cmul(pr, pi, br, bi)
            br, bi = _cmul(br, bi, br, bi)
            e >>= 1
        er, ei = self.st_ref[:, :half], self.st_ref[:, half:]
        zero = jnp.zeros((1, half), F32)
        order = range(S5_SEGS) if self.fwd else range(S5_SEGS - 1, -1, -1)
        cr, ci, out_r, out_i = zero, zero, {}, {}
        for j in order:
            out_r[j], out_i[j] = cr, ci
            nr, ni = _cmul(pr[0:1], pi[0:1], cr, ci)
            cr, ci = nr + er[j:j + 1], ni + ei[j:j + 1]
        self.st_ref[:, :half] = jnp.concatenate([out_r[j] for j in range(S5_SEGS)], axis=0)
        self.st_ref[:, half:] = jnp.concatenate([out_i[j] for j in range(S5_SEGS)], axis=0)


def _s5_body(tl, seg_len, ucf_ref, unf_ref, ucb_ref, unb_ref, perm_ref, permt_ref, bb_ref, a_ref, c_ref, yf_ref,
             yb_ref, baf_ref, bbf_ref, bab_ref, bbb_ref, st_ref):
    ps = pl.program_id(1)
    i = pl.program_id(2)
    dirs = [_S5Dir(True, tl, ucf_ref, unf_ref, yf_ref, baf_ref, bbf_ref, st_ref.at[0], perm_ref, permt_ref,
                   bb_ref.at[0], a_ref.at[0], c_ref.at[0]),
            _S5Dir(False, tl, ucb_ref, unb_ref, yb_ref, bab_ref, bbb_ref, st_ref.at[1], perm_ref, permt_ref,
                   bb_ref.at[1], a_ref.at[1], c_ref.at[1])]

    @pl.when(jnp.logical_and(i == 0, ps == 0))
    def _():
        st_ref[...] = jnp.zeros_like(st_ref)

    @pl.when(jnp.logical_and(i == 0, ps == 1))
    def _():
        for z in dirs:
            z.init_state(seg_len)

    @pl.when(i == 0)
    def _():
        _run_lockstep([z.project(z.ucur_ref, z.off_a, z.bua_ref) for z in dirs])

    @pl.when(ps == 0)
    def _():
        _run_lockstep([g for z in dirs for g in (z.scan(z.bua_ref, False), z.project(z.ucur_ref, z.off_b, z.bub_ref))])
        _run_lockstep([g for z in dirs for g in (z.scan(z.bub_ref, False), z.project(z.unext_ref, z.off_a, z.bua_ref))])

    @pl.when(ps == 1)
    def _():
        _run_lockstep([g for z in dirs for g in (z.scan(z.bua_ref, True), z.project(z.ucur_ref, z.off_b, z.bub_ref))])
        _run_lockstep([g for z in dirs for g in (z.scan(z.bub_ref, True), z.emit(z.bua_ref, z.off_a),
                                                 z.project(z.unext_ref, z.off_a, z.bua_ref))])
        _run_lockstep([z.emit(z.bub_ref, z.off_b) for z in dirs])


def _s5_params(lam_re, lam_im, log_dt, b_re, b_im, c_re, c_im):
    g, p, hch = SSM_GROUPS, SSM_STATE, SSM_CH
    dt = jnp.exp(log_dt)[..., None]
    mag = jnp.exp(lam_re * dt)
    lb_re = mag * jnp.cos(lam_im * dt)
    lb_im = mag * jnp.sin(lam_im * dt)
    den = lam_re * lam_re + lam_im * lam_im
    nr = lb_re - 1.0
    coef_re = (nr * lam_re + lb_im * lam_im) / den
    coef_im = (lb_im * lam_re - nr * lam_im) / den
    bb_re = coef_re[..., None] * b_re - coef_im[..., None] * b_im
    bb_im = coef_re[..., None] * b_im + coef_im[..., None] * b_re
    eye = jnp.eye(g, dtype=F32)
    bd_in = lambda m: jnp.einsum('dgph,gk->dghkp', m, eye).reshape(2, g * hch, g * p)
    bb = jnp.concatenate([bd_in(bb_re), bd_in(bb_im)], axis=-1)
    bd_out = lambda m: jnp.einsum('dghp,gk->dgpkh', m, eye).reshape(2, g * p, g * hch)
    cc = jnp.concatenate([bd_out(c_re), -bd_out(c_im)], axis=1)
    a = jnp.concatenate([lb_re.reshape(2, 1, g * p), lb_im.reshape(2, 1, g * p)], axis=-1)
    return bb, a, cc


def _s5_scan(p_ssm, bsz, seq, bb, a, cc, tl):
    seg_len = seq // S5_SEGS
    n_pairs = seg_len // (2 * tl)
    rows = tl * S5_SEGS
    w2 = 2 * SSM_GROUPS * SSM_STATE
    src = (np.arange(rows) % S5_SEGS) * tl + np.arange(rows) // S5_SEGS
    perm = np.zeros((rows, rows), np.float32)
    perm[np.arange(rows), src] = 1.0
    last = n_pairs - 1
    u4 = p_ssm.reshape(bsz, S5_SEGS, seg_len, W_SSM)
    ublk = (1, S5_SEGS, 2 * tl, W_SSM)
    const = lambda shape: pl.BlockSpec(shape, lambda b, ps, i: (0,) * len(shape))
    y_f, y_b = pl.pallas_call(
        functools.partial(_s5_body, tl, seg_len),
        grid=(bsz, 2, n_pairs),
        in_specs=[pl.BlockSpec(ublk, lambda b, ps, i: (b, 0, i, 0)),
                  pl.BlockSpec(ublk, lambda b, ps, i: (b, 0, jnp.minimum(i + 1, last), 0)),
                  pl.BlockSpec(ublk, lambda b, ps, i: (b, 0, last - i, 0)),
                  pl.BlockSpec(ublk, lambda b, ps, i: (b, 0, jnp.maximum(last - i - 1, 0), 0)),
                  const((rows, rows)), const((rows, rows)), const((2, W_SSM, w2)), const((2, 1, w2)),
                  const((2, w2, W_SSM))],
        out_specs=[pl.BlockSpec(ublk, lambda b, ps, i: (b, 0, jnp.where(ps == 0, 0, i), 0)),
                   pl.BlockSpec(ublk, lambda b, ps, i: (b, 0, jnp.where(ps == 0, last, last - i), 0))],
        out_shape=[jax.ShapeDtypeStruct((bsz, S5_SEGS, seg_len, W_SSM), BF16)] * 2,
        scratch_shapes=[pltpu.VMEM((rows, w2), F32)] * 4 + [pltpu.VMEM((2, S5_SEGS, w2), F32)],
        compiler_params=_cparams(("arbitrary",) * 3),
        name="s5_scan",
    )(u4, u4, u4, u4, jnp.asarray(perm).astype(BF16), jnp.asarray(perm.T).astype(BF16), bb.astype(BF16), a,
      cc.astype(BF16))
    return y_f.reshape(bsz * seq, W_SSM), y_b.reshape(bsz * seq, W_SSM)


def _gelu_tanh(x):
    c = math.sqrt(2.0 / math.pi)
    return 0.5 * x * (1.0 + jnp.tanh(c * (x + 0.044715 * (x * x * x))))


def _s5_post(y_f, y_b, u, dskip, glu_w, glu_b):
    y = _gelu_tanh(y_f.astype(F32) + y_b.astype(F32) + dskip * u)
    return y * jax.nn.sigmoid(_bdot(y, glu_w) + glu_b)


def _split_dot(x, w):
    hi = x.astype(BF16)
    lo = (x - hi.astype(F32)).astype(BF16)
    return (jnp.dot(hi, w, preferred_element_type=F32) + jnp.dot(lo, w, preferred_element_type=F32))


def _dot_nt(a, b):
    return lax.dot_general(a.astype(BF16), b.astype(BF16), (((1,), (1,)), ((), ())),
                           preferred_element_type=F32)


def _rwkv_masks(chunk):
    hh = RWKV_HEADS
    r = np.arange(hh * chunk)[:, None] // chunk
    bm_feat = (r == np.arange(W_RWKV)[None, :] // RWKV_HEAD).astype(np.float32)
    bm_time = (r == np.arange(hh * chunk)[None, :] // chunk).astype(np.float32)
    f = np.arange(W_RWKV)
    bm_head = (f[:, None] // RWKV_HEAD == f[None, :] // RWKV_HEAD).astype(np.float32)
    return bm_feat, bm_time, bm_head


def _rwkv_chunk(fwd, x, edge_row, logw, a, mu, k_k, k_a, r_k, bmf16, bmt16, bmh):
    ll = x.shape[0]
    w = W_RWKV
    rows = lax.broadcasted_iota(jnp.int32, x.shape, 0)
    if fwd:
        shifted = jnp.where(rows == 0, edge_row, pltpu.roll(x, 1, axis=0))
    else:
        shifted = jnp.where(rows == ll - 1, edge_row, pltpu.roll(x, ll - 1, axis=0))
    rkv = x + (shifted - x) * mu
    r, k, v = rkv[:, :w], rkv[:, w:2 * w], rkv[:, 2 * w:]
    bmh16 = bmh.astype(BF16)

    kk = k * k_k
    ksq = _split_dot(kk * kk, bmh16)
    k2 = k * (1.0 + (a - 1.0) * k_a)
    bonus = _split_dot(r * k2 * r_k, bmh16) * v

    ti = lax.broadcasted_iota(jnp.int32, (ll, ll), 0)
    si = lax.broadcasted_iota(jnp.int32, (ll, ll), 1)
    tri = ((si <= ti) if fwd else (si >= ti)).astype(F32).astype(BF16)
    cum = _split_dot_left(tri, logw)
    yield
    kk = kk * lax.rsqrt(ksq + 1e-12)
    ctot = jnp.sum(logw, axis=0, keepdims=True)
    e_neg = jnp.exp(-cum)
    e_rem = jnp.exp(ctot - cum)
    ah = -kk * jnp.exp(cum - logw)
    rh = r * jnp.exp(cum)
    bvec = kk * a
    bh, kh = bvec * e_neg, k2 * e_neg
    bt, kt = bvec * e_rem, k2 * e_rem

    def bd(m):
        m16 = m.astype(BF16)
        return jnp.concatenate([m16] * RWKV_HEADS, axis=0) * (bmf16 if m.shape[1] == w else bmt16)

    gram = _dot_nt(jnp.concatenate([ah, rh], axis=0), jnp.concatenate([bd(bh), bd(kh)], axis=0))
    yield
    l4 = RWKV_HEADS * ll
    tt = lax.broadcasted_iota(jnp.int32, (ll, l4), 0)
    ss = lax.broadcasted_iota(jnp.int32, (ll, l4), 1) % ll
    strict = (ss < tt) if fwd else (ss > tt)
    incl = (ss <= tt) if fwd else (ss >= tt)
    n_ab = jnp.where(strict, gram[:ll, :l4], 0.0)
    a_ak = jnp.where(strict, gram[:ll, l4:], 0.0)
    a_rb = jnp.where(incl, gram[ll:, :l4], 0.0)
    a_rk = jnp.where(incl, gram[ll:, l4:], 0.0)

    pw = n_ab
    tinv = jnp.where(ss == tt, 1.0, 0.0) + n_ab
    akv = _bdot(a_ak, bd(v))
    for _ in range(int(math.log2(ll)) - 1):
        pw = _bdot(pw, bd(pw))
        yield
        tinv = tinv + _bdot(tinv, bd(pw))
        yield

    ta = _bdot(tinv, jnp.concatenate([bd(ah), bd(akv)], axis=1))
    yield
    ap, wm = ta[:, :w], ta[:, w:]
    rp = rh + _bdot(a_rb, bd(ap))
    y0 = _bdot(jnp.concatenate([a_rb, a_rk], axis=1), jnp.concatenate([bd(wm), bd(v)], axis=0))
    btk = jnp.transpose(jnp.concatenate([bt, kt], axis=0))
    rhs = jnp.concatenate([jnp.concatenate([ap, wm], axis=1),
                           jnp.concatenate([jnp.zeros_like(v), v], axis=1)], axis=0)
    pq = _bdot(btk, rhs)
    yield
    eye = (lax.broadcasted_iota(jnp.int32, (w, w), 0) == lax.broadcasted_iota(jnp.int32, (w, w), 1))
    pm = pq[:, :w] * bmh + jnp.where(eye, jnp.exp(ctot), 0.0)
    qm = pq[:, w:] * bmh
    yield rp, y0, pm, qm, bonus


def _run_lockstep(gens):
    results = [None] * len(gens)
    live = list(range(len(gens)))
    while live:
        for s in list(live):
            try:
                out = next(gens[s])
            except StopIteration:
                live.remove(s)
            else:
                if out is not None:
                    results[s] = out
    return results


def _rwkv_body(chunk, nsub, nb, *refs):
    dir_refs = [refs[0:4], refs[4:8]]
    mu_ref, kk_ref, ka_ref, rk_ref, bmf_ref, bmt_ref, bmh_ref = refs[8:15]
    y_refs, bonus_refs, h_ref = refs[15:17], refs[17:19], refs[19]
    c = pl.program_id(0)

    @pl.when(c == 0)
    def _():
        h_ref[...] = jnp.zeros_like(h_ref)

    gens, where = [], []
    for d in range(2):
        x_ref, edge_ref, logw_ref, a_ref = dir_refs[d]
        order = range(nsub) if d == 0 else range(nsub - 1, -1, -1)
        for b in range(nb):
            x = x_ref[b].astype(F32)
            for n, j in enumerate(order):
                rows = slice(j * chunk, (j + 1) * chunk)
                if n == 0:
                    hb = edge_ref.shape[1]
                    edge = edge_ref[b, hb - 1:hb, :] if d == 0 else edge_ref[b, 0:1, :]
                    edge = jnp.where(c == 0, 0.0, edge.astype(F32))
                else:
                    e = j * chunk - 1 if d == 0 else (j + 1) * chunk
                    edge = x[e:e + 1, :]
                gens.append(_rwkv_chunk(d == 0, x[rows], edge, logw_ref[0, b, rows, :],
                                        a_ref[0, b, rows, :].astype(F32), mu_ref[d], kk_ref[...], ka_ref[...],
                                        rk_ref[...], bmf_ref[...], bmt_ref[...], bmh_ref[...]))
                where.append((d, b, rows))
    parts = _run_lockstep(gens)

    def carry(s):
        h = h_ref[s]
        for n in range(nsub):
            rp, y0, pm, qm, bonus = parts[s * nsub + n]
            d, b, rows = where[s * nsub + n]
            y_refs[d][b, rows, :] = (_bdot(rp, h) + y0).astype(y_refs[d].dtype)
            bonus_refs[d][b, rows, :] = bonus.astype(bonus_refs[d].dtype)
            h = _bdot(pm, h) + qm
            yield
        h_ref[s] = h

    _run_lockstep([carry(s) for s in range(2 * nb)])


def _split_dot_left(w, x):
    hi = x.astype(BF16)
    lo = (x - hi.astype(F32)).astype(BF16)
    return (jnp.dot(w, hi, preferred_element_type=F32) + jnp.dot(w, lo, preferred_element_type=F32))


def _rwkv_scan(p_rkv, logw, agate, bsz, seq, chunk, nsub, mu_rkv, k_k, k_a, r_k):
    n = bsz * seq
    blk = chunk * nsub
    n_chunks = seq // blk
    hb = PACKED_ROWS
    per = blk // hb
    w3 = 3 * W_RWKV
    bmf, bmt, bmh = _rwkv_masks(chunk)
    consts = (mu_rkv.reshape(2, 1, w3), k_k.reshape(1, -1), k_a.reshape(1, -1), r_k.reshape(1, -1),
              jnp.asarray(bmf).astype(BF16), jnp.asarray(bmt).astype(BF16), jnp.asarray(bmh))
    x3 = p_rkv.reshape(bsz, seq, w3)
    logw4 = logw.reshape(2, bsz, seq, W_RWKV)
    a4 = agate.reshape(2, bsz, seq, W_RWKV)
    args, in_specs, out_specs = [], [], []
    for d in range(2):
        cidx = (lambda c: c) if d == 0 else (lambda c: n_chunks - 1 - c)
        if d == 0:
            edge = lambda c: (0, jnp.maximum(c * per - 1, 0), 0)
        else:
            edge = lambda c: (0, jnp.minimum((n_chunks - c) * per, seq // hb - 1), 0)
        args += [x3, x3, logw4, a4]
        in_specs += [pl.BlockSpec((bsz, blk, w3), lambda c, cidx=cidx: (0, cidx(c), 0)),
                     pl.BlockSpec((bsz, hb, w3), edge),
                     pl.BlockSpec((1, bsz, blk, W_RWKV), lambda c, d=d, cidx=cidx: (d, 0, cidx(c), 0)),
                     pl.BlockSpec((1, bsz, blk, W_RWKV), lambda c, d=d, cidx=cidx: (d, 0, cidx(c), 0))]
        out_specs.append(pl.BlockSpec((bsz, blk, W_RWKV), lambda c, cidx=cidx: (0, cidx(c), 0)))

    def full(a):
        nd = a.ndim
        return pl.BlockSpec(a.shape, lambda c: (0,) * nd)

    y_f, y_b, bonus_f, bonus_b = pl.pallas_call(
        functools.partial(_rwkv_body, chunk, nsub, bsz),
        grid=(n_chunks,),
        in_specs=in_specs + [full(a) for a in consts],
        out_specs=out_specs + out_specs,
        out_shape=[jax.ShapeDtypeStruct((bsz, seq, W_RWKV), BF16)] * 4,
        scratch_shapes=[pltpu.VMEM((2 * bsz, W_RWKV, W_RWKV), F32)],
        compiler_params=_cparams(("arbitrary",)),
        name="rwkv_scan",
    )(*args, *consts)
    return [a.reshape(n, W_RWKV) for a in (y_f, y_b, bonus_f, bonus_b)]


def _head_mean(x, bmh16):
    return _split_dot(x, bmh16) * (1.0 / RWKV_HEAD)


def _rwkv_post(y, bonus, g, gn_g, gn_b, bmh16):
    mu = _head_mean(y, bmh16)
    yc = y - mu
    var = _head_mean(yc * yc, bmh16)
    yn = yc * lax.rsqrt(var + RWKV_GN_EPS) * gn_g + gn_b
    return (yn + bonus) * g


FFT_R1 = 64


def _fft_tables(seq):
    r1, r2 = FFT_R1, seq // FFT_R1
    i1 = jnp.arange(r1, dtype=jnp.int32)
    ang1 = ((i1[:, None] * i1[None, :]) % r1).astype(F32) * (2.0 * math.pi / r1)
    stage1 = jnp.concatenate([jnp.cos(ang1), -jnp.sin(ang1)], axis=0)
    k1 = i1[:, None, None]
    k2 = jnp.arange(r2, dtype=jnp.int32)[None, :, None]
    n2 = jnp.arange(r2, dtype=jnp.int32)[None, None, :]
    ang2 = ((n2 * (k1 + r1 * k2)) % seq).astype(F32) * (2.0 * math.pi / seq)
    mr, mi = jnp.cos(ang2), -jnp.sin(ang2)
    stage2 = jnp.concatenate([jnp.concatenate([mr, -mi], axis=2),
                              jnp.concatenate([mi, mr], axis=2)], axis=1)
    c = jnp.arange(W_FFT, dtype=jnp.int32)
    same = (c[:, None] // FFT_CH) == (c[None, :] // FFT_CH)
    angc = ((c[:, None] * c[None, :]) % FFT_CH).astype(F32) * (2.0 * math.pi / FFT_CH)
    scale = 1.0 / math.sqrt(seq * FFT_CH)
    chan = jnp.concatenate([jnp.where(same, jnp.cos(angc), 0.0), jnp.where(same, jnp.sin(angc), 0.0)],
                           axis=0) * scale
    return stage1, stage2, chan


def _fft1_body(z_ref, m_ref, are_ref, aim_ref):
    r1 = FFT_R1
    a = _bdot(m_ref[...], z_ref[0])
    are_ref[0] = a[:r1].astype(are_ref.dtype)
    aim_ref[0] = a[r1:].astype(aim_ref.dtype)


def _fft2_body(kb, r2, are_ref, aim_ref, m_ref, chan_ref, o_ref):
    for j in range(kb):
        x = jnp.concatenate([are_ref[0, j], aim_ref[0, j]], axis=0)
        f = _bdot(m_ref[j], x)
        fri = jnp.concatenate([f[:r2], f[r2:]], axis=1)
        o_ref[0, :, j * W_FFT:(j + 1) * W_FFT] = _bdot(fri, chan_ref[...]).astype(o_ref.dtype)


def _fourier(p_fft, bsz, seq, tables=None):
    r1, r2 = FFT_R1, seq // FFT_R1
    stage1, stage2, chan = tables if tables is not None else _fft_tables(seq)
    cols = r2 * W_FFT
    tc = min(cols, 4096)
    z = p_fft.reshape(bsz, r1, cols)
    blk = pl.BlockSpec((1, r1, tc), lambda b, j: (b, 0, j))
    a_re, a_im = pl.pallas_call(
        _fft1_body,
        grid=(bsz, cols // tc),
        in_specs=[blk, pl.BlockSpec((2 * r1, r1), lambda b, j: (0, 0))],
        out_specs=[blk, blk],
        out_shape=[jax.ShapeDtypeStruct((bsz, r1, cols), BF16)] * 2,
        compiler_params=_cparams(("parallel", "parallel")),
        name="fft_stage1",
    )(z, stage1.astype(BF16))
    kb = 8
    ablk = pl.BlockSpec((1, kb, r2, W_FFT), lambda b, j: (b, j, 0, 0))
    out = pl.pallas_call(
        functools.partial(_fft2_body, kb, r2),
        grid=(bsz, r1 // kb),
        in_specs=[ablk, ablk,
                  pl.BlockSpec((kb, 2 * r2, 2 * r2), lambda b, j: (j, 0, 0)),
                  pl.BlockSpec((2 * W_FFT, W_FFT), lambda b, j: (0, 0))],
        out_specs=pl.BlockSpec((1, r2, kb * W_FFT), lambda b, j: (b, 0, j)),
        out_shape=jax.ShapeDtypeStruct((bsz, r2, r1 * W_FFT), BF16),
        compiler_params=_cparams(("parallel", "parallel")),
        name="fft_stage2",
    )(a_re.reshape(bsz, r1, r2, W_FFT), a_im.reshape(bsz, r1, r2, W_FFT), stage2.astype(BF16), chan.astype(BF16))
    return out.reshape(bsz * seq, W_FFT)


def _mixout_body(seq, tile, h_ref, ysf_ref, ysb_ref, pssm_ref, yrf_ref, yrb_ref, bnf_ref, bnb_ref, g_ref, pc_ref,
                 pcp_ref, pcn_ref, yf_ref, wout_ref, dskip_ref, gluw_ref, glub_ref, gng_ref, gnb_ref, convw_ref,
                 bmh_ref, lng_ref, lnb_ref, o_ref):
    i = pl.program_id(0)
    f32 = lambda ref: ref[...].astype(F32)
    y_a = _s5_post(f32(ysf_ref), f32(ysb_ref), f32(pssm_ref), dskip_ref[...], gluw_ref[...], glub_ref[...])
    y_b = _rwkv_post(f32(yrf_ref) + f32(yrb_ref), f32(bnf_ref) + f32(bnb_ref), f32(g_ref), gng_ref[...],
                     gnb_ref[...], bmh_ref[...])
    pc = f32(pc_ref)
    wc = W_CONV
    prev_row, next_row = _halo_rows(pcp_ref, pcn_ref, i * tile, tile, seq)
    z = pc[:, wc:2 * wc] * pc[:, 2 * wc:]
    z_prev, z_next = _shift_rows(z, prev_row[:, wc:2 * wc] * prev_row[:, 2 * wc:],
                                 next_row[:, wc:2 * wc] * next_row[:, 2 * wc:])
    y_c = pc[:, :wc] * (convw_ref[0:1, :] * z_prev + convw_ref[1:2, :] * z + convw_ref[2:3, :] * z_next)
    mix = (_bdot(y_a, wout_ref[0:W_SSM, :]) + _bdot(y_b, wout_ref[W_SSM:W_SSM + W_RWKV, :])
           + _bdot(y_c, wout_ref[W_SSM + W_RWKV:W_SSM + W_RWKV + W_CONV, :])
           + _bdot(yf_ref[...], wout_ref[W_SSM + W_RWKV + W_CONV:, :]))
    o_ref[...] = _layer_norm(DEEPNORM_ALPHA * h_ref[...] + mix, lng_ref[...], lnb_ref[...])


def _mixout(h, seq, tile, y_s5, p_ssm, rwkv_outs, g, p_conv, y_fft, w_out, dskip, glu_w, glu_b, gn_g, gn_b,
            conv_w, ln_g, ln_b):
    n, d = h.shape
    row = lambda w: pl.BlockSpec((tile, w), lambda i: (i, 0))
    pcp, pcn = _halo_specs(tile, 3 * W_CONV, n, PACKED_ROWS)
    bmh16 = jnp.asarray(_rwkv_masks(RWKV_HEAD)[2]).astype(BF16)
    consts = (w_out.astype(BF16), dskip.reshape(1, -1), glu_w.astype(BF16), glu_b.reshape(1, -1),
              gn_g.reshape(1, -1), gn_b.reshape(1, -1), conv_w, bmh16, ln_g.reshape(1, -1), ln_b.reshape(1, -1))

    def full(a):
        nd = a.ndim
        return pl.BlockSpec(a.shape, lambda i: (0,) * nd)

    return pl.pallas_call(
        functools.partial(_mixout_body, seq, tile),
        grid=(n // tile,),
        in_specs=[row(d)] + [row(W_SSM)] * 3 + [row(W_RWKV)] * 5
                 + [row(3 * W_CONV), pcp, pcn, row(W_FFT)] + [full(c) for c in consts],
        out_specs=row(d),
        out_shape=jax.ShapeDtypeStruct((n, d), F32),
        compiler_params=_cparams(("parallel",)),
        name="mix_out",
    )(h, *y_s5, p_ssm, *rwkv_outs, g, p_conv, p_conv, p_conv, y_fft, *consts)


def _ffn_body(n_ff, h_ref, w1_ref, w3_ref, w2_ref, lng_ref, lnb_ref, o_ref, acc_ref):
    j = pl.program_id(1)

    @pl.when(j == 0)
    def _():
        acc_ref[...] = jnp.zeros_like(acc_ref)

    x = h_ref[...].astype(BF16)
    u = jax.nn.silu(jnp.dot(x, w1_ref[...], preferred_element_type=F32)) * jnp.dot(
        x, w3_ref[...], preferred_element_type=F32)
    acc_ref[...] += _bdot(u, w2_ref[...])

    @pl.when(j == n_ff - 1)
    def _():
        o_ref[...] = _layer_norm(DEEPNORM_ALPHA * h_ref[...] + acc_ref[...], lng_ref[...], lnb_ref[...])


def _ffn(h, tile, tf, w1, w3, w2, ln_g, ln_b):
    n, d = h.shape
    dff = w1.shape[1]
    n_ff = dff // tf
    return pl.pallas_call(
        functools.partial(_ffn_body, n_ff),
        grid=(n // tile, n_ff),
        in_specs=[pl.BlockSpec((tile, d), lambda i, j: (i, 0)),
                  pl.BlockSpec((d, tf), lambda i, j: (0, j)),
                  pl.BlockSpec((d, tf), lambda i, j: (0, j)),
                  pl.BlockSpec((tf, d), lambda i, j: (j, 0)),
                  pl.BlockSpec((1, d), lambda i, j: (0, 0)),
                  pl.BlockSpec((1, d), lambda i, j: (0, 0))],
        out_specs=pl.BlockSpec((tile, d), lambda i, j: (i, 0)),
        out_shape=jax.ShapeDtypeStruct((n, d), F32),
        scratch_shapes=[pltpu.VMEM((tile, d), F32)],
        compiler_params=_cparams(("parallel", "arbitrary")),
        name="ffn",
    )(h, w1.astype(BF16), w3.astype(BF16), w2.astype(BF16), ln_g.reshape(1, d), ln_b.reshape(1, d))


ROW_TILE = 512
S5_TILE_STEPS = 64
RWKV_CHUNK = 64
RWKV_CHUNKS_PER_STEP = 2
FFN_COLS = 1408


def kernel(x, ln0_g, ln0_b, w_in, s5_lambda_re, s5_lambda_im, s5_log_dt, s5_b_re, s5_b_im, s5_c_re, s5_c_im, s5_d,
           s5_glu_w, s5_glu_b, rwkv_mu_rkv, rwkv_mu_w, rwkv_mu_a, rwkv_mu_g, rwkv_w0, rwkv_w1, rwkv_w2, rwkv_a0,
           rwkv_a1, rwkv_a2, rwkv_g1, rwkv_g2, rwkv_k_k, rwkv_k_a, rwkv_r_k, rwkv_gn_g, rwkv_gn_b, conv_w, w_out,
           ln1_g, ln1_b, ffn_w1, ffn_w3, ffn_w2, ln2_g, ln2_b):
    bsz, seq, d = x.shape
    n = bsz * seq
    tile = min(ROW_TILE, seq)
    s5_tl = min(S5_TILE_STEPS, seq // S5_SEGS // 2)
    fft_tables = _fft_tables(seq)
    h = _ln0(x.reshape(n, d), ln0_g, ln0_b, tile)
    for l in range(w_in.shape[0]):
        p_ssm, p_rkv, p_conv, p_fft, logw, agate, g = _inproj(
            h, seq, tile, w_in[l], rwkv_mu_w[l], rwkv_mu_a[l], rwkv_mu_g[l], rwkv_w1[l], rwkv_w2[l], rwkv_w0[l],
            rwkv_a1[l], rwkv_a2[l], rwkv_a0[l], rwkv_g1[l], rwkv_g2[l])
        bb, a, cc = _s5_params(s5_lambda_re[l], s5_lambda_im[l], s5_log_dt[l], s5_b_re[l], s5_b_im[l],
                               s5_c_re[l], s5_c_im[l])
        y_s5 = _s5_scan(p_ssm, bsz, seq, bb, a, cc, s5_tl)
        rwkv_outs = _rwkv_scan(p_rkv, logw, agate, bsz, seq, RWKV_CHUNK, RWKV_CHUNKS_PER_STEP, rwkv_mu_rkv[l], rwkv_k_k[l],
                               rwkv_k_a[l], rwkv_r_k[l].reshape(-1))
        y_fft = _fourier(p_fft, bsz, seq, fft_tables)
        h = _mixout(h, seq, tile, y_s5, p_ssm, rwkv_outs, g, p_conv, y_fft, w_out[l], s5_d[l], s5_glu_w[l],
                    s5_glu_b[l], rwkv_gn_g[l], rwkv_gn_b[l], conv_w[l], ln1_g[l], ln1_b[l])
        h = _ffn(h, tile, FFN_COLS, ffn_w1[l], ffn_w3[l], ffn_w2[l], ln2_g[l], ln2_b[l])
    return h.reshape(bsz, seq, d)
```

```python
import functools
import math

import jax
import jax.numpy as jnp
import numpy as np
from jax import lax
from jax.experimental import pallas as pl
from jax.experimental.pallas import tpu as pltpu

W_SSM = 256
W_RWKV = 256
W_CONV = 256
W_FFT = 256
SSM_CH = 16
SSM_GROUPS = 16
SSM_STATE = 64
RWKV_HEAD = 64
RWKV_HEADS = 4
FFT_GROUPS = 4
FFT_CH = 64
RWKV_DECAY_SCALE = math.exp(-0.5)
RWKV_GN_EPS = 64e-5
LN_EPS = 1e-5
DEPTH = 2
DEEPNORM_ALPHA = (2 * DEPTH) ** 0.25

SUBLANES = 8
PACKED_ROWS = 16
VMEM_LIMIT = 48 * 1024 * 1024

BF16 = jnp.bfloat16
F32 = jnp.float32


def _cparams(sem):
    return pltpu.CompilerParams(dimension_semantics=sem, vmem_limit_bytes=VMEM_LIMIT)


def _bdot(a, b):
    return jnp.dot(a.astype(BF16), b.astype(BF16), preferred_element_type=F32)


def _layer_norm(x, g, b):
    mu = jnp.mean(x, axis=-1, keepdims=True)
    xc = x - mu
    var = jnp.mean(xc * xc, axis=-1, keepdims=True)
    return xc * lax.rsqrt(var + LN_EPS) * g + b


def _shift_rows(x, prev_row, next_row):
    n = x.shape[0]
    rows = lax.broadcasted_iota(jnp.int32, x.shape, 0)
    x_prev = jnp.where(rows == 0, prev_row, pltpu.roll(x, 1, axis=0))
    x_next = jnp.where(rows == n - 1, next_row, pltpu.roll(x, n - 1, axis=0))
    return x_prev, x_next


def _halo_rows(prev_ref, next_ref, row0, n_rows, seq):
    first = (row0 % seq) == 0
    last = ((row0 + n_rows) % seq) == 0
    hb = prev_ref.shape[0]
    prev_row = jnp.where(first, 0.0, prev_ref[hb - 1:hb, :].astype(F32))
    next_row = jnp.where(last, 0.0, next_ref[0:1, :].astype(F32))
    return prev_row, next_row


def _halo_specs(tile, width, n_rows_total, hb):
    per = tile // hb
    last_blk = n_rows_total // hb - 1
    prev = pl.BlockSpec((hb, width), lambda i: (jnp.maximum(i * per - 1, 0), 0))
    nxt = pl.BlockSpec((hb, width), lambda i: (jnp.minimum((i + 1) * per, last_blk), 0))
    return prev, nxt


def _ln0_body(x_ref, g_ref, b_ref, o_ref):
    o_ref[...] = _layer_norm(x_ref[...], g_ref[...], b_ref[...])


def _ln0(x2, g, b, tile):
    n, d = x2.shape
    return pl.pallas_call(
        _ln0_body,
        grid=(n // tile,),
        in_specs=[pl.BlockSpec((tile, d), lambda i: (i, 0)),
                  pl.BlockSpec((1, d), lambda i: (0, 0)),
                  pl.BlockSpec((1, d), lambda i: (0, 0))],
        out_specs=pl.BlockSpec((tile, d), lambda i: (i, 0)),
        out_shape=jax.ShapeDtypeStruct((n, d), F32),
        compiler_params=_cparams(("parallel",)),
        name="ln0",
    )(x2, g.reshape(1, d), b.reshape(1, d))


def _inproj_body(seq, tile, h_ref, hp_ref, hn_ref, win_ref, muw_ref, mua_ref, mug_ref,
                 w1_ref, w2_ref, w0_ref, a1_ref, a2_ref, a0_ref, g1_ref, g2_ref,
                 pssm_ref, prkv_ref, pconv_ref, pfft_ref, logw_ref, agate_ref, g_ref):
    i = pl.program_id(0)
    h = h_ref[...]
    prev_row, next_row = _halo_rows(hp_ref, hn_ref, i * tile, tile, seq)
    x_prev, x_next = _shift_rows(h, prev_row, next_row)

    h16 = h.astype(BF16)

    def main():
        col = 0
        for ref in (pssm_ref, prkv_ref, pconv_ref, pfft_ref):
            width = ref.shape[1]
            ref[...] = jnp.dot(h16, win_ref[:, col:col + width], preferred_element_type=F32).astype(ref.dtype)
            col += width
            yield

    def lora(d, x_sh):
        dx = x_sh - h
        t1 = _bdot(h + dx * muw_ref[d:d + 1, :], w1_ref[d])
        yield
        w_lora = _bdot(jnp.tanh(t1), w2_ref[d])
        t2 = _bdot(h + dx * mua_ref[d:d + 1, :], a1_ref[d])
        yield
        logw_ref[d] = -RWKV_DECAY_SCALE * jax.nn.sigmoid(w0_ref[d:d + 1, :] + w_lora)
        a_lora = _bdot(t2, a2_ref[d])
        yield
        agate_ref[d] = jax.nn.sigmoid(a0_ref[d:d + 1, :] + a_lora).astype(agate_ref.dtype)

    def gate():
        xg = h + (0.5 * (x_prev + x_next) - h) * mug_ref[...]
        t3 = _bdot(xg, g1_ref[...])
        yield
        g_ref[...] = _bdot(jax.nn.sigmoid(t3), g2_ref[...]).astype(g_ref.dtype)

    _run_lockstep([main(), lora(0, x_prev), lora(1, x_next), gate()])


def _inproj(h, seq, tile, w_in, mu_w, mu_a, mu_g, w1, w2, w0, a1, a2, a0, g1, g2):
    n, d = h.shape
    hp_spec, hn_spec = _halo_specs(tile, d, n, SUBLANES)

    def full(a):
        nd = a.ndim
        return pl.BlockSpec(a.shape, lambda i: (0,) * nd)

    row = lambda w: pl.BlockSpec((tile, w), lambda i: (i, 0))
    row2 = lambda w: pl.BlockSpec((2, tile, w), lambda i: (0, i, 0))
    consts = (w_in.astype(BF16), mu_w, mu_a, mu_g.reshape(1, d), w1.astype(BF16), w2.astype(BF16), w0,
              a1.astype(BF16), a2.astype(BF16), a0, g1.astype(BF16), g2.astype(BF16))
    return pl.pallas_call(
        functools.partial(_inproj_body, seq, tile),
        grid=(n // tile,),
        in_specs=[pl.BlockSpec((tile, d), lambda i: (i, 0)), hp_spec, hn_spec] + [full(c) for c in consts],
        out_specs=[row(W_SSM), row(3 * W_RWKV), row(3 * W_CONV), row(W_FFT),
                   row2(W_RWKV), row2(W_RWKV), row(W_RWKV)],
        out_shape=[jax.ShapeDtypeStruct((n, W_SSM), BF16),
                   jax.ShapeDtypeStruct((n, 3 * W_RWKV), BF16),
                   jax.ShapeDtypeStruct((n, 3 * W_CONV), BF16),
                   jax.ShapeDtypeStruct((n, W_FFT), BF16),
                   jax.ShapeDtypeStruct((2, n, W_RWKV), F32),
                   jax.ShapeDtypeStruct((2, n, W_RWKV), BF16),
                   jax.ShapeDtypeStruct((n, W_RWKV), BF16)],
        compiler_params=_cparams(("parallel",)),
        name="inproj",
    )(h, h, h, *consts)


S5_SEGS = SUBLANES


def _cmul(ar, ai, br, bi):
    return ar * br - ai * bi, ar * bi + ai * br


class _S5Dir:
    def __init__(self, fwd, tl, ucur_ref, unext_ref, y_ref, bua_ref, bub_ref, st_ref, perm_ref, permt_ref, bb_ref,
                 a_ref, c_ref):
        self.fwd, self.tl = fwd, tl
        self.ucur_ref, self.unext_ref, self.y_ref = ucur_ref, unext_ref, y_ref
        self.bua_ref, self.bub_ref, self.st_ref = bua_ref, bub_ref, st_ref
        self.perm_ref, self.permt_ref, self.bb_ref, self.c_ref = perm_ref, permt_ref, bb_ref, c_ref
        self.half = SSM_GROUPS * SSM_STATE
        self.rows = tl * S5_SEGS
        self.n_parts = S5_SEGS
        self.part = 2 * self.half // self.n_parts
        shape = (S5_SEGS, self.half)
        self.ar = jnp.broadcast_to(a_ref[:, :self.half], shape)
        self.ai = jnp.broadcast_to(a_ref[:, self.half:], shape)
        self.off_a, self.off_b = (0, tl) if fwd else (tl, 0)

    def project(self, u_ref, off, buf_ref):
        u = u_ref[0, :, off:off + self.tl, :].reshape(self.rows, W_SSM).astype(BF16)
        up = jnp.dot(self.perm_ref[...], u, preferred_element_type=F32).astype(BF16)
        for k in range(self.n_parts):
            cols = slice(k * self.part, (k + 1) * self.part)
            buf_ref[:, cols] = jnp.dot(up, self.bb_ref[:, cols], preferred_element_type=F32)
            yield

    def scan(self, buf_ref, store):
        half, tl = self.half, self.tl
        xr, xi = self.st_ref[:, :half], self.st_ref[:, half:]
        for t in range(tl):
            row = (t if self.fwd else tl - 1 - t) * S5_SEGS
            nr = self.ar * xr - self.ai * xi + buf_ref[row:row + S5_SEGS, :half]
            ni = self.ar * xi + self.ai * xr + buf_ref[row:row + S5_SEGS, half:]
            if store:
                buf_ref[row:row + S5_SEGS, :half] = nr
                buf_ref[row:row + S5_SEGS, half:] = ni
            xr, xi = nr, ni
            if (t + 1) % (tl // self.n_parts) == 0:
                yield
        self.st_ref[:, :half] = xr
        self.st_ref[:, half:] = xi

    def emit(self, buf_ref, off):
        y = None
        for k in range(self.n_parts):
            cols = slice(k * self.part, (k + 1) * self.part)
            yk = _bdot(buf_ref[:, cols], self.c_ref[cols, :])
            y = yk if y is None else y + yk
            yield
        y = jnp.dot(self.permt_ref[...], y.astype(BF16), preferred_element_type=F32)
        self.y_ref[0, :, off:off + self.tl, :] = y.reshape(S5_SEGS, self.tl, W_SSM).astype(self.y_ref.dtype)

    def init_state(self, seg_len):
        half = self.half
        pr, pi = jnp.ones_like(self.ar), jnp.zeros_like(self.ar)
        br, bi = self.ar, self.ai
        e = seg_len
        while e:
            if e & 1:
                pr, pi = _cmul(pr, pi, br, bi)
            br, bi = _cmul(br, bi, br, bi)
            e >>= 1
        er, ei = self.st_ref[:, :half], self.st_ref[:, half:]
        zero = jnp.zeros((1, half), F32)
        order = range(S5_SEGS) if self.fwd else range(S5_SEGS - 1, -1, -1)
        cr, ci, out_r, out_i = zero, zero, {}, {}
        for j in order:
            out_r[j], out_i[j] = cr, ci
            nr, ni = _cmul(pr[0:1], pi[0:1], cr, ci)
            cr, ci = nr + er[j:j + 1], ni + ei[j:j + 1]
        self.st_ref[:, :half] = jnp.concatenate([out_r[j] for j in range(S5_SEGS)], axis=0)
        self.st_ref[:, half:] = jnp.concatenate([out_i[j] for j in range(S5_SEGS)], axis=0)


def _s5_body(tl, seg_len, ucf_ref, unf_ref, ucb_ref, unb_ref, perm_ref, permt_ref, bb_ref, a_ref, c_ref, yf_ref,
             yb_ref, baf_ref, bbf_ref, bab_ref, bbb_ref, st_ref):
    ps = pl.program_id(1)
    i = pl.program_id(2)
    dirs = [_S5Dir(True, tl, ucf_ref, unf_ref, yf_ref, baf_ref, bbf_ref, st_ref.at[0], perm_ref, permt_ref,
                   bb_ref.at[0], a_ref.at[0], c_ref.at[0]),
            _S5Dir(False, tl, ucb_ref, unb_ref, yb_ref, bab_ref, bbb_ref, st_ref.at[1], perm_ref, permt_ref,
                   bb_ref.at[1], a_ref.at[1], c_ref.at[1])]

    @pl.when(jnp.logical_and(i == 0, ps == 0))
    def _():
        st_ref[...] = jnp.zeros_like(st_ref)

    @pl.when(jnp.logical_and(i == 0, ps == 1))
    def _():
        for z in dirs:
            z.init_state(seg_len)

    @pl.when(i == 0)
    def _():
        _run_lockstep([z.project(z.ucur_ref, z.off_a, z.bua_ref) for z in dirs])

    @pl.when(ps == 0)
    def _():
        _run_lockstep([g for z in dirs for g in (z.scan(z.bua_ref, False), z.project(z.ucur_ref, z.off_b, z.bub_ref))])
        _run_lockstep([g for z in dirs for g in (z.scan(z.bub_ref, False), z.project(z.unext_ref, z.off_a, z.bua_ref))])

    @pl.when(ps == 1)
    def _():
        _run_lockstep([g for z in dirs for g in (z.scan(z.bua_ref, True), z.project(z.ucur_ref, z.off_b, z.bub_ref))])
        _run_lockstep([g for z in dirs for g in (z.scan(z.bub_ref, True), z.emit(z.bua_ref, z.off_a),
                                                 z.project(z.unext_ref, z.off_a, z.bua_ref))])
        _run_lockstep([z.emit(z.bub_ref, z.off_b) for z in dirs])


def _s5_params(lam_re, lam_im, log_dt, b_re, b_im, c_re, c_im):
    g, p, hch = SSM_GROUPS, SSM_STATE, SSM_CH
    dt = jnp.exp(log_dt)[..., None]
    mag = jnp.exp(lam_re * dt)
    lb_re = mag * jnp.cos(lam_im * dt)
    lb_im = mag * jnp.sin(lam_im * dt)
    den = lam_re * lam_re + lam_im * lam_im
    nr = lb_re - 1.0
    coef_re = (nr * lam_re + lb_im * lam_im) / den
    coef_im = (lb_im * lam_re - nr * lam_im) / den
    bb_re = coef_re[..., None] * b_re - coef_im[..., None] * b_im
    bb_im = coef_re[..., None] * b_im + coef_im[..., None] * b_re
    eye = jnp.eye(g, dtype=F32)
    bd_in = lambda m: jnp.einsum('dgph,gk->dghkp', m, eye).reshape(2, g * hch, g * p)
    bb = jnp.concatenate([bd_in(bb_re), bd_in(bb_im)], axis=-1)
    bd_out = lambda m: jnp.einsum('dghp,gk->dgpkh', m, eye).reshape(2, g * p, g * hch)
    cc = jnp.concatenate([bd_out(c_re), -bd_out(c_im)], axis=1)
    a = jnp.concatenate([lb_re.reshape(2, 1, g * p), lb_im.reshape(2, 1, g * p)], axis=-1)
    return bb, a, cc


def _s5_scan(p_ssm, bsz, seq, bb, a, cc, tl):
    seg_len = seq // S5_SEGS
    n_pairs = seg_len // (2 * tl)
    rows = tl * S5_SEGS
    w2 = 2 * SSM_GROUPS * SSM_STATE
    src = (np.arange(rows) % S5_SEGS) * tl + np.arange(rows) // S5_SEGS
    perm = np.zeros((rows, rows), np.float32)
    perm[np.arange(rows), src] = 1.0
    last = n_pairs - 1
    u4 = p_ssm.reshape(bsz, S5_SEGS, seg_len, W_SSM)
    ublk = (1, S5_SEGS, 2 * tl, W_SSM)
    const = lambda shape: pl.BlockSpec(shape, lambda b, ps, i: (0,) * len(shape))
    y_f, y_b = pl.pallas_call(
        functools.partial(_s5_body, tl, seg_len),
        grid=(bsz, 2, n_pairs),
        in_specs=[pl.BlockSpec(ublk, lambda b, ps, i: (b, 0, i, 0)),
                  pl.BlockSpec(ublk, lambda b, ps, i: (b, 0, jnp.minimum(i + 1, last), 0)),
                  pl.BlockSpec(ublk, lambda b, ps, i: (b, 0, last - i, 0)),
                  pl.BlockSpec(ublk, lambda b, ps, i: (b, 0, jnp.maximum(last - i - 1, 0), 0)),
                  const((rows, rows)), const((rows, rows)), const((2, W_SSM, w2)), const((2, 1, w2)),
                  const((2, w2, W_SSM))],
        out_specs=[pl.BlockSpec(ublk, lambda b, ps, i: (b, 0, jnp.where(ps == 0, 0, i), 0)),
                   pl.BlockSpec(ublk, lambda b, ps, i: (b, 0, jnp.where(ps == 0, last, last - i), 0))],
        out_shape=[jax.ShapeDtypeStruct((bsz, S5_SEGS, seg_len, W_SSM), BF16)] * 2,
        scratch_shapes=[pltpu.VMEM((rows, w2), F32)] * 4 + [pltpu.VMEM((2, S5_SEGS, w2), F32)],
        compiler_params=_cparams(("arbitrary",) * 3),
        name="s5_scan",
    )(u4, u4, u4, u4, jnp.asarray(perm).astype(BF16), jnp.asarray(perm.T).astype(BF16), bb.astype(BF16), a,
      cc.astype(BF16))
    return y_f.reshape(bsz * seq, W_SSM), y_b.reshape(bsz * seq, W_SSM)


def _gelu_tanh(x):
    c = math.sqrt(2.0 / math.pi)
    return 0.5 * x * (1.0 + jnp.tanh(c * (x + 0.044715 * (x * x * x))))


def _s5_post(y_f, y_b, u, dskip, glu_w, glu_b):
    y = _gelu_tanh(y_f.astype(F32) + y_b.astype(F32) + dskip * u)
    return y * jax.nn.sigmoid(_bdot(y, glu_w) + glu_b)


def _split_dot(x, w):
    hi = x.astype(BF16)
    lo = (x - hi.astype(F32)).astype(BF16)
    return (jnp.dot(hi, w, preferred_element_type=F32) + jnp.dot(lo, w, preferred_element_type=F32))


def _dot_nt(a, b):
    return lax.dot_general(a.astype(BF16), b.astype(BF16), (((1,), (1,)), ((), ())),
                           preferred_element_type=F32)


def _rwkv_masks(chunk):
    hh = RWKV_HEADS
    r = np.arange(hh * chunk)[:, None] // chunk
    bm_feat = (r == np.arange(W_RWKV)[None, :] // RWKV_HEAD).astype(np.float32)
    bm_time = (r == np.arange(hh * chunk)[None, :] // chunk).astype(np.float32)
    f = np.arange(W_RWKV)
    bm_head = (f[:, None] // RWKV_HEAD == f[None, :] // RWKV_HEAD).astype(np.float32)
    return bm_feat, bm_time, bm_head


def _rwkv_chunk(fwd, x, edge_row, logw, a, mu, k_k, k_a, r_k, bmf16, bmt16, bmh):
    ll = x.shape[0]
    w = W_RWKV
    rows = lax.broadcasted_iota(jnp.int32, x.shape, 0)
    if fwd:
        shifted = jnp.where(rows == 0, edge_row, pltpu.roll(x, 1, axis=0))
    else:
        shifted = jnp.where(rows == ll - 1, edge_row, pltpu.roll(x, ll - 1, axis=0))
    rkv = x + (shifted - x) * mu
    r, k, v = rkv[:, :w], rkv[:, w:2 * w], rkv[:, 2 * w:]
    bmh16 = bmh.astype(BF16)

    kk = k * k_k
    ksq = _split_dot(kk * kk, bmh16)
    k2 = k * (1.0 + (a - 1.0) * k_a)
    bonus = _split_dot(r * k2 * r_k, bmh16) * v

    ti = lax.broadcasted_iota(jnp.int32, (ll, ll), 0)
    si = lax.broadcasted_iota(jnp.int32, (ll, ll), 1)
    tri = ((si <= ti) if fwd else (si >= ti)).astype(F32).astype(BF16)
    cum = _split_dot_left(tri, logw)
    yield
    kk = kk * lax.rsqrt(ksq + 1e-12)
    ctot = jnp.sum(logw, axis=0, keepdims=True)
    e_neg = jnp.exp(-cum)
    e_rem = jnp.exp(ctot - cum)
    ah = -kk * jnp.exp(cum - logw)
    rh = r * jnp.exp(cum)
    bvec = kk * a
    bh, kh = bvec * e_neg, k2 * e_neg
    bt, kt = bvec * e_rem, k2 * e_rem

    def bd(m):
        m16 = m.astype(BF16)
        return jnp.concatenate([m16] * RWKV_HEADS, axis=0) * (bmf16 if m.shape[1] == w else bmt16)

    gram = _dot_nt(jnp.concatenate([ah, rh], axis=0), jnp.concatenate([bd(bh), bd(kh)], axis=0))
    yield
    l4 = RWKV_HEADS * ll
    tt = lax.broadcasted_iota(jnp.int32, (ll, l4), 0)
    ss = lax.broadcasted_iota(jnp.int32, (ll, l4), 1) % ll
    strict = (ss < tt) if fwd else (ss > tt)
    incl = (ss <= tt) if fwd else (ss >= tt)
    n_ab = jnp.where(strict, gram[:ll, :l4], 0.0)
    a_ak = jnp.where(strict, gram[:ll, l4:], 0.0)
    a_rb = jnp.where(incl, gram[ll:, :l4], 0.0)
    a_rk = jnp.where(incl, gram[ll:, l4:], 0.0)

    pw = n_ab
    tinv = jnp.where(ss == tt, 1.0, 0.0) + n_ab
    akv = _bdot(a_ak, bd(v))
    for _ in range(int(math.log2(ll)) - 1):
        pw = _bdot(pw, bd(pw))
        yield
        tinv = tinv + _bdot(tinv, bd(pw))
        yield

    ta = _bdot(tinv, jnp.concatenate([bd(ah), bd(akv)], axis=1))
    yield
    ap, wm = ta[:, :w], ta[:, w:]
    rp = rh + _bdot(a_rb, bd(ap))
    y0 = _bdot(jnp.concatenate([a_rb, a_rk], axis=1), jnp.concatenate([bd(wm), bd(v)], axis=0))
    btk = jnp.transpose(jnp.concatenate([bt, kt], axis=0))
    rhs = jnp.concatenate([jnp.concatenate([ap, wm], axis=1),
                           jnp.concatenate([jnp.zeros_like(v), v], axis=1)], axis=0)
    pq = _bdot(btk, rhs)
    yield
    eye = (lax.broadcasted_iota(jnp.int32, (w, w), 0) == lax.broadcasted_iota(jnp.int32, (w, w), 1))
    pm = pq[:, :w] * bmh + jnp.where(eye, jnp.exp(ctot), 0.0)
    qm = pq[:, w:] * bmh
    yield rp, y0, pm, qm, bonus


def _run_lockstep(gens):
    results = [None] * len(gens)
    live = list(range(len(gens)))
    while live:
        for s in list(live):
            try:
                out = next(gens[s])
            except StopIteration:
                live.remove(s)
            else:
                if out is not None:
                    results[s] = out
    return results


def _rwkv_body(chunk, nsub, nb, *refs):
    dir_refs = [refs[0:4], refs[4:8]]
    mu_ref, kk_ref, ka_ref, rk_ref, bmf_ref, bmt_ref, bmh_ref = refs[8:15]
    y_refs, bonus_refs, h_ref = refs[15:17], refs[17:19], refs[19]
    c = pl.program_id(0)

    @pl.when(c == 0)
    def _():
        h_ref[...] = jnp.zeros_like(h_ref)

    gens, where = [], []
    for d in range(2):
        x_ref, edge_ref, logw_ref, a_ref = dir_refs[d]
        order = range(nsub) if d == 0 else range(nsub - 1, -1, -1)
        for b in range(nb):
            x = x_ref[b].astype(F32)
            for n, j in enumerate(order):
                rows = slice(j * chunk, (j + 1) * chunk)
                if n == 0:
                    hb = edge_ref.shape[1]
                    edge = edge_ref[b, hb - 1:hb, :] if d == 0 else edge_ref[b, 0:1, :]
                    edge = jnp.where(c == 0, 0.0, edge.astype(F32))
                else:
                    e = j * chunk - 1 if d == 0 else (j + 1) * chunk
                    edge = x[e:e + 1, :]
                gens.append(_rwkv_chunk(d == 0, x[rows], edge, logw_ref[0, b, rows, :],
                                        a_ref[0, b, rows, :].astype(F32), mu_ref[d], kk_ref[...], ka_ref[...],
                                        rk_ref[...], bmf_ref[...], bmt_ref[...], bmh_ref[...]))
                where.append((d, b, rows))
    parts = _run_lockstep(gens)

    def carry(s):
        h = h_ref[s]
        for n in range(nsub):
            rp, y0, pm, qm, bonus = parts[s * nsub + n]
            d, b, rows = where[s * nsub + n]
            y_refs[d][b, rows, :] = (_bdot(rp, h) + y0).astype(y_refs[d].dtype)
            bonus_refs[d][b, rows, :] = bonus.astype(bonus_refs[d].dtype)
            h = _bdot(pm, h) + qm
            yield
        h_ref[s] = h

    _run_lockstep([carry(s) for s in range(2 * nb)])


def _split_dot_left(w, x):
    hi = x.astype(BF16)
    lo = (x - hi.astype(F32)).astype(BF16)
    return (jnp.dot(w, hi, preferred_element_type=F32) + jnp.dot(w, lo, preferred_element_type=F32))


def _rwkv_scan(p_rkv, logw, agate, bsz, seq, chunk, nsub, mu_rkv, k_k, k_a, r_k):
    n = bsz * seq
    blk = chunk * nsub
    n_chunks = seq // blk
    hb = PACKED_ROWS
    per = blk // hb
    w3 = 3 * W_RWKV
    bmf, bmt, bmh = _rwkv_masks(chunk)
    consts = (mu_rkv.reshape(2, 1, w3), k_k.reshape(1, -1), k_a.reshape(1, -1), r_k.reshape(1, -1),
              jnp.asarray(bmf).astype(BF16), jnp.asarray(bmt).astype(BF16), jnp.asarray(bmh))
    x3 = p_rkv.reshape(bsz, seq, w3)
    logw4 = logw.reshape(2, bsz, seq, W_RWKV)
    a4 = agate.reshape(2, bsz, seq, W_RWKV)
    args, in_specs, out_specs = [], [], []
    for d in range(2):
        cidx = (lambda c: c) if d == 0 else (lambda c: n_chunks - 1 - c)
        if d == 0:
            edge = lambda c: (0, jnp.maximum(c * per - 1, 0), 0)
        else:
            edge = lambda c: (0, jnp.minimum((n_chunks - c) * per, seq // hb - 1), 0)
        args += [x3, x3, logw4, a4]
        in_specs += [pl.BlockSpec((bsz, blk, w3), lambda c, cidx=cidx: (0, cidx(c), 0)),
                     pl.BlockSpec((bsz, hb, w3), edge),
                     pl.BlockSpec((1, bsz, blk, W_RWKV), lambda c, d=d, cidx=cidx: (d, 0, cidx(c), 0)),
                     pl.BlockSpec((1, bsz, blk, W_RWKV), lambda c, d=d, cidx=cidx: (d, 0, cidx(c), 0))]
        out_specs.append(pl.BlockSpec((bsz, blk, W_RWKV), lambda c, cidx=cidx: (0, cidx(c), 0)))

    def full(a):
        nd = a.ndim
        return pl.BlockSpec(a.shape, lambda c: (0,) * nd)

    y_f, y_b, bonus_f, bonus_b = pl.pallas_call(
        functools.partial(_rwkv_body, chunk, nsub, bsz),
        grid=(n_chunks,),
        in_specs=in_specs + [full(a) for a in consts],
        out_specs=out_specs + out_specs,
        out_shape=[jax.ShapeDtypeStruct((bsz, seq, W_RWKV), BF16)] * 4,
        scratch_shapes=[pltpu.VMEM((2 * bsz, W_RWKV, W_RWKV), F32)],
        compiler_params=_cparams(("arbitrary",)),
        name="rwkv_scan",
    )(*args, *consts)
    return [a.reshape(n, W_RWKV) for a in (y_f, y_b, bonus_f, bonus_b)]


def _head_mean(x, bmh16):
    return _split_dot(x, bmh16) * (1.0 / RWKV_HEAD)


def _rwkv_post(y, bonus, g, gn_g, gn_b, bmh16):
    mu = _head_mean(y, bmh16)
    yc = y - mu
    var = _head_mean(yc * yc, bmh16)
    yn = yc * lax.rsqrt(var + RWKV_GN_EPS) * gn_g + gn_b
    return (yn + bonus) * g


FFT_R1 = 64


def _fft_tables(seq):
    r1, r2 = FFT_R1, seq // FFT_R1
    i1 = jnp.arange(r1, dtype=jnp.int32)
    ang1 = ((i1[:, None] * i1[None, :]) % r1).astype(F32) * (2.0 * math.pi / r1)
    stage1 = jnp.concatenate([jnp.cos(ang1), -jnp.sin(ang1)], axis=0)
    k1 = i1[:, None, None]
    k2 = jnp.arange(r2, dtype=jnp.int32)[None, :, None]
    n2 = jnp.arange(r2, dtype=jnp.int32)[None, None, :]
    ang2 = ((n2 * (k1 + r1 * k2)) % seq).astype(F32) * (2.0 * math.pi / seq)
    mr, mi = jnp.cos(ang2), -jnp.sin(ang2)
    stage2 = jnp.concatenate([jnp.concatenate([mr, -mi], axis=2),
                              jnp.concatenate([mi, mr], axis=2)], axis=1)
    c = jnp.arange(W_FFT, dtype=jnp.int32)
    same = (c[:, None] // FFT_CH) == (c[None, :] // FFT_CH)
    angc = ((c[:, None] * c[None, :]) % FFT_CH).astype(F32) * (2.0 * math.pi / FFT_CH)
    scale = 1.0 / math.sqrt(seq * FFT_CH)
    chan = jnp.concatenate([jnp.where(same, jnp.cos(angc), 0.0), jnp.where(same, jnp.sin(angc), 0.0)],
                           axis=0) * scale
    return stage1, stage2, chan


def _fft1_body(z_ref, m_ref, are_ref, aim_ref):
    r1 = FFT_R1
    a = _bdot(m_ref[...], z_ref[0])
    are_ref[0] = a[:r1].astype(are_ref.dtype)
    aim_ref[0] = a[r1:].astype(aim_ref.dtype)


def _fft2_body(kb, r2, are_ref, aim_ref, m_ref, chan_ref, o_ref):
    for j in range(kb):
        x = jnp.concatenate([are_ref[0, j], aim_ref[0, j]], axis=0)
        f = _bdot(m_ref[j], x)
        fri = jnp.concatenate([f[:r2], f[r2:]], axis=1)
        o_ref[0, :, j * W_FFT:(j + 1) * W_FFT] = _bdot(fri, chan_ref[...]).astype(o_ref.dtype)


def _fourier(p_fft, bsz, seq, tables=None):
    r1, r2 = FFT_R1, seq // FFT_R1
    stage1, stage2, chan = tables if tables is not None else _fft_tables(seq)
    cols = r2 * W_FFT
    tc = min(cols, 4096)
    z = p_fft.reshape(bsz, r1, cols)
    blk = pl.BlockSpec((1, r1, tc), lambda b, j: (b, 0, j))
    a_re, a_im = pl.pallas_call(
        _fft1_body,
        grid=(bsz, cols // tc),
        in_specs=[blk, pl.BlockSpec((2 * r1, r1), lambda b, j: (0, 0))],
        out_specs=[blk, blk],
        out_shape=[jax.ShapeDtypeStruct((bsz, r1, cols), BF16)] * 2,
        compiler_params=_cparams(("parallel", "parallel")),
        name="fft_stage1",
    )(z, stage1.astype(BF16))
    kb = 8
    ablk = pl.BlockSpec((1, kb, r2, W_FFT), lambda b, j: (b, j, 0, 0))
    out = pl.pallas_call(
        functools.partial(_fft2_body, kb, r2),
        grid=(bsz, r1 // kb),
        in_specs=[ablk, ablk,
                  pl.BlockSpec((kb, 2 * r2, 2 * r2), lambda b, j: (j, 0, 0)),
                  pl.BlockSpec((2 * W_FFT, W_FFT), lambda b, j: (0, 0))],
        out_specs=pl.BlockSpec((1, r2, kb * W_FFT), lambda b, j: (b, 0, j)),
        out_shape=jax.ShapeDtypeStruct((bsz, r2, r1 * W_FFT), BF16),
        compiler_params=_cparams(("parallel", "parallel")),
        name="fft_stage2",
    )(a_re.reshape(bsz, r1, r2, W_FFT), a_im.reshape(bsz, r1, r2, W_FFT), stage2.astype(BF16), chan.astype(BF16))
    return out.reshape(bsz * seq, W_FFT)


def _mixout_body(seq, tile, h_ref, ysf_ref, ysb_ref, pssm_ref, yrf_ref, yrb_ref, bnf_ref, bnb_ref, g_ref, pc_ref,
                 pcp_ref, pcn_ref, yf_ref, wout_ref, dskip_ref, gluw_ref, glub_ref, gng_ref, gnb_ref, convw_ref,
                 bmh_ref, lng_ref, lnb_ref, o_ref):
    i = pl.program_id(0)
    f32 = lambda ref: ref[...].astype(F32)
    y_a = _s5_post(f32(ysf_ref), f32(ysb_ref), f32(pssm_ref), dskip_ref[...], gluw_ref[...], glub_ref[...])
    y_b = _rwkv_post(f32(yrf_ref) + f32(yrb_ref), f32(bnf_ref) + f32(bnb_ref), f32(g_ref), gng_ref[...],
                     gnb_ref[...], bmh_ref[...])
    pc = f32(pc_ref)
    wc = W_CONV
    prev_row, next_row = _halo_rows(pcp_ref, pcn_ref, i * tile, tile, seq)
    z = pc[:, wc:2 * wc] * pc[:, 2 * wc:]
    z_prev, z_next = _shift_rows(z, prev_row[:, wc:2 * wc] * prev_row[:, 2 * wc:],
                                 next_row[:, wc:2 * wc] * next_row[:, 2 * wc:])
    y_c = pc[:, :wc] * (convw_ref[0:1, :] * z_prev + convw_ref[1:2, :] * z + convw_ref[2:3, :] * z_next)
    mix = (_bdot(y_a, wout_ref[0:W_SSM, :]) + _bdot(y_b, wout_ref[W_SSM:W_SSM + W_RWKV, :])
           + _bdot(y_c, wout_ref[W_SSM + W_RWKV:W_SSM + W_RWKV + W_CONV, :])
           + _bdot(yf_ref[...], wout_ref[W_SSM + W_RWKV + W_CONV:, :]))
    o_ref[...] = _layer_norm(DEEPNORM_ALPHA * h_ref[...] + mix, lng_ref[...], lnb_ref[...])


def _mixout(h, seq, tile, y_s5, p_ssm, rwkv_outs, g, p_conv, y_fft, w_out, dskip, glu_w, glu_b, gn_g, gn_b,
            conv_w, ln_g, ln_b):
    n, d = h.shape
    row = lambda w: pl.BlockSpec((tile, w), lambda i: (i, 0))
    pcp, pcn = _halo_specs(tile, 3 * W_CONV, n, PACKED_ROWS)
    bmh16 = jnp.asarray(_rwkv_masks(RWKV_HEAD)[2]).astype(BF16)
    consts = (w_out.astype(BF16), dskip.reshape(1, -1), glu_w.astype(BF16), glu_b.reshape(1, -1),
              gn_g.reshape(1, -1), gn_b.reshape(1, -1), conv_w, bmh16, ln_g.reshape(1, -1), ln_b.reshape(1, -1))

    def full(a):
        nd = a.ndim
        return pl.BlockSpec(a.shape, lambda i: (0,) * nd)

    return pl.pallas_call(
        functools.partial(_mixout_body, seq, tile),
        grid=(n // tile,),
        in_specs=[row(d)] + [row(W_SSM)] * 3 + [row(W_RWKV)] * 5
                 + [row(3 * W_CONV), pcp, pcn, row(W_FFT)] + [full(c) for c in consts],
        out_specs=row(d),
        out_shape=jax.ShapeDtypeStruct((n, d), F32),
        compiler_params=_cparams(("parallel",)),
        name="mix_out",
    )(h, *y_s5, p_ssm, *rwkv_outs, g, p_conv, p_conv, p_conv, y_fft, *consts)


def _ffn_body(cols, h_ref, w1_ref, w3_ref, w2_ref, lng_ref, lnb_ref, o_ref, u_ref):
    x = h_ref[...].astype(BF16)
    dff = w1_ref.shape[1]
    for c0 in range(0, dff, cols):
        cs = slice(c0, min(c0 + cols, dff))
        a1 = jnp.dot(x, w1_ref[:, cs], preferred_element_type=F32)
        a3 = jnp.dot(x, w3_ref[:, cs], preferred_element_type=F32)
        u_ref[:, cs] = (jax.nn.silu(a1) * a3).astype(BF16)
    f = jnp.dot(u_ref[...], w2_ref[...], preferred_element_type=F32)
    o_ref[...] = _layer_norm(DEEPNORM_ALPHA * h_ref[...] + f, lng_ref[...], lnb_ref[...])


def _ffn(h, tile, cols, w1, w3, w2, ln_g, ln_b):
    n, d = h.shape
    dff = w1.shape[1]
    once = lambda shape: pl.BlockSpec(shape, lambda i: (0, 0), pipeline_mode=pl.Buffered(1))
    return pl.pallas_call(
        functools.partial(_ffn_body, cols),
        grid=(n // tile,),
        in_specs=[pl.BlockSpec((tile, d), lambda i: (i, 0)), once((d, dff)), once((d, dff)), once((dff, d)),
                  once((1, d)), once((1, d))],
        out_specs=pl.BlockSpec((tile, d), lambda i: (i, 0)),
        out_shape=jax.ShapeDtypeStruct((n, d), F32),
        scratch_shapes=[pltpu.VMEM((tile, dff), BF16)],
        compiler_params=_cparams(("parallel",)),
        name="ffn",
    )(h, w1.astype(BF16), w3.astype(BF16), w2.astype(BF16), ln_g.reshape(1, d), ln_b.reshape(1, d))


ROW_TILE = 512
S5_TILE_STEPS = 64
RWKV_CHUNK = 64
RWKV_CHUNKS_PER_STEP = 2
FFN_COLS = 256


def kernel(x, ln0_g, ln0_b, w_in, s5_lambda_re, s5_lambda_im, s5_log_dt, s5_b_re, s5_b_im, s5_c_re, s5_c_im, s5_d,
           s5_glu_w, s5_glu_b, rwkv_mu_rkv, rwkv_mu_w, rwkv_mu_a, rwkv_mu_g, rwkv_w0, rwkv_w1, rwkv_w2, rwkv_a0,
           rwkv_a1, rwkv_a2, rwkv_g1, rwkv_g2, rwkv_k_k, rwkv_k_a, rwkv_r_k, rwkv_gn_g, rwkv_gn_b, conv_w, w_out,
           ln1_g, ln1_b, ffn_w1, ffn_w3, ffn_w2, ln2_g, ln2_b):
    bsz, seq, d = x.shape
    n = bsz * seq
    tile = min(ROW_TILE, seq)
    s5_tl = min(S5_TILE_STEPS, seq // S5_SEGS // 2)
    fft_tables = _fft_tables(seq)
    h = _ln0(x.reshape(n, d), ln0_g, ln0_b, tile)
    for l in range(w_in.shape[0]):
        p_ssm, p_rkv, p_conv, p_fft, logw, agate, g = _inproj(
            h, seq, tile, w_in[l], rwkv_mu_w[l], rwkv_mu_a[l], rwkv_mu_g[l], rwkv_w1[l], rwkv_w2[l], rwkv_w0[l],
            rwkv_a1[l], rwkv_a2[l], rwkv_a0[l], rwkv_g1[l], rwkv_g2[l])
        bb, a, cc = _s5_params(s5_lambda_re[l], s5_lambda_im[l], s5_log_dt[l], s5_b_re[l], s5_b_im[l],
                               s5_c_re[l], s5_c_im[l])
        y_s5 = _s5_scan(p_ssm, bsz, seq, bb, a, cc, s5_tl)
        rwkv_outs = _rwkv_scan(p_rkv, logw, agate, bsz, seq, RWKV_CHUNK, RWKV_CHUNKS_PER_STEP, rwkv_mu_rkv[l], rwkv_k_k[l],
                               rwkv_k_a[l], rwkv_r_k[l].reshape(-1))
        y_fft = _fourier(p_fft, bsz, seq, fft_tables)
        h = _mixout(h, seq, tile, y_s5, p_ssm, rwkv_outs, g, p_conv, y_fft, w_out[l], s5_d[l], s5_glu_w[l],
                    s5_glu_b[l], rwkv_gn_g[l], rwkv_gn_b[l], conv_w[l], ln1_g[l], ln1_b[l])
        h = _ffn(h, tile, FFN_COLS, ffn_w1[l], ffn_w3[l], ffn_w2[l], ln2_g[l], ln2_b[l])
    return h.reshape(bsz, seq, d)
```

```python
import functools
import math

import jax
import jax.numpy as jnp
import numpy as np
from jax import lax
from jax.experimental import pallas as pl
from jax.experimental.pallas import tpu as pltpu

W_SSM = 256
W_RWKV = 256
W_CONV = 256
W_FFT = 256
SSM_CH = 16
SSM_GROUPS = 16
SSM_STATE = 64
RWKV_HEAD = 64
RWKV_HEADS = 4
RWKV_PAIR = 2 * RWKV_HEAD
FFT_GROUPS = 4
FFT_CH = 64
RWKV_DECAY_SCALE = math.exp(-0.5)
RWKV_GN_EPS = 64e-5
LN_EPS = 1e-5
DEPTH = 2
DEEPNORM_ALPHA = (2 * DEPTH) ** 0.25

SUBLANES = 8
PACKED_ROWS = 16
VMEM_LIMIT = 48 * 1024 * 1024

BF16 = jnp.bfloat16
F32 = jnp.float32


def _cparams(sem):
    return pltpu.CompilerParams(dimension_semantics=sem, vmem_limit_bytes=VMEM_LIMIT)


def _bdot(a, b):
    return jnp.dot(a.astype(BF16), b.astype(BF16), preferred_element_type=F32)


def _layer_norm(x, g, b):
    mu = jnp.mean(x, axis=-1, keepdims=True)
    xc = x - mu
    var = jnp.mean(xc * xc, axis=-1, keepdims=True)
    return xc * lax.rsqrt(var + LN_EPS) * g + b


def _shift_rows(x, prev_row, next_row):
    n = x.shape[0]
    rows = lax.broadcasted_iota(jnp.int32, x.shape, 0)
    x_prev = jnp.where(rows == 0, prev_row, pltpu.roll(x, 1, axis=0))
    x_next = jnp.where(rows == n - 1, next_row, pltpu.roll(x, n - 1, axis=0))
    return x_prev, x_next


def _halo_rows(prev_ref, next_ref, row0, n_rows, seq):
    first = (row0 % seq) == 0
    last = ((row0 + n_rows) % seq) == 0
    hb = prev_ref.shape[0]
    prev_row = jnp.where(first, 0.0, prev_ref[hb - 1:hb, :].astype(F32))
    next_row = jnp.where(last, 0.0, next_ref[0:1, :].astype(F32))
    return prev_row, next_row


def _halo_specs(tile, width, n_rows_total, hb):
    per = tile // hb
    last_blk = n_rows_total // hb - 1
    prev = pl.BlockSpec((hb, width), lambda i: (jnp.maximum(i * per - 1, 0), 0))
    nxt = pl.BlockSpec((hb, width), lambda i: (jnp.minimum((i + 1) * per, last_blk), 0))
    return prev, nxt


def _ln0_body(x_ref, g_ref, b_ref, o_ref):
    o_ref[...] = _layer_norm(x_ref[...], g_ref[...], b_ref[...])


def _ln0(x2, g, b, tile):
    n, d = x2.shape
    return pl.pallas_call(
        _ln0_body,
        grid=(n // tile,),
        in_specs=[pl.BlockSpec((tile, d), lambda i: (i, 0)),
                  pl.BlockSpec((1, d), lambda i: (0, 0)),
                  pl.BlockSpec((1, d), lambda i: (0, 0))],
        out_specs=pl.BlockSpec((tile, d), lambda i: (i, 0)),
        out_shape=jax.ShapeDtypeStruct((n, d), F32),
        compiler_params=_cparams(("parallel",)),
        name="ln0",
    )(x2, g.reshape(1, d), b.reshape(1, d))


def _inproj_body(seq, tile, h_ref, hp_ref, hn_ref, win_ref, muw_ref, mua_ref, mug_ref,
                 w1_ref, w2_ref, w0_ref, a1_ref, a2_ref, a0_ref, g1_ref, g2_ref,
                 pssm_ref, prkv_ref, pconv_ref, pfft_ref, logw_ref, agate_ref, g_ref):
    i = pl.program_id(0)
    h = h_ref[...]
    prev_row, next_row = _halo_rows(hp_ref, hn_ref, i * tile, tile, seq)
    x_prev, x_next = _shift_rows(h, prev_row, next_row)

    h16 = h.astype(BF16)

    def main():
        col = 0
        for ref in (pssm_ref, prkv_ref, pconv_ref, pfft_ref):
            width = ref.shape[1]
            ref[...] = jnp.dot(h16, win_ref[:, col:col + width], preferred_element_type=F32).astype(ref.dtype)
            col += width
            yield

    def lora(d, x_sh):
        dx = x_sh - h
        t1 = _bdot(h + dx * muw_ref[d:d + 1, :], w1_ref[d])
        yield
        w_lora = _bdot(jnp.tanh(t1), w2_ref[d])
        t2 = _bdot(h + dx * mua_ref[d:d + 1, :], a1_ref[d])
        yield
        logw_ref[d] = -RWKV_DECAY_SCALE * jax.nn.sigmoid(w0_ref[d:d + 1, :] + w_lora)
        a_lora = _bdot(t2, a2_ref[d])
        yield
        agate_ref[d] = jax.nn.sigmoid(a0_ref[d:d + 1, :] + a_lora).astype(agate_ref.dtype)

    def gate():
        xg = h + (0.5 * (x_prev + x_next) - h) * mug_ref[...]
        t3 = _bdot(xg, g1_ref[...])
        yield
        g_ref[...] = _bdot(jax.nn.sigmoid(t3), g2_ref[...]).astype(g_ref.dtype)

    _run_lockstep([main(), lora(0, x_prev), lora(1, x_next), gate()])


def _inproj(h, seq, tile, w_in, mu_w, mu_a, mu_g, w1, w2, w0, a1, a2, a0, g1, g2):
    n, d = h.shape
    hp_spec, hn_spec = _halo_specs(tile, d, n, SUBLANES)

    def full(a):
        nd = a.ndim
        return pl.BlockSpec(a.shape, lambda i: (0,) * nd)

    row = lambda w: pl.BlockSpec((tile, w), lambda i: (i, 0))
    row2 = lambda w: pl.BlockSpec((2, tile, w), lambda i: (0, i, 0))
    consts = (w_in.astype(BF16), mu_w, mu_a, mu_g.reshape(1, d), w1.astype(BF16), w2.astype(BF16), w0,
              a1.astype(BF16), a2.astype(BF16), a0, g1.astype(BF16), g2.astype(BF16))
    return pl.pallas_call(
        functools.partial(_inproj_body, seq, tile),
        grid=(n // tile,),
        in_specs=[pl.BlockSpec((tile, d), lambda i: (i, 0)), hp_spec, hn_spec] + [full(c) for c in consts],
        out_specs=[row(W_SSM), row(3 * W_RWKV), row(3 * W_CONV), row(W_FFT),
                   row2(W_RWKV), row2(W_RWKV), row(W_RWKV)],
        out_shape=[jax.ShapeDtypeStruct((n, W_SSM), BF16),
                   jax.ShapeDtypeStruct((n, 3 * W_RWKV), BF16),
                   jax.ShapeDtypeStruct((n, 3 * W_CONV), BF16),
                   jax.ShapeDtypeStruct((n, W_FFT), BF16),
                   jax.ShapeDtypeStruct((2, n, W_RWKV), F32),
                   jax.ShapeDtypeStruct((2, n, W_RWKV), BF16),
                   jax.ShapeDtypeStruct((n, W_RWKV), BF16)],
        compiler_params=_cparams(("parallel",)),
        name="inproj",
    )(h, h, h, *consts)


S5_SEGS = SUBLANES


def _cmul(ar, ai, br, bi):
    return ar * br - ai * bi, ar * bi + ai * br


class _S5Dir:
    def __init__(self, fwd, tl, ucur_ref, unext_ref, y_ref, bua_ref, bub_ref, st_ref, perm_ref, permt_ref, bb_ref,
                 a_ref, c_ref):
        self.fwd, self.tl = fwd, tl
        self.ucur_ref, self.unext_ref, self.y_ref = ucur_ref, unext_ref, y_ref
        self.bua_ref, self.bub_ref, self.st_ref = bua_ref, bub_ref, st_ref
        self.perm_ref, self.permt_ref, self.bb_ref, self.c_ref = perm_ref, permt_ref, bb_ref, c_ref
        self.half = SSM_GROUPS * SSM_STATE
        self.rows = tl * S5_SEGS
        self.n_parts = S5_SEGS
        self.part = 2 * self.half // self.n_parts
        shape = (S5_SEGS, self.half)
        self.ar = jnp.broadcast_to(a_ref[:, :self.half], shape)
        self.ai = jnp.broadcast_to(a_ref[:, self.half:], shape)
        self.off_a, self.off_b = (0, tl) if fwd else (tl, 0)

    def project(self, u_ref, off, buf_ref):
        u = u_ref[0, :, off:off + self.tl, :].reshape(self.rows, W_SSM).astype(BF16)
        up = jnp.dot(self.perm_ref[...], u, preferred_element_type=F32).astype(BF16)
        for k in range(self.n_parts):
            cols = slice(k * self.part, (k + 1) * self.part)
            buf_ref[:, cols] = jnp.dot(up, self.bb_ref[:, cols], preferred_element_type=F32)
            yield

    def scan(self, buf_ref, store):
        half, tl = self.half, self.tl
        xr, xi = self.st_ref[:, :half], self.st_ref[:, half:]
        for t in range(tl):
            row = (t if self.fwd else tl - 1 - t) * S5_SEGS
            nr = self.ar * xr - self.ai * xi + buf_ref[row:row + S5_SEGS, :half]
            ni = self.ar * xi + self.ai * xr + buf_ref[row:row + S5_SEGS, half:]
            if store:
                buf_ref[row:row + S5_SEGS, :half] = nr
                buf_ref[row:row + S5_SEGS, half:] = ni
            xr, xi = nr, ni
            if (t + 1) % (tl // self.n_parts) == 0:
                yield
        self.st_ref[:, :half] = xr
        self.st_ref[:, half:] = xi

    def emit(self, buf_ref, off):
        y = None
        for k in range(self.n_parts):
            cols = slice(k * self.part, (k + 1) * self.part)
            yk = _bdot(buf_ref[:, cols], self.c_ref[cols, :])
            y = yk if y is None else y + yk
            yield
        y = jnp.dot(self.permt_ref[...], y.astype(BF16), preferred_element_type=F32)
        self.y_ref[0, :, off:off + self.tl, :] = y.reshape(S5_SEGS, self.tl, W_SSM).astype(self.y_ref.dtype)

    def init_state(self, seg_len):
        half = self.half
        pr, pi = jnp.ones_like(self.ar), jnp.zeros_like(self.ar)
        br, bi = self.ar, self.ai
        e = seg_len
        while e:
            if e & 1:
                pr, pi = _cmul(pr, pi, br, bi)
            br, bi = _cmul(br, bi, br, bi)
            e >>= 1
        er, ei = self.st_ref[:, :half], self.st_ref[:, half:]
        zero = jnp.zeros((1, half), F32)
        order = range(S5_SEGS) if self.fwd else range(S5_SEGS - 1, -1, -1)
        cr, ci, out_r, out_i = zero, zero, {}, {}
        for j in order:
            out_r[j], out_i[j] = cr, ci
            nr, ni = _cmul(pr[0:1], pi[0:1], cr, ci)
            cr, ci = nr + er[j:j + 1], ni + ei[j:j + 1]
        self.st_ref[:, :half] = jnp.concatenate([out_r[j] for j in range(S5_SEGS)], axis=0)
        self.st_ref[:, half:] = jnp.concatenate([out_i[j] for j in range(S5_SEGS)], axis=0)


def _s5_body(tl, seg_len, ucf_ref, unf_ref, ucb_ref, unb_ref, perm_ref, permt_ref, bb_ref, a_ref, c_ref, yf_ref,
             yb_ref, baf_ref, bbf_ref, bab_ref, bbb_ref, st_ref):
    ps = pl.program_id(1)
    i = pl.program_id(2)
    dirs = [_S5Dir(True, tl, ucf_ref, unf_ref, yf_ref, baf_ref, bbf_ref, st_ref.at[0], perm_ref, permt_ref,
                   bb_ref.at[0], a_ref.at[0], c_ref.at[0]),
            _S5Dir(False, tl, ucb_ref, unb_ref, yb_ref, bab_ref, bbb_ref, st_ref.at[1], perm_ref, permt_ref,
                   bb_ref.at[1], a_ref.at[1], c_ref.at[1])]

    @pl.when(jnp.logical_and(i == 0, ps == 0))
    def _():
        st_ref[...] = jnp.zeros_like(st_ref)

    @pl.when(jnp.logical_and(i == 0, ps == 1))
    def _():
        for z in dirs:
            z.init_state(seg_len)

    @pl.when(i == 0)
    def _():
        _run_lockstep([z.project(z.ucur_ref, z.off_a, z.bua_ref) for z in dirs])

    @pl.when(ps == 0)
    def _():
        _run_lockstep([g for z in dirs for g in (z.scan(z.bua_ref, False), z.project(z.ucur_ref, z.off_b, z.bub_ref))])
        _run_lockstep([g for z in dirs for g in (z.scan(z.bub_ref, False), z.project(z.unext_ref, z.off_a, z.bua_ref))])

    @pl.when(ps == 1)
    def _():
        _run_lockstep([g for z in dirs for g in (z.scan(z.bua_ref, True), z.project(z.ucur_ref, z.off_b, z.bub_ref))])
        _run_lockstep([g for z in dirs for g in (z.scan(z.bub_ref, True), z.emit(z.bua_ref, z.off_a),
                                                 z.project(z.unext_ref, z.off_a, z.bua_ref))])
        _run_lockstep([z.emit(z.bub_ref, z.off_b) for z in dirs])


def _s5_params(lam_re, lam_im, log_dt, b_re, b_im, c_re, c_im):
    g, p, hch = SSM_GROUPS, SSM_STATE, SSM_CH
    dt = jnp.exp(log_dt)[..., None]
    mag = jnp.exp(lam_re * dt)
    lb_re = mag * jnp.cos(lam_im * dt)
    lb_im = mag * jnp.sin(lam_im * dt)
    den = lam_re * lam_re + lam_im * lam_im
    nr = lb_re - 1.0
    coef_re = (nr * lam_re + lb_im * lam_im) / den
    coef_im = (lb_im * lam_re - nr * lam_im) / den
    bb_re = coef_re[..., None] * b_re - coef_im[..., None] * b_im
    bb_im = coef_re[..., None] * b_im + coef_im[..., None] * b_re
    eye = jnp.eye(g, dtype=F32)
    bd_in = lambda m: jnp.einsum('dgph,gk->dghkp', m, eye).reshape(2, g * hch, g * p)
    bb = jnp.concatenate([bd_in(bb_re), bd_in(bb_im)], axis=-1)
    bd_out = lambda m: jnp.einsum('dghp,gk->dgpkh', m, eye).reshape(2, g * p, g * hch)
    cc = jnp.concatenate([bd_out(c_re), -bd_out(c_im)], axis=1)
    a = jnp.concatenate([lb_re.reshape(2, 1, g * p), lb_im.reshape(2, 1, g * p)], axis=-1)
    return bb, a, cc


def _s5_scan(p_ssm, bsz, seq, bb, a, cc, tl):
    seg_len = seq // S5_SEGS
    n_pairs = seg_len // (2 * tl)
    rows = tl * S5_SEGS
    w2 = 2 * SSM_GROUPS * SSM_STATE
    src = (np.arange(rows) % S5_SEGS) * tl + np.arange(rows) // S5_SEGS
    perm = np.zeros((rows, rows), np.float32)
    perm[np.arange(rows), src] = 1.0
    last = n_pairs - 1
    u4 = p_ssm.reshape(bsz, S5_SEGS, seg_len, W_SSM)
    ublk = (1, S5_SEGS, 2 * tl, W_SSM)
    const = lambda shape: pl.BlockSpec(shape, lambda b, ps, i: (0,) * len(shape))
    y_f, y_b = pl.pallas_call(
        functools.partial(_s5_body, tl, seg_len),
        grid=(bsz, 2, n_pairs),
        in_specs=[pl.BlockSpec(ublk, lambda b, ps, i: (b, 0, i, 0)),
                  pl.BlockSpec(ublk, lambda b, ps, i: (b, 0, jnp.minimum(i + 1, last), 0)),
                  pl.BlockSpec(ublk, lambda b, ps, i: (b, 0, last - i, 0)),
                  pl.BlockSpec(ublk, lambda b, ps, i: (b, 0, jnp.maximum(last - i - 1, 0), 0)),
                  const((rows, rows)), const((rows, rows)), const((2, W_SSM, w2)), const((2, 1, w2)),
                  const((2, w2, W_SSM))],
        out_specs=[pl.BlockSpec(ublk, lambda b, ps, i: (b, 0, jnp.where(ps == 0, 0, i), 0)),
                   pl.BlockSpec(ublk, lambda b, ps, i: (b, 0, jnp.where(ps == 0, last, last - i), 0))],
        out_shape=[jax.ShapeDtypeStruct((bsz, S5_SEGS, seg_len, W_SSM), BF16)] * 2,
        scratch_shapes=[pltpu.VMEM((rows, w2), F32)] * 4 + [pltpu.VMEM((2, S5_SEGS, w2), F32)],
        compiler_params=_cparams(("arbitrary",) * 3),
        name="s5_scan",
    )(u4, u4, u4, u4, jnp.asarray(perm).astype(BF16), jnp.asarray(perm.T).astype(BF16), bb.astype(BF16), a,
      cc.astype(BF16))
    return y_f.reshape(bsz * seq, W_SSM), y_b.reshape(bsz * seq, W_SSM)


def _gelu_tanh(x):
    c = math.sqrt(2.0 / math.pi)
    return 0.5 * x * (1.0 + jnp.tanh(c * (x + 0.044715 * (x * x * x))))


def _s5_post(y_f, y_b, u, dskip, glu_w, glu_b):
    y = _gelu_tanh(y_f.astype(F32) + y_b.astype(F32) + dskip * u)
    return y * jax.nn.sigmoid(_bdot(y, glu_w) + glu_b)


def _split_dot(x, w):
    hi = x.astype(BF16)
    lo = (x - hi.astype(F32)).astype(BF16)
    return (jnp.dot(hi, w, preferred_element_type=F32) + jnp.dot(lo, w, preferred_element_type=F32))


def _dot_nt(a, b):
    return lax.dot_general(a.astype(BF16), b.astype(BF16), (((1,), (1,)), ((), ())),
                           preferred_element_type=F32)


def _rwkv_masks(chunk):
    hh = RWKV_HEADS
    r = np.arange(hh * chunk)[:, None] // chunk
    bm_feat = (r == np.arange(W_RWKV)[None, :] // RWKV_HEAD).astype(np.float32)
    bm_time = (r == np.arange(hh * chunk)[None, :] // chunk).astype(np.float32)
    f = np.arange(W_RWKV)
    bm_head = (f[:, None] // RWKV_HEAD == f[None, :] // RWKV_HEAD).astype(np.float32)
    return bm_feat, bm_time, bm_head


def _rwkv_chunk(fwd, x, edge_row, logw, a, mu, k_k, k_a, r_k, bmf16, bmt16, bmh):
    ll = x.shape[0]
    w = W_RWKV
    rows = lax.broadcasted_iota(jnp.int32, x.shape, 0)
    if fwd:
        shifted = jnp.where(rows == 0, edge_row, pltpu.roll(x, 1, axis=0))
    else:
        shifted = jnp.where(rows == ll - 1, edge_row, pltpu.roll(x, ll - 1, axis=0))
    rkv = x + (shifted - x) * mu
    r, k, v = rkv[:, :w], rkv[:, w:2 * w], rkv[:, 2 * w:]
    bmh16 = bmh.astype(BF16)

    kk = k * k_k
    k2 = k * (1.0 + (a - 1.0) * k_a)
    both = jnp.concatenate([kk * kk, r * k2 * r_k], axis=0)
    hi = both.astype(BF16)
    lo = (both - hi.astype(F32)).astype(BF16)
    sums = jnp.dot(jnp.concatenate([hi, lo], axis=0), bmh16, preferred_element_type=F32)
    sums = sums[:2 * ll] + sums[2 * ll:]
    ksq = sums[:ll]
    bonus = sums[ll:] * v

    ti = lax.broadcasted_iota(jnp.int32, (ll, ll), 0)
    si = lax.broadcasted_iota(jnp.int32, (ll, ll), 1)
    tri = ((si <= ti) if fwd else (si >= ti)).astype(F32).astype(BF16)
    lhi = logw.astype(BF16)
    llo = (logw - lhi.astype(F32)).astype(BF16)
    cum2 = jnp.dot(tri, jnp.concatenate([lhi, llo], axis=1), preferred_element_type=F32)
    cum = cum2[:, :w] + cum2[:, w:]
    yield
    kk = kk * lax.rsqrt(ksq + 1e-12)
    ctot = jnp.sum(logw, axis=0, keepdims=True)
    e_neg = jnp.exp(-cum)
    e_rem = jnp.exp(ctot - cum)
    ah = -kk * jnp.exp(cum - logw)
    rh = r * jnp.exp(cum)
    bvec = kk * a
    bh, kh = bvec * e_neg, k2 * e_neg
    bt, kt = bvec * e_rem, k2 * e_rem

    def bd(m):
        m16 = m.astype(BF16)
        return jnp.concatenate([m16] * RWKV_HEADS, axis=0) * (bmf16 if m.shape[1] == w else bmt16)

    gram = _dot_nt(jnp.concatenate([ah, rh], axis=0), jnp.concatenate([bd(bh), bd(kh)], axis=0))
    yield
    l4 = RWKV_HEADS * ll
    tt = lax.broadcasted_iota(jnp.int32, (ll, l4), 0)
    ss = lax.broadcasted_iota(jnp.int32, (ll, l4), 1) % ll
    strict = (ss < tt) if fwd else (ss > tt)
    incl = (ss <= tt) if fwd else (ss >= tt)
    n_ab = jnp.where(strict, gram[:ll, :l4], 0.0)
    a_ak = jnp.where(strict, gram[:ll, l4:], 0.0)
    a_rb = jnp.where(incl, gram[ll:, :l4], 0.0)
    a_rk = jnp.where(incl, gram[ll:, l4:], 0.0)

    pw = n_ab
    tinv = jnp.where(ss == tt, 1.0, 0.0) + n_ab
    akv = _bdot(a_ak, bd(v))
    for _ in range(int(math.log2(ll)) - 1):
        pw = _bdot(pw, bd(pw))
        yield
        tinv = tinv + _bdot(tinv, bd(pw))
        yield

    ta = _bdot(tinv, jnp.concatenate([bd(ah), bd(akv)], axis=1))
    yield
    ap, wm = ta[:, :w], ta[:, w:]
    rp = rh + _bdot(a_rb, bd(ap))
    y0 = _bdot(jnp.concatenate([a_rb, a_rk], axis=1), jnp.concatenate([bd(wm), bd(v)], axis=0))
    btk = jnp.transpose(jnp.concatenate([bt, kt], axis=0))
    hp = RWKV_PAIR
    pqs = []
    for cs in (slice(0, hp), slice(hp, w)):
        rhs = jnp.concatenate([jnp.concatenate([ap[:, cs], wm[:, cs]], axis=1),
                               jnp.concatenate([jnp.zeros_like(v[:, cs]), v[:, cs]], axis=1)], axis=0)
        pqs.append(_bdot(btk[cs, :], rhs))
    yield
    eye = (lax.broadcasted_iota(jnp.int32, (hp, hp), 0) == lax.broadcasted_iota(jnp.int32, (hp, hp), 1))
    bmp = bmh[:hp, :hp]
    decay = jnp.exp(ctot)
    pm = [pq[:, :hp] * bmp + jnp.where(eye, decay[:, cs], 0.0) for pq, cs in zip(pqs, (slice(0, hp), slice(hp, w)))]
    qm = [pq[:, hp:] * bmp for pq in pqs]
    yield rp, y0, pm, qm, bonus


def _run_lockstep(gens):
    results = [None] * len(gens)
    live = list(range(len(gens)))
    while live:
        for s in list(live):
            try:
                out = next(gens[s])
            except StopIteration:
                live.remove(s)
            else:
                if out is not None:
                    results[s] = out
    return results


def _rwkv_body(chunk, nsub, nb, *refs):
    dir_refs = [refs[0:4], refs[4:8]]
    mu_ref, kk_ref, ka_ref, rk_ref, bmf_ref, bmt_ref, bmh_ref = refs[8:15]
    y_refs, bonus_refs, h_ref = refs[15:17], refs[17:19], refs[19]
    c = pl.program_id(0)

    @pl.when(c == 0)
    def _():
        h_ref[...] = jnp.zeros_like(h_ref)

    gens, where = [], []
    for d in range(2):
        x_ref, edge_ref, logw_ref, a_ref = dir_refs[d]
        order = range(nsub) if d == 0 else range(nsub - 1, -1, -1)
        for b in range(nb):
            x = x_ref[b].astype(F32)
            for n, j in enumerate(order):
                rows = slice(j * chunk, (j + 1) * chunk)
                if n == 0:
                    hb = edge_ref.shape[1]
                    edge = edge_ref[b, hb - 1:hb, :] if d == 0 else edge_ref[b, 0:1, :]
                    edge = jnp.where(c == 0, 0.0, edge.astype(F32))
                else:
                    e = j * chunk - 1 if d == 0 else (j + 1) * chunk
                    edge = x[e:e + 1, :]
                gens.append(_rwkv_chunk(d == 0, x[rows], edge, logw_ref[0, b, rows, :],
                                        a_ref[0, b, rows, :].astype(F32), mu_ref[d], kk_ref[...], ka_ref[...],
                                        rk_ref[...], bmf_ref[...], bmt_ref[...], bmh_ref[...]))
                where.append((d, b, rows))
    parts = _run_lockstep(gens)

    def carry(s):
        hp = RWKV_PAIR
        h = [h_ref[s, 0], h_ref[s, 1]]
        for n in range(nsub):
            rp, y0, pm, qm, bonus = parts[s * nsub + n]
            d, b, rows = where[s * nsub + n]
            y = jnp.concatenate([_bdot(rp[:, :hp], h[0]), _bdot(rp[:, hp:], h[1])], axis=1) + y0
            y_refs[d][b, rows, :] = y.astype(y_refs[d].dtype)
            bonus_refs[d][b, rows, :] = bonus.astype(bonus_refs[d].dtype)
            h = [_bdot(pm[0], h[0]) + qm[0], _bdot(pm[1], h[1]) + qm[1]]
            yield
        h_ref[s, 0] = h[0]
        h_ref[s, 1] = h[1]

    _run_lockstep([carry(s) for s in range(2 * nb)])


def _rwkv_scan(p_rkv, logw, agate, bsz, seq, chunk, nsub, mu_rkv, k_k, k_a, r_k):
    n = bsz * seq
    blk = chunk * nsub
    n_chunks = seq // blk
    hb = PACKED_ROWS
    per = blk // hb
    w3 = 3 * W_RWKV
    bmf, bmt, bmh = _rwkv_masks(chunk)
    consts = (mu_rkv.reshape(2, 1, w3), k_k.reshape(1, -1), k_a.reshape(1, -1), r_k.reshape(1, -1),
              jnp.asarray(bmf).astype(BF16), jnp.asarray(bmt).astype(BF16), jnp.asarray(bmh))
    x3 = p_rkv.reshape(bsz, seq, w3)
    logw4 = logw.reshape(2, bsz, seq, W_RWKV)
    a4 = agate.reshape(2, bsz, seq, W_RWKV)
    args, in_specs, out_specs = [], [], []
    for d in range(2):
        cidx = (lambda c: c) if d == 0 else (lambda c: n_chunks - 1 - c)
        if d == 0:
            edge = lambda c: (0, jnp.maximum(c * per - 1, 0), 0)
        else:
            edge = lambda c: (0, jnp.minimum((n_chunks - c) * per, seq // hb - 1), 0)
        args += [x3, x3, logw4, a4]
        in_specs += [pl.BlockSpec((bsz, blk, w3), lambda c, cidx=cidx: (0, cidx(c), 0)),
                     pl.BlockSpec((bsz, hb, w3), edge),
                     pl.BlockSpec((1, bsz, blk, W_RWKV), lambda c, d=d, cidx=cidx: (d, 0, cidx(c), 0)),
                     pl.BlockSpec((1, bsz, blk, W_RWKV), lambda c, d=d, cidx=cidx: (d, 0, cidx(c), 0))]
        out_specs.append(pl.BlockSpec((bsz, blk, W_RWKV), lambda c, cidx=cidx: (0, cidx(c), 0)))

    def full(a):
        nd = a.ndim
        return pl.BlockSpec(a.shape, lambda c: (0,) * nd)

    y_f, y_b, bonus_f, bonus_b = pl.pallas_call(
        functools.partial(_rwkv_body, chunk, nsub, bsz),
        grid=(n_chunks,),
        in_specs=in_specs + [full(a) for a in consts],
        out_specs=out_specs + out_specs,
        out_shape=[jax.ShapeDtypeStruct((bsz, seq, W_RWKV), BF16)] * 4,
        scratch_shapes=[pltpu.VMEM((2 * bsz, W_RWKV // RWKV_PAIR, RWKV_PAIR, RWKV_PAIR), F32)],
        compiler_params=_cparams(("arbitrary",)),
        name="rwkv_scan",
    )(*args, *consts)
    return [a.reshape(n, W_RWKV) for a in (y_f, y_b, bonus_f, bonus_b)]


def _head_mean(x, bmh16):
    return _split_dot(x, bmh16) * (1.0 / RWKV_HEAD)


def _rwkv_post(y, bonus, g, gn_g, gn_b, bmh16):
    mu = _head_mean(y, bmh16)
    yc = y - mu
    var = _head_mean(yc * yc, bmh16)
    yn = yc * lax.rsqrt(var + RWKV_GN_EPS) * gn_g + gn_b
    return (yn + bonus) * g


FFT_R1 = 64


def _fft_tables(seq):
    r1, r2 = FFT_R1, seq // FFT_R1
    i1 = jnp.arange(r1, dtype=jnp.int32)
    ang1 = ((i1[:, None] * i1[None, :]) % r1).astype(F32) * (2.0 * math.pi / r1)
    stage1 = jnp.concatenate([jnp.cos(ang1), -jnp.sin(ang1)], axis=0)
    k1 = i1[:, None, None]
    k2 = jnp.arange(r2, dtype=jnp.int32)[None, :, None]
    n2 = jnp.arange(r2, dtype=jnp.int32)[None, None, :]
    ang2 = ((n2 * (k1 + r1 * k2)) % seq).astype(F32) * (2.0 * math.pi / seq)
    mr, mi = jnp.cos(ang2), -jnp.sin(ang2)
    stage2 = jnp.concatenate([jnp.concatenate([mr, -mi], axis=2),
                              jnp.concatenate([mi, mr], axis=2)], axis=1)
    c = jnp.arange(W_FFT, dtype=jnp.int32)
    same = (c[:, None] // FFT_CH) == (c[None, :] // FFT_CH)
    angc = ((c[:, None] * c[None, :]) % FFT_CH).astype(F32) * (2.0 * math.pi / FFT_CH)
    scale = 1.0 / math.sqrt(seq * FFT_CH)
    chan = jnp.concatenate([jnp.where(same, jnp.cos(angc), 0.0), jnp.where(same, jnp.sin(angc), 0.0)],
                           axis=0) * scale
    return stage1, stage2, chan


def _fft1_body(z_ref, m_ref, are_ref, aim_ref):
    r1 = FFT_R1
    a = _bdot(m_ref[...], z_ref[0])
    are_ref[0] = a[:r1].astype(are_ref.dtype)
    aim_ref[0] = a[r1:].astype(aim_ref.dtype)


def _fft2_body(kb, r2, are_ref, aim_ref, m_ref, chan_ref, o_ref):
    for j in range(kb):
        x = jnp.concatenate([are_ref[0, j], aim_ref[0, j]], axis=0)
        f = _bdot(m_ref[j], x)
        fri = jnp.concatenate([f[:r2], f[r2:]], axis=1)
        o_ref[0, :, j * W_FFT:(j + 1) * W_FFT] = _bdot(fri, chan_ref[...]).astype(o_ref.dtype)


def _fourier(p_fft, bsz, seq, tables=None):
    r1, r2 = FFT_R1, seq // FFT_R1
    stage1, stage2, chan = tables if tables is not None else _fft_tables(seq)
    cols = r2 * W_FFT
    tc = min(cols, 4096)
    z = p_fft.reshape(bsz, r1, cols)
    blk = pl.BlockSpec((1, r1, tc), lambda b, j: (b, 0, j))
    a_re, a_im = pl.pallas_call(
        _fft1_body,
        grid=(bsz, cols // tc),
        in_specs=[blk, pl.BlockSpec((2 * r1, r1), lambda b, j: (0, 0))],
        out_specs=[blk, blk],
        out_shape=[jax.ShapeDtypeStruct((bsz, r1, cols), BF16)] * 2,
        compiler_params=_cparams(("parallel", "parallel")),
        name="fft_stage1",
    )(z, stage1.astype(BF16))
    kb = 8
    ablk = pl.BlockSpec((1, kb, r2, W_FFT), lambda b, j: (b, j, 0, 0))
    out = pl.pallas_call(
        functools.partial(_fft2_body, kb, r2),
        grid=(bsz, r1 // kb),
        in_specs=[ablk, ablk,
                  pl.BlockSpec((kb, 2 * r2, 2 * r2), lambda b, j: (j, 0, 0)),
                  pl.BlockSpec((2 * W_FFT, W_FFT), lambda b, j: (0, 0))],
        out_specs=pl.BlockSpec((1, r2, kb * W_FFT), lambda b, j: (b, 0, j)),
        out_shape=jax.ShapeDtypeStruct((bsz, r2, r1 * W_FFT), BF16),
        compiler_params=_cparams(("parallel", "parallel")),
        name="fft_stage2",
    )(a_re.reshape(bsz, r1, r2, W_FFT), a_im.reshape(bsz, r1, r2, W_FFT), stage2.astype(BF16), chan.astype(BF16))
    return out.reshape(bsz * seq, W_FFT)


def _mixout_body(seq, tile, h_ref, ysf_ref, ysb_ref, pssm_ref, yrf_ref, yrb_ref, bnf_ref, bnb_ref, g_ref, pc_ref,
                 pcp_ref, pcn_ref, yf_ref, wout_ref, dskip_ref, gluw_ref, glub_ref, gng_ref, gnb_ref, convw_ref,
                 bmh_ref, lng_ref, lnb_ref, o_ref):
    i = pl.program_id(0)
    f32 = lambda ref: ref[...].astype(F32)
    y_a = _s5_post(f32(ysf_ref), f32(ysb_ref), f32(pssm_ref), dskip_ref[...], gluw_ref[...], glub_ref[...])
    y_b = _rwkv_post(f32(yrf_ref) + f32(yrb_ref), f32(bnf_ref) + f32(bnb_ref), f32(g_ref), gng_ref[...],
                     gnb_ref[...], bmh_ref[...])
    pc = f32(pc_ref)
    wc = W_CONV
    prev_row, next_row = _halo_rows(pcp_ref, pcn_ref, i * tile, tile, seq)
    z = pc[:, wc:2 * wc] * pc[:, 2 * wc:]
    z_prev, z_next = _shift_rows(z, prev_row[:, wc:2 * wc] * prev_row[:, 2 * wc:],
                                 next_row[:, wc:2 * wc] * next_row[:, 2 * wc:])
    y_c = pc[:, :wc] * (convw_ref[0:1, :] * z_prev + convw_ref[1:2, :] * z + convw_ref[2:3, :] * z_next)
    mix = (_bdot(y_a, wout_ref[0:W_SSM, :]) + _bdot(y_b, wout_ref[W_SSM:W_SSM + W_RWKV, :])
           + _bdot(y_c, wout_ref[W_SSM + W_RWKV:W_SSM + W_RWKV + W_CONV, :])
           + _bdot(yf_ref[...], wout_ref[W_SSM + W_RWKV + W_CONV:, :]))
    o_ref[...] = _layer_norm(DEEPNORM_ALPHA * h_ref[...] + mix, lng_ref[...], lnb_ref[...])


def _mixout(h, seq, tile, y_s5, p_ssm, rwkv_outs, g, p_conv, y_fft, w_out, dskip, glu_w, glu_b, gn_g, gn_b,
            conv_w, ln_g, ln_b):
    n, d = h.shape
    row = lambda w: pl.BlockSpec((tile, w), lambda i: (i, 0))
    pcp, pcn = _halo_specs(tile, 3 * W_CONV, n, PACKED_ROWS)
    bmh16 = jnp.asarray(_rwkv_masks(RWKV_HEAD)[2]).astype(BF16)
    consts = (w_out.astype(BF16), dskip.reshape(1, -1), glu_w.astype(BF16), glu_b.reshape(1, -1),
              gn_g.reshape(1, -1), gn_b.reshape(1, -1), conv_w, bmh16, ln_g.reshape(1, -1), ln_b.reshape(1, -1))

    def full(a):
        nd = a.ndim
        return pl.BlockSpec(a.shape, lambda i: (0,) * nd)

    return pl.pallas_call(
        functools.partial(_mixout_body, seq, tile),
        grid=(n // tile,),
        in_specs=[row(d)] + [row(W_SSM)] * 3 + [row(W_RWKV)] * 5
                 + [row(3 * W_CONV), pcp, pcn, row(W_FFT)] + [full(c) for c in consts],
        out_specs=row(d),
        out_shape=jax.ShapeDtypeStruct((n, d), F32),
        compiler_params=_cparams(("parallel",)),
        name="mix_out",
    )(h, *y_s5, p_ssm, *rwkv_outs, g, p_conv, p_conv, p_conv, y_fft, *consts)


def _ffn_body(cols, h_ref, w1_ref, w3_ref, w2_ref, lng_ref, lnb_ref, o_ref, u_ref):
    x = h_ref[...].astype(BF16)
    dff = w1_ref.shape[1]
    for c0 in range(0, dff, cols):
        cs = slice(c0, min(c0 + cols, dff))
        a1 = jnp.dot(x, w1_ref[:, cs], preferred_element_type=F32)
        a3 = jnp.dot(x, w3_ref[:, cs], preferred_element_type=F32)
        u_ref[:, cs] = (jax.nn.silu(a1) * a3).astype(BF16)
    f = jnp.dot(u_ref[...], w2_ref[...], preferred_element_type=F32)
    o_ref[...] = _layer_norm(DEEPNORM_ALPHA * h_ref[...] + f, lng_ref[...], lnb_ref[...])


def _ffn(h, tile, cols, w1, w3, w2, ln_g, ln_b):
    n, d = h.shape
    dff = w1.shape[1]
    once = lambda shape: pl.BlockSpec(shape, lambda i: (0, 0), pipeline_mode=pl.Buffered(1))
    return pl.pallas_call(
        functools.partial(_ffn_body, cols),
        grid=(n // tile,),
        in_specs=[pl.BlockSpec((tile, d), lambda i: (i, 0)), once((d, dff)), once((d, dff)), once((dff, d)),
                  once((1, d)), once((1, d))],
        out_specs=pl.BlockSpec((tile, d), lambda i: (i, 0)),
        out_shape=jax.ShapeDtypeStruct((n, d), F32),
        scratch_shapes=[pltpu.VMEM((tile, dff), BF16)],
        compiler_params=_cparams(("parallel",)),
        name="ffn",
    )(h, w1.astype(BF16), w3.astype(BF16), w2.astype(BF16), ln_g.reshape(1, d), ln_b.reshape(1, d))


ROW_TILE = 512
S5_TILE_STEPS = 64
RWKV_CHUNK = 64
RWKV_CHUNKS_PER_STEP = 4
FFN_COLS = 256


def kernel(x, ln0_g, ln0_b, w_in, s5_lambda_re, s5_lambda_im, s5_log_dt, s5_b_re, s5_b_im, s5_c_re, s5_c_im, s5_d,
           s5_glu_w, s5_glu_b, rwkv_mu_rkv, rwkv_mu_w, rwkv_mu_a, rwkv_mu_g, rwkv_w0, rwkv_w1, rwkv_w2, rwkv_a0,
           rwkv_a1, rwkv_a2, rwkv_g1, rwkv_g2, rwkv_k_k, rwkv_k_a, rwkv_r_k, rwkv_gn_g, rwkv_gn_b, conv_w, w_out,
           ln1_g, ln1_b, ffn_w1, ffn_w3, ffn_w2, ln2_g, ln2_b):
    bsz, seq, d = x.shape
    n = bsz * seq
    tile = min(ROW_TILE, seq)
    s5_tl = min(S5_TILE_STEPS, seq // S5_SEGS // 2)
    fft_tables = _fft_tables(seq)
    h = _ln0(x.reshape(n, d), ln0_g, ln0_b, tile)
    for l in range(w_in.shape[0]):
        p_ssm, p_rkv, p_conv, p_fft, logw, agate, g = _inproj(
            h, seq, tile, w_in[l], rwkv_mu_w[l], rwkv_mu_a[l], rwkv_mu_g[l], rwkv_w1[l], rwkv_w2[l], rwkv_w0[l],
            rwkv_a1[l], rwkv_a2[l], rwkv_a0[l], rwkv_g1[l], rwkv_g2[l])
        bb, a, cc = _s5_params(s5_lambda_re[l], s5_lambda_im[l], s5_log_dt[l], s5_b_re[l], s5_b_im[l],
                               s5_c_re[l], s5_c_im[l])
        y_s5 = _s5_scan(p_ssm, bsz, seq, bb, a, cc, s5_tl)
        rwkv_outs = _rwkv_scan(p_rkv, logw, agate, bsz, seq, RWKV_CHUNK, RWKV_CHUNKS_PER_STEP, rwkv_mu_rkv[l], rwkv_k_k[l],
                               rwkv_k_a[l], rwkv_r_k[l].reshape(-1))
        y_fft = _fourier(p_fft, bsz, seq, fft_tables)
        h = _mixout(h, seq, tile, y_s5, p_ssm, rwkv_outs, g, p_conv, y_fft, w_out[l], s5_d[l], s5_glu_w[l],
                    s5_glu_b[l], rwkv_gn_g[l], rwkv_gn_b[l], conv_w[l], ln1_g[l], ln1_b[l])
        h = _ffn(h, tile, FFN_COLS, ffn_w1[l], ffn_w3[l], ffn_w2[l], ln2_g[l], ln2_b[l])
    return h.reshape(bsz, seq, d)
```

```python
import functools
import math

import jax
import jax.numpy as jnp
import numpy as np
from jax import lax
from jax.experimental import pallas as pl
from jax.experimental.pallas import tpu as pltpu

W_SSM = 256
W_RWKV = 256
W_CONV = 256
W_FFT = 256
SSM_CH = 16
SSM_GROUPS = 16
SSM_STATE = 64
RWKV_HEAD = 64
RWKV_HEADS = 4
RWKV_PAIR = 2 * RWKV_HEAD
FFT_GROUPS = 4
FFT_CH = 64
RWKV_DECAY_SCALE = math.exp(-0.5)
RWKV_GN_EPS = 64e-5
LN_EPS = 1e-5
DEPTH = 2
DEEPNORM_ALPHA = (2 * DEPTH) ** 0.25

SUBLANES = 8
PACKED_ROWS = 16
VMEM_LIMIT = 48 * 1024 * 1024

BF16 = jnp.bfloat16
F32 = jnp.float32


def _cparams(sem):
    return pltpu.CompilerParams(dimension_semantics=sem, vmem_limit_bytes=VMEM_LIMIT)


def _bdot(a, b):
    return jnp.dot(a.astype(BF16), b.astype(BF16), preferred_element_type=F32)


def _layer_norm(x, g, b):
    mu = jnp.mean(x, axis=-1, keepdims=True)
    xc = x - mu
    var = jnp.mean(xc * xc, axis=-1, keepdims=True)
    return xc * lax.rsqrt(var + LN_EPS) * g + b


def _shift_rows(x, prev_row, next_row):
    n = x.shape[0]
    rows = lax.broadcasted_iota(jnp.int32, x.shape, 0)
    x_prev = jnp.where(rows == 0, prev_row, pltpu.roll(x, 1, axis=0))
    x_next = jnp.where(rows == n - 1, next_row, pltpu.roll(x, n - 1, axis=0))
    return x_prev, x_next


def _halo_rows(prev_ref, next_ref, row0, n_rows, seq):
    first = (row0 % seq) == 0
    last = ((row0 + n_rows) % seq) == 0
    hb = prev_ref.shape[0]
    prev_row = jnp.where(first, 0.0, prev_ref[hb - 1:hb, :].astype(F32))
    next_row = jnp.where(last, 0.0, next_ref[0:1, :].astype(F32))
    return prev_row, next_row


def _layer_spec(stacked, l, **kw):
    return pl.BlockSpec((None,) + stacked.shape[1:], lambda i: (l, 0, 0), **kw)


def _halo_specs(tile, width, n_rows_total, hb):
    per = tile // hb
    last_blk = n_rows_total // hb - 1
    prev = pl.BlockSpec((hb, width), lambda i: (jnp.maximum(i * per - 1, 0), 0))
    nxt = pl.BlockSpec((hb, width), lambda i: (jnp.minimum((i + 1) * per, last_blk), 0))
    return prev, nxt


def _ln0_body(x_ref, g_ref, b_ref, o_ref):
    o_ref[...] = _layer_norm(x_ref[...], g_ref[...], b_ref[...])


def _ln0(x2, g, b, tile):
    n, d = x2.shape
    return pl.pallas_call(
        _ln0_body,
        grid=(n // tile,),
        in_specs=[pl.BlockSpec((tile, d), lambda i: (i, 0)),
                  pl.BlockSpec((1, d), lambda i: (0, 0)),
                  pl.BlockSpec((1, d), lambda i: (0, 0))],
        out_specs=pl.BlockSpec((tile, d), lambda i: (i, 0)),
        out_shape=jax.ShapeDtypeStruct((n, d), F32),
        compiler_params=_cparams(("parallel",)),
        name="ln0",
    )(x2, g.reshape(1, d), b.reshape(1, d))


def _inproj_body(seq, tile, h_ref, hp_ref, hn_ref, win_ref, muw_ref, mua_ref, mug_ref,
                 w1_ref, w2_ref, w0_ref, a1_ref, a2_ref, a0_ref, g1_ref, g2_ref,
                 pssm_ref, prkv_ref, pconv_ref, pfft_ref, logw_ref, agate_ref, g_ref):
    i = pl.program_id(0)
    h = h_ref[...]
    prev_row, next_row = _halo_rows(hp_ref, hn_ref, i * tile, tile, seq)
    x_prev, x_next = _shift_rows(h, prev_row, next_row)

    h16 = h.astype(BF16)

    def main():
        col = 0
        for ref in (pssm_ref, prkv_ref, pconv_ref, pfft_ref):
            width = ref.shape[1]
            ref[...] = jnp.dot(h16, win_ref[:, col:col + width], preferred_element_type=F32).astype(ref.dtype)
            col += width
            yield

    def lora(d, x_sh):
        dx = x_sh - h
        t1 = _bdot(h + dx * muw_ref[d:d + 1, :], w1_ref[d])
        yield
        w_lora = _bdot(jnp.tanh(t1), w2_ref[d])
        t2 = _bdot(h + dx * mua_ref[d:d + 1, :], a1_ref[d])
        yield
        logw_ref[d] = -RWKV_DECAY_SCALE * jax.nn.sigmoid(w0_ref[d:d + 1, :] + w_lora)
        a_lora = _bdot(t2, a2_ref[d])
        yield
        agate_ref[d] = jax.nn.sigmoid(a0_ref[d:d + 1, :] + a_lora).astype(agate_ref.dtype)

    def gate():
        xg = h + (0.5 * (x_prev + x_next) - h) * mug_ref[...]
        t3 = _bdot(xg, g1_ref[...])
        yield
        g_ref[...] = _bdot(jax.nn.sigmoid(t3), g2_ref[...]).astype(g_ref.dtype)

    _run_lockstep([main(), lora(0, x_prev), lora(1, x_next), gate()])


def _inproj(h, seq, tile, w_in_all, l, mu_w, mu_a, mu_g, w1, w2, w0, a1, a2, a0, g1, g2):
    n, d = h.shape
    hp_spec, hn_spec = _halo_specs(tile, d, n, SUBLANES)

    def full(a):
        nd = a.ndim
        return pl.BlockSpec(a.shape, lambda i: (0,) * nd)

    row = lambda w: pl.BlockSpec((tile, w), lambda i: (i, 0))
    row2 = lambda w: pl.BlockSpec((2, tile, w), lambda i: (0, i, 0))
    consts = (mu_w, mu_a, mu_g.reshape(1, d), w1.astype(BF16), w2.astype(BF16), w0,
              a1.astype(BF16), a2.astype(BF16), a0, g1.astype(BF16), g2.astype(BF16))
    return pl.pallas_call(
        functools.partial(_inproj_body, seq, tile),
        grid=(n // tile,),
        in_specs=[pl.BlockSpec((tile, d), lambda i: (i, 0)), hp_spec, hn_spec, _layer_spec(w_in_all, l)]
                 + [full(c) for c in consts],
        out_specs=[row(W_SSM), row(3 * W_RWKV), row(3 * W_CONV), row(W_FFT),
                   row2(W_RWKV), row2(W_RWKV), row(W_RWKV)],
        out_shape=[jax.ShapeDtypeStruct((n, W_SSM), BF16),
                   jax.ShapeDtypeStruct((n, 3 * W_RWKV), BF16),
                   jax.ShapeDtypeStruct((n, 3 * W_CONV), BF16),
                   jax.ShapeDtypeStruct((n, W_FFT), F32),
                   jax.ShapeDtypeStruct((2, n, W_RWKV), F32),
                   jax.ShapeDtypeStruct((2, n, W_RWKV), BF16),
                   jax.ShapeDtypeStruct((n, W_RWKV), BF16)],
        compiler_params=_cparams(("parallel",)),
        name="inproj",
    )(h, h, h, w_in_all, *consts)


S5_SEGS = SUBLANES


def _cmul(ar, ai, br, bi):
    return ar * br - ai * bi, ar * bi + ai * br


class _S5Dir:
    def __init__(self, fwd, tl, ucur_ref, unext_ref, y_ref, bua_ref, bub_ref, st_ref, perm_ref, permt_ref, bb_ref,
                 a_ref, c_ref):
        self.fwd, self.tl = fwd, tl
        self.ucur_ref, self.unext_ref, self.y_ref = ucur_ref, unext_ref, y_ref
        self.bua_ref, self.bub_ref, self.st_ref = bua_ref, bub_ref, st_ref
        self.perm_ref, self.permt_ref, self.bb_ref, self.c_ref = perm_ref, permt_ref, bb_ref, c_ref
        self.half = SSM_GROUPS * SSM_STATE
        self.rows = tl * S5_SEGS
        self.n_parts = S5_SEGS
        self.part = 2 * self.half // self.n_parts
        shape = (S5_SEGS, self.half)
        self.ar = jnp.broadcast_to(a_ref[:, :self.half], shape)
        self.ai = jnp.broadcast_to(a_ref[:, self.half:], shape)
        self.off_a, self.off_b = (0, tl) if fwd else (tl, 0)

    def project(self, u_ref, off, buf_ref):
        u = u_ref[0, :, off:off + self.tl, :].reshape(self.rows, W_SSM).astype(BF16)
        up = jnp.dot(self.perm_ref[...], u, preferred_element_type=F32).astype(BF16)
        for k in range(self.n_parts):
            cols = slice(k * self.part, (k + 1) * self.part)
            buf_ref[:, cols] = jnp.dot(up, self.bb_ref[:, cols], preferred_element_type=F32)
            yield

    def scan(self, buf_ref, store):
        half, tl = self.half, self.tl
        xr, xi = self.st_ref[:, :half], self.st_ref[:, half:]
        for t in range(tl):
            row = (t if self.fwd else tl - 1 - t) * S5_SEGS
            nr = self.ar * xr - self.ai * xi + buf_ref[row:row + S5_SEGS, :half]
            ni = self.ar * xi + self.ai * xr + buf_ref[row:row + S5_SEGS, half:]
            if store:
                buf_ref[row:row + S5_SEGS, :half] = nr
                buf_ref[row:row + S5_SEGS, half:] = ni
            xr, xi = nr, ni
            if (t + 1) % (tl // self.n_parts) == 0:
                yield
        self.st_ref[:, :half] = xr
        self.st_ref[:, half:] = xi

    def emit(self, buf_ref, off):
        y = None
        for k in range(self.n_parts):
            cols = slice(k * self.part, (k + 1) * self.part)
            yk = _bdot(buf_ref[:, cols], self.c_ref[cols, :])
            y = yk if y is None else y + yk
            yield
        y = jnp.dot(self.permt_ref[...], y.astype(BF16), preferred_element_type=F32)
        self.y_ref[0, :, off:off + self.tl, :] = y.reshape(S5_SEGS, self.tl, W_SSM).astype(self.y_ref.dtype)

    def init_state(self, seg_len):
        half = self.half
        pr, pi = jnp.ones_like(self.ar), jnp.zeros_like(self.ar)
        br, bi = self.ar, self.ai
        e = seg_len
        while e:
            if e & 1:
                pr, pi = _cmul(pr, pi, br, bi)
            br, bi = _cmul(br, bi, br, bi)
            e >>= 1
        er, ei = self.st_ref[:, :half], self.st_ref[:, half:]
        zero = jnp.zeros((1, half), F32)
        order = range(S5_SEGS) if self.fwd else range(S5_SEGS - 1, -1, -1)
        cr, ci, out_r, out_i = zero, zero, {}, {}
        for j in order:
            out_r[j], out_i[j] = cr, ci
            nr, ni = _cmul(pr[0:1], pi[0:1], cr, ci)
            cr, ci = nr + er[j:j + 1], ni + ei[j:j + 1]
        self.st_ref[:, :half] = jnp.concatenate([out_r[j] for j in range(S5_SEGS)], axis=0)
        self.st_ref[:, half:] = jnp.concatenate([out_i[j] for j in range(S5_SEGS)], axis=0)


def _s5_body(tl, seg_len, ucf_ref, unf_ref, ucb_ref, unb_ref, perm_ref, permt_ref, bb_ref, a_ref, c_ref, yf_ref,
             yb_ref, baf_ref, bbf_ref, bab_ref, bbb_ref, st_ref):
    ps = pl.program_id(1)
    i = pl.program_id(2)
    dirs = [_S5Dir(True, tl, ucf_ref, unf_ref, yf_ref, baf_ref, bbf_ref, st_ref.at[0], perm_ref, permt_ref,
                   bb_ref.at[0], a_ref.at[0], c_ref.at[0]),
            _S5Dir(False, tl, ucb_ref, unb_ref, yb_ref, bab_ref, bbb_ref, st_ref.at[1], perm_ref, permt_ref,
                   bb_ref.at[1], a_ref.at[1], c_ref.at[1])]

    @pl.when(jnp.logical_and(i == 0, ps == 0))
    def _():
        st_ref[...] = jnp.zeros_like(st_ref)

    @pl.when(jnp.logical_and(i == 0, ps == 1))
    def _():
        for z in dirs:
            z.init_state(seg_len)

    @pl.when(i == 0)
    def _():
        _run_lockstep([z.project(z.ucur_ref, z.off_a, z.bua_ref) for z in dirs])

    @pl.when(ps == 0)
    def _():
        _run_lockstep([g for z in dirs for g in (z.scan(z.bua_ref, False), z.project(z.ucur_ref, z.off_b, z.bub_ref))])
        _run_lockstep([g for z in dirs for g in (z.scan(z.bub_ref, False), z.project(z.unext_ref, z.off_a, z.bua_ref))])

    @pl.when(ps == 1)
    def _():
        _run_lockstep([g for z in dirs for g in (z.scan(z.bua_ref, True), z.project(z.ucur_ref, z.off_b, z.bub_ref))])
        _run_lockstep([g for z in dirs for g in (z.scan(z.bub_ref, True), z.emit(z.bua_ref, z.off_a),
                                                 z.project(z.unext_ref, z.off_a, z.bua_ref))])
        _run_lockstep([z.emit(z.bub_ref, z.off_b) for z in dirs])


def _s5_params(lam_re, lam_im, log_dt, b_re, b_im, c_re, c_im):
    g, p, hch = SSM_GROUPS, SSM_STATE, SSM_CH
    dt = jnp.exp(log_dt)[..., None]
    mag = jnp.exp(lam_re * dt)
    lb_re = mag * jnp.cos(lam_im * dt)
    lb_im = mag * jnp.sin(lam_im * dt)
    den = lam_re * lam_re + lam_im * lam_im
    nr = lb_re - 1.0
    coef_re = (nr * lam_re + lb_im * lam_im) / den
    coef_im = (lb_im * lam_re - nr * lam_im) / den
    bb_re = coef_re[..., None] * b_re - coef_im[..., None] * b_im
    bb_im = coef_re[..., None] * b_im + coef_im[..., None] * b_re
    eye = jnp.eye(g, dtype=F32)
    bd_in = lambda m: jnp.einsum('dgph,gk->dghkp', m, eye).reshape(2, g * hch, g * p)
    bb = jnp.concatenate([bd_in(bb_re), bd_in(bb_im)], axis=-1)
    bd_out = lambda m: jnp.einsum('dghp,gk->dgpkh', m, eye).reshape(2, g * p, g * hch)
    cc = jnp.concatenate([bd_out(c_re), -bd_out(c_im)], axis=1)
    a = jnp.concatenate([lb_re.reshape(2, 1, g * p), lb_im.reshape(2, 1, g * p)], axis=-1)
    return bb, a, cc


def _s5_scan(p_ssm, bsz, seq, bb, a, cc, tl):
    seg_len = seq // S5_SEGS
    n_pairs = seg_len // (2 * tl)
    rows = tl * S5_SEGS
    w2 = 2 * SSM_GROUPS * SSM_STATE
    src = (np.arange(rows) % S5_SEGS) * tl + np.arange(rows) // S5_SEGS
    perm = np.zeros((rows, rows), np.float32)
    perm[np.arange(rows), src] = 1.0
    last = n_pairs - 1
    u4 = p_ssm.reshape(bsz, S5_SEGS, seg_len, W_SSM)
    ublk = (1, S5_SEGS, 2 * tl, W_SSM)
    const = lambda shape: pl.BlockSpec(shape, lambda b, ps, i: (0,) * len(shape))
    y_f, y_b = pl.pallas_call(
        functools.partial(_s5_body, tl, seg_len),
        grid=(bsz, 2, n_pairs),
        in_specs=[pl.BlockSpec(ublk, lambda b, ps, i: (b, 0, i, 0)),
                  pl.BlockSpec(ublk, lambda b, ps, i: (b, 0, jnp.minimum(i + 1, last), 0)),
                  pl.BlockSpec(ublk, lambda b, ps, i: (b, 0, last - i, 0)),
                  pl.BlockSpec(ublk, lambda b, ps, i: (b, 0, jnp.maximum(last - i - 1, 0), 0)),
                  const((rows, rows)), const((rows, rows)), const((2, W_SSM, w2)), const((2, 1, w2)),
                  const((2, w2, W_SSM))],
        out_specs=[pl.BlockSpec(ublk, lambda b, ps, i: (b, 0, jnp.where(ps == 0, 0, i), 0)),
                   pl.BlockSpec(ublk, lambda b, ps, i: (b, 0, jnp.where(ps == 0, last, last - i), 0))],
        out_shape=[jax.ShapeDtypeStruct((bsz, S5_SEGS, seg_len, W_SSM), BF16)] * 2,
        scratch_shapes=[pltpu.VMEM((rows, w2), F32)] * 4 + [pltpu.VMEM((2, S5_SEGS, w2), F32)],
        compiler_params=_cparams(("arbitrary",) * 3),
        name="s5_scan",
    )(u4, u4, u4, u4, jnp.asarray(perm).astype(BF16), jnp.asarray(perm.T).astype(BF16), bb.astype(BF16), a,
      cc.astype(BF16))
    return y_f.reshape(bsz * seq, W_SSM), y_b.reshape(bsz * seq, W_SSM)


def _gelu_tanh(x):
    c = math.sqrt(2.0 / math.pi)
    return 0.5 * x * (1.0 + jnp.tanh(c * (x + 0.044715 * (x * x * x))))


def _s5_post(y_f, y_b, u, dskip, glu_w, glu_b):
    y = _gelu_tanh(y_f.astype(F32) + y_b.astype(F32) + dskip * u)
    return y * jax.nn.sigmoid(_bdot(y, glu_w) + glu_b)


def _split_dot(x, w):
    hi = x.astype(BF16)
    lo = (x - hi.astype(F32)).astype(BF16)
    return (jnp.dot(hi, w, preferred_element_type=F32) + jnp.dot(lo, w, preferred_element_type=F32))


def _dot_nt(a, b):
    return lax.dot_general(a.astype(BF16), b.astype(BF16), (((1,), (1,)), ((), ())),
                           preferred_element_type=F32)


def _rwkv_masks(chunk):
    hh = RWKV_HEADS
    r = np.arange(hh * chunk)[:, None] // chunk
    bm_feat = (r == np.arange(W_RWKV)[None, :] // RWKV_HEAD).astype(np.float32)
    bm_time = (r == np.arange(hh * chunk)[None, :] // chunk).astype(np.float32)
    f = np.arange(W_RWKV)
    bm_head = (f[:, None] // RWKV_HEAD == f[None, :] // RWKV_HEAD).astype(np.float32)
    return bm_feat, bm_time, bm_head


def _rwkv_chunk(fwd, x, edge_row, logw, a, mu, k_k, k_a, r_k, bmf16, bmt16, bmh):
    ll = x.shape[0]
    w = W_RWKV
    rows = lax.broadcasted_iota(jnp.int32, x.shape, 0)
    if fwd:
        shifted = jnp.where(rows == 0, edge_row, pltpu.roll(x, 1, axis=0))
    else:
        shifted = jnp.where(rows == ll - 1, edge_row, pltpu.roll(x, ll - 1, axis=0))
    rkv = x + (shifted - x) * mu
    r, k, v = rkv[:, :w], rkv[:, w:2 * w], rkv[:, 2 * w:]
    bmh16 = bmh.astype(BF16)

    kk = k * k_k
    k2 = k * (1.0 + (a - 1.0) * k_a)
    both = jnp.concatenate([kk * kk, r * k2 * r_k], axis=0)
    hi = both.astype(BF16)
    lo = (both - hi.astype(F32)).astype(BF16)
    sums = jnp.dot(jnp.concatenate([hi, lo], axis=0), bmh16, preferred_element_type=F32)
    sums = sums[:2 * ll] + sums[2 * ll:]
    ksq = sums[:ll]
    bonus = sums[ll:] * v

    ti = lax.broadcasted_iota(jnp.int32, (ll, ll), 0)
    si = lax.broadcasted_iota(jnp.int32, (ll, ll), 1)
    tri = ((si <= ti) if fwd else (si >= ti)).astype(F32).astype(BF16)
    lhi = logw.astype(BF16)
    llo = (logw - lhi.astype(F32)).astype(BF16)
    cum2 = jnp.dot(tri, jnp.concatenate([lhi, llo], axis=1), preferred_element_type=F32)
    cum = cum2[:, :w] + cum2[:, w:]
    yield
    kk = kk * lax.rsqrt(ksq + 1e-12)
    ctot = jnp.sum(logw, axis=0, keepdims=True)
    e_neg = jnp.exp(-cum)
    e_rem = jnp.exp(ctot - cum)
    ah = -kk * jnp.exp(cum - logw)
    rh = r * jnp.exp(cum)
    bvec = kk * a
    bh, kh = bvec * e_neg, k2 * e_neg
    bt, kt = bvec * e_rem, k2 * e_rem

    def bd(m):
        m16 = m.astype(BF16)
        return jnp.concatenate([m16] * RWKV_HEADS, axis=0) * (bmf16 if m.shape[1] == w else bmt16)

    gram = _dot_nt(jnp.concatenate([ah, rh], axis=0), jnp.concatenate([bd(bh), bd(kh)], axis=0))
    yield
    l4 = RWKV_HEADS * ll
    tt = lax.broadcasted_iota(jnp.int32, (ll, l4), 0)
    ss = lax.broadcasted_iota(jnp.int32, (ll, l4), 1) % ll
    strict = (ss < tt) if fwd else (ss > tt)
    incl = (ss <= tt) if fwd else (ss >= tt)
    n_ab = jnp.where(strict, gram[:ll, :l4], 0.0)
    a_ak = jnp.where(strict, gram[:ll, l4:], 0.0)
    a_rb = jnp.where(incl, gram[ll:, :l4], 0.0)
    a_rk = jnp.where(incl, gram[ll:, l4:], 0.0)

    pw = n_ab
    tinv = jnp.where(ss == tt, 1.0, 0.0) + n_ab
    akv = _bdot(a_ak, bd(v))
    for _ in range(int(math.log2(ll)) - 1):
        pw = _bdot(pw, bd(pw))
        yield
        tinv = tinv + _bdot(tinv, bd(pw))
        yield

    ta = _bdot(tinv, jnp.concatenate([bd(ah), bd(akv)], axis=1))
    yield
    ap, wm = ta[:, :w], ta[:, w:]
    rp = rh + _bdot(a_rb, bd(ap))
    y0 = _bdot(jnp.concatenate([a_rb, a_rk], axis=1), jnp.concatenate([bd(wm), bd(v)], axis=0))
    btk = jnp.transpose(jnp.concatenate([bt, kt], axis=0))
    hp = RWKV_PAIR
    pqs = []
    for cs in (slice(0, hp), slice(hp, w)):
        rhs = jnp.concatenate([jnp.concatenate([ap[:, cs], wm[:, cs]], axis=1),
                               jnp.concatenate([jnp.zeros_like(v[:, cs]), v[:, cs]], axis=1)], axis=0)
        pqs.append(_bdot(btk[cs, :], rhs))
    yield
    eye = (lax.broadcasted_iota(jnp.int32, (hp, hp), 0) == lax.broadcasted_iota(jnp.int32, (hp, hp), 1))
    bmp = bmh[:hp, :hp]
    decay = jnp.exp(ctot)
    pm = [pq[:, :hp] * bmp + jnp.where(eye, decay[:, cs], 0.0) for pq, cs in zip(pqs, (slice(0, hp), slice(hp, w)))]
    qm = [pq[:, hp:] * bmp for pq in pqs]
    yield rp, y0, pm, qm, bonus


def _run_lockstep(gens):
    results = [None] * len(gens)
    live = list(range(len(gens)))
    while live:
        for s in list(live):
            try:
                out = next(gens[s])
            except StopIteration:
                live.remove(s)
            else:
                if out is not None:
                    results[s] = out
    return results


def _rwkv_body(chunk, nsub, nb, *refs):
    dir_refs = [refs[0:4], refs[4:8]]
    mu_ref, kk_ref, ka_ref, rk_ref, bmf_ref, bmt_ref, bmh_ref = refs[8:15]
    y_refs, bonus_refs, h_ref = refs[15:17], refs[17:19], refs[19]
    c = pl.program_id(0)

    @pl.when(c == 0)
    def _():
        h_ref[...] = jnp.zeros_like(h_ref)

    gens, where = [], []
    for d in range(2):
        x_ref, edge_ref, logw_ref, a_ref = dir_refs[d]
        order = range(nsub) if d == 0 else range(nsub - 1, -1, -1)
        for b in range(nb):
            x = x_ref[b].astype(F32)
            for n, j in enumerate(order):
                rows = slice(j * chunk, (j + 1) * chunk)
                if n == 0:
                    hb = edge_ref.shape[1]
                    edge = edge_ref[b, hb - 1:hb, :] if d == 0 else edge_ref[b, 0:1, :]
                    edge = jnp.where(c == 0, 0.0, edge.astype(F32))
                else:
                    e = j * chunk - 1 if d == 0 else (j + 1) * chunk
                    edge = x[e:e + 1, :]
                gens.append(_rwkv_chunk(d == 0, x[rows], edge, logw_ref[0, b, rows, :],
                                        a_ref[0, b, rows, :].astype(F32), mu_ref[d], kk_ref[...], ka_ref[...],
                                        rk_ref[...], bmf_ref[...], bmt_ref[...], bmh_ref[...]))
                where.append((d, b, rows))
    parts = _run_lockstep(gens)

    def carry(s):
        hp = RWKV_PAIR
        h = [h_ref[s, 0], h_ref[s, 1]]
        for n in range(nsub):
            rp, y0, pm, qm, bonus = parts[s * nsub + n]
            d, b, rows = where[s * nsub + n]
            y = jnp.concatenate([_bdot(rp[:, :hp], h[0]), _bdot(rp[:, hp:], h[1])], axis=1) + y0
            y_refs[d][b, rows, :] = y.astype(y_refs[d].dtype)
            bonus_refs[d][b, rows, :] = bonus.astype(bonus_refs[d].dtype)
            h = [_bdot(pm[0], h[0]) + qm[0], _bdot(pm[1], h[1]) + qm[1]]
            yield
        h_ref[s, 0] = h[0]
        h_ref[s, 1] = h[1]

    _run_lockstep([carry(s) for s in range(2 * nb)])


def _rwkv_scan(p_rkv, logw, agate, bsz, seq, chunk, nsub, mu_rkv, k_k, k_a, r_k):
    n = bsz * seq
    blk = chunk * nsub
    n_chunks = seq // blk
    hb = PACKED_ROWS
    per = blk // hb
    w3 = 3 * W_RWKV
    bmf, bmt, bmh = _rwkv_masks(chunk)
    consts = (mu_rkv.reshape(2, 1, w3), k_k.reshape(1, -1), k_a.reshape(1, -1), r_k.reshape(1, -1),
              jnp.asarray(bmf).astype(BF16), jnp.asarray(bmt).astype(BF16), jnp.asarray(bmh))
    x3 = p_rkv.reshape(bsz, seq, w3)
    logw4 = logw.reshape(2, bsz, seq, W_RWKV)
    a4 = agate.reshape(2, bsz, seq, W_RWKV)
    args, in_specs, out_specs = [], [], []
    for d in range(2):
        cidx = (lambda c: c) if d == 0 else (lambda c: n_chunks - 1 - c)
        if d == 0:
            edge = lambda c: (0, jnp.maximum(c * per - 1, 0), 0)
        else:
            edge = lambda c: (0, jnp.minimum((n_chunks - c) * per, seq // hb - 1), 0)
        args += [x3, x3, logw4, a4]
        in_specs += [pl.BlockSpec((bsz, blk, w3), lambda c, cidx=cidx: (0, cidx(c), 0)),
                     pl.BlockSpec((bsz, hb, w3), edge),
                     pl.BlockSpec((1, bsz, blk, W_RWKV), lambda c, d=d, cidx=cidx: (d, 0, cidx(c), 0)),
                     pl.BlockSpec((1, bsz, blk, W_RWKV), lambda c, d=d, cidx=cidx: (d, 0, cidx(c), 0))]
        out_specs.append(pl.BlockSpec((bsz, blk, W_RWKV), lambda c, cidx=cidx: (0, cidx(c), 0)))

    def full(a):
        nd = a.ndim
        return pl.BlockSpec(a.shape, lambda c: (0,) * nd)

    y_f, y_b, bonus_f, bonus_b = pl.pallas_call(
        functools.partial(_rwkv_body, chunk, nsub, bsz),
        grid=(n_chunks,),
        in_specs=in_specs + [full(a) for a in consts],
        out_specs=out_specs + out_specs,
        out_shape=[jax.ShapeDtypeStruct((bsz, seq, W_RWKV), BF16)] * 4,
        scratch_shapes=[pltpu.VMEM((2 * bsz, W_RWKV // RWKV_PAIR, RWKV_PAIR, RWKV_PAIR), F32)],
        compiler_params=_cparams(("arbitrary",)),
        name="rwkv_scan",
    )(*args, *consts)
    return [a.reshape(n, W_RWKV) for a in (y_f, y_b, bonus_f, bonus_b)]


def _head_mean(x, bmh16):
    return _split_dot(x, bmh16) * (1.0 / RWKV_HEAD)


def _rwkv_post(y, bonus, g, gn_g, gn_b, bmh16):
    mu = _head_mean(y, bmh16)
    yc = y - mu
    var = _head_mean(yc * yc, bmh16)
    yn = yc * lax.rsqrt(var + RWKV_GN_EPS) * gn_g + gn_b
    return (yn + bonus) * g


FFT_R1 = 64


def _fft_tables(seq):
    r1, r2 = FFT_R1, seq // FFT_R1
    i1 = np.arange(r1)
    ang1 = ((i1[:, None] * i1[None, :]) % r1) * (2.0 * math.pi / r1)
    stage1 = np.concatenate([np.cos(ang1), -np.sin(ang1)], axis=0)
    k1 = i1[:, None, None]
    k2 = np.arange(r2)[None, :, None]
    n2 = np.arange(r2)[None, None, :]
    ang2 = ((n2 * (k1 + r1 * k2)) % seq) * (2.0 * math.pi / seq)
    mr, mi = np.cos(ang2), -np.sin(ang2)
    stage2 = np.concatenate([np.concatenate([mr, -mi], axis=2),
                             np.concatenate([mi, mr], axis=2)], axis=1)
    c = np.arange(W_FFT)
    same = (c[:, None] // FFT_CH) == (c[None, :] // FFT_CH)
    angc = ((c[:, None] * c[None, :]) % FFT_CH) * (2.0 * math.pi / FFT_CH)
    scale = 1.0 / math.sqrt(seq * FFT_CH)
    chan = np.concatenate([np.where(same, np.cos(angc), 0.0), np.where(same, np.sin(angc), 0.0)],
                          axis=0) * scale
    return tuple(jnp.asarray(t, dtype=F32).astype(BF16) for t in (stage1, stage2, chan))


def _fft1_body(z_ref, m_ref, a_ref):
    r1 = FFT_R1
    for j in range(z_ref.shape[2]):
        a = _bdot(m_ref[...], z_ref[0, :, j, :])
        a_ref[0, 0, :, j, :] = a[:r1]
        a_ref[0, 1, :, j, :] = a[r1:]


def _fft2_body(kb, r2, a_ref, m_ref, chan_ref, o_ref):
    for j in range(kb):
        x = jnp.concatenate([a_ref[0, 0, j], a_ref[0, 1, j]], axis=0)
        f = _bdot(m_ref[j], x)
        fri = jnp.concatenate([f[:r2], f[r2:]], axis=1)
        o_ref[0, :, j * W_FFT:(j + 1) * W_FFT] = _bdot(fri, chan_ref[...]).astype(o_ref.dtype)


def _fourier(p_fft, bsz, seq, tables):
    r1, r2 = FFT_R1, seq // FFT_R1
    stage1, stage2, chan = tables
    tn = SUBLANES
    a = pl.pallas_call(
        _fft1_body,
        grid=(bsz, r2 // tn),
        in_specs=[pl.BlockSpec((1, r1, tn, W_FFT), lambda b, j: (b, 0, j, 0)),
                  pl.BlockSpec((2 * r1, r1), lambda b, j: (0, 0))],
        out_specs=pl.BlockSpec((1, 2, r1, tn, W_FFT), lambda b, j: (b, 0, 0, j, 0)),
        out_shape=jax.ShapeDtypeStruct((bsz, 2, r1, r2, W_FFT), F32),
        compiler_params=_cparams(("parallel", "parallel")),
        name="fft_stage1",
    )(p_fft.reshape(bsz, r1, r2, W_FFT), stage1)
    kb = 8
    out = pl.pallas_call(
        functools.partial(_fft2_body, kb, r2),
        grid=(bsz, r1 // kb),
        in_specs=[pl.BlockSpec((1, 2, kb, r2, W_FFT), lambda b, j: (b, 0, j, 0, 0)),
                  pl.BlockSpec((kb, 2 * r2, 2 * r2), lambda b, j: (j, 0, 0)),
                  pl.BlockSpec((2 * W_FFT, W_FFT), lambda b, j: (0, 0))],
        out_specs=pl.BlockSpec((1, r2, kb * W_FFT), lambda b, j: (b, 0, j)),
        out_shape=jax.ShapeDtypeStruct((bsz, r2, r1 * W_FFT), BF16),
        compiler_params=_cparams(("parallel", "parallel")),
        name="fft_stage2",
    )(a, stage2, chan)
    return out.reshape(bsz * seq, W_FFT)


def _mixout_body(seq, tile, h_ref, ysf_ref, ysb_ref, pssm_ref, yrf_ref, yrb_ref, bnf_ref, bnb_ref, g_ref, pc_ref,
                 pcp_ref, pcn_ref, yf_ref, wout_ref, dskip_ref, gluw_ref, glub_ref, gng_ref, gnb_ref, convw_ref,
                 bmh_ref, lng_ref, lnb_ref, o_ref):
    i = pl.program_id(0)
    f32 = lambda ref: ref[...].astype(F32)
    y_a = _s5_post(f32(ysf_ref), f32(ysb_ref), f32(pssm_ref), dskip_ref[...], gluw_ref[...], glub_ref[...])
    y_b = _rwkv_post(f32(yrf_ref) + f32(yrb_ref), f32(bnf_ref) + f32(bnb_ref), f32(g_ref), gng_ref[...],
                     gnb_ref[...], bmh_ref[...])
    pc = f32(pc_ref)
    wc = W_CONV
    prev_row, next_row = _halo_rows(pcp_ref, pcn_ref, i * tile, tile, seq)
    z = pc[:, wc:2 * wc] * pc[:, 2 * wc:]
    z_prev, z_next = _shift_rows(z, prev_row[:, wc:2 * wc] * prev_row[:, 2 * wc:],
                                 next_row[:, wc:2 * wc] * next_row[:, 2 * wc:])
    y_c = pc[:, :wc] * (convw_ref[0:1, :] * z_prev + convw_ref[1:2, :] * z + convw_ref[2:3, :] * z_next)
    mix = (_bdot(y_a, wout_ref[0:W_SSM, :]) + _bdot(y_b, wout_ref[W_SSM:W_SSM + W_RWKV, :])
           + _bdot(y_c, wout_ref[W_SSM + W_RWKV:W_SSM + W_RWKV + W_CONV, :])
           + _bdot(yf_ref[...], wout_ref[W_SSM + W_RWKV + W_CONV:, :]))
    o_ref[...] = _layer_norm(DEEPNORM_ALPHA * h_ref[...] + mix, lng_ref[...], lnb_ref[...])


def _mixout(h, seq, tile, y_s5, p_ssm, rwkv_outs, g, p_conv, y_fft, w_out_all, l, dskip, glu_w, glu_b, gn_g, gn_b,
            conv_w, ln_g, ln_b):
    n, d = h.shape
    row = lambda w: pl.BlockSpec((tile, w), lambda i: (i, 0))
    pcp, pcn = _halo_specs(tile, 3 * W_CONV, n, PACKED_ROWS)
    bmh16 = jnp.asarray(_rwkv_masks(RWKV_HEAD)[2]).astype(BF16)
    consts = (dskip.reshape(1, -1), glu_w.astype(BF16), glu_b.reshape(1, -1),
              gn_g.reshape(1, -1), gn_b.reshape(1, -1), conv_w, bmh16, ln_g.reshape(1, -1), ln_b.reshape(1, -1))

    def full(a):
        nd = a.ndim
        return pl.BlockSpec(a.shape, lambda i: (0,) * nd)

    return pl.pallas_call(
        functools.partial(_mixout_body, seq, tile),
        grid=(n // tile,),
        in_specs=[row(d)] + [row(W_SSM)] * 3 + [row(W_RWKV)] * 5
                 + [row(3 * W_CONV), pcp, pcn, row(W_FFT), _layer_spec(w_out_all, l)] + [full(c) for c in consts],
        out_specs=row(d),
        out_shape=jax.ShapeDtypeStruct((n, d), F32),
        compiler_params=_cparams(("parallel",)),
        name="mix_out",
    )(h, *y_s5, p_ssm, *rwkv_outs, g, p_conv, p_conv, p_conv, y_fft, w_out_all, *consts)


def _ffn_body(cols, h_ref, w1_ref, w3_ref, w2_ref, lng_ref, lnb_ref, o_ref, u_ref):
    x = h_ref[...].astype(BF16)
    dff = w1_ref.shape[1]
    for c0 in range(0, dff, cols):
        cs = slice(c0, min(c0 + cols, dff))
        a1 = jnp.dot(x, w1_ref[:, cs], preferred_element_type=F32)
        a3 = jnp.dot(x, w3_ref[:, cs], preferred_element_type=F32)
        u_ref[:, cs] = (jax.nn.silu(a1) * a3).astype(BF16)
    f = jnp.dot(u_ref[...], w2_ref[...], preferred_element_type=F32)
    o_ref[...] = _layer_norm(DEEPNORM_ALPHA * h_ref[...] + f, lng_ref[...], lnb_ref[...])


def _ffn(h, tile, cols, w1_all, w3_all, w2_all, l, ln_g, ln_b):
    n, d = h.shape
    dff = w1_all.shape[2]
    once = lambda shape: pl.BlockSpec(shape, lambda i: (0, 0), pipeline_mode=pl.Buffered(1))
    layer = lambda w: _layer_spec(w, l, pipeline_mode=pl.Buffered(1))
    return pl.pallas_call(
        functools.partial(_ffn_body, cols),
        grid=(n // tile,),
        in_specs=[pl.BlockSpec((tile, d), lambda i: (i, 0)), layer(w1_all), layer(w3_all), layer(w2_all),
                  once((1, d)), once((1, d))],
        out_specs=pl.BlockSpec((tile, d), lambda i: (i, 0)),
        out_shape=jax.ShapeDtypeStruct((n, d), F32),
        scratch_shapes=[pltpu.VMEM((tile, dff), BF16)],
        compiler_params=_cparams(("parallel",)),
        name="ffn",
    )(h, w1_all, w3_all, w2_all, ln_g.reshape(1, d), ln_b.reshape(1, d))


ROW_TILE = 512
S5_TILE_STEPS = 64
RWKV_CHUNK = 64
RWKV_CHUNKS_PER_STEP = 4
FFN_COLS = 256


def kernel(x, ln0_g, ln0_b, w_in, s5_lambda_re, s5_lambda_im, s5_log_dt, s5_b_re, s5_b_im, s5_c_re, s5_c_im, s5_d,
           s5_glu_w, s5_glu_b, rwkv_mu_rkv, rwkv_mu_w, rwkv_mu_a, rwkv_mu_g, rwkv_w0, rwkv_w1, rwkv_w2, rwkv_a0,
           rwkv_a1, rwkv_a2, rwkv_g1, rwkv_g2, rwkv_k_k, rwkv_k_a, rwkv_r_k, rwkv_gn_g, rwkv_gn_b, conv_w, w_out,
           ln1_g, ln1_b, ffn_w1, ffn_w3, ffn_w2, ln2_g, ln2_b):
    bsz, seq, d = x.shape
    n = bsz * seq
    tile = min(ROW_TILE, seq)
    s5_tl = min(S5_TILE_STEPS, seq // S5_SEGS // 2)
    fft_tables = _fft_tables(seq)
    w_in16, w_out16 = w_in.astype(BF16), w_out.astype(BF16)
    ffn_w116, ffn_w316, ffn_w216 = ffn_w1.astype(BF16), ffn_w3.astype(BF16), ffn_w2.astype(BF16)
    h = _ln0(x.reshape(n, d), ln0_g, ln0_b, tile)
    for l in range(w_in.shape[0]):
        p_ssm, p_rkv, p_conv, p_fft, logw, agate, g = _inproj(
            h, seq, tile, w_in16, l, rwkv_mu_w[l], rwkv_mu_a[l], rwkv_mu_g[l], rwkv_w1[l], rwkv_w2[l], rwkv_w0[l],
            rwkv_a1[l], rwkv_a2[l], rwkv_a0[l], rwkv_g1[l], rwkv_g2[l])
        bb, a, cc = _s5_params(s5_lambda_re[l], s5_lambda_im[l], s5_log_dt[l], s5_b_re[l], s5_b_im[l],
                               s5_c_re[l], s5_c_im[l])
        y_s5 = _s5_scan(p_ssm, bsz, seq, bb, a, cc, s5_tl)
        rwkv_outs = _rwkv_scan(p_rkv, logw, agate, bsz, seq, RWKV_CHUNK, RWKV_CHUNKS_PER_STEP, rwkv_mu_rkv[l], rwkv_k_k[l],
                               rwkv_k_a[l], rwkv_r_k[l].reshape(-1))
        y_fft = _fourier(p_fft, bsz, seq, fft_tables)
        h = _mixout(h, seq, tile, y_s5, p_ssm, rwkv_outs, g, p_conv, y_fft, w_out16, l, s5_d[l], s5_glu_w[l],
                    s5_glu_b[l], rwkv_gn_g[l], rwkv_gn_b[l], conv_w[l], ln1_g[l], ln1_b[l])
        h = _ffn(h, tile, FFN_COLS, ffn_w116, ffn_w316, ffn_w216, l, ln2_g[l], ln2_b[l])
    return h.reshape(bsz, seq, d)
```

```python
import functools
import math

import jax
import jax.numpy as jnp
import numpy as np
from jax import lax
from jax.experimental import pallas as pl
from jax.experimental.pallas import tpu as pltpu

W_SSM = 256
W_RWKV = 256
W_CONV = 256
W_FFT = 256
SSM_CH = 16
SSM_GROUPS = 16
SSM_STATE = 64
RWKV_HEAD = 64
RWKV_HEADS = 4
RWKV_PAIR = 2 * RWKV_HEAD
FFT_GROUPS = 4
FFT_CH = 64
RWKV_DECAY_SCALE = math.exp(-0.5)
RWKV_GN_EPS = 64e-5
LN_EPS = 1e-5
DEPTH = 2
DEEPNORM_ALPHA = (2 * DEPTH) ** 0.25

SUBLANES = 8
PACKED_ROWS = 16
VMEM_LIMIT = 48 * 1024 * 1024

BF16 = jnp.bfloat16
F32 = jnp.float32


def _cparams(sem):
    return pltpu.CompilerParams(dimension_semantics=sem, vmem_limit_bytes=VMEM_LIMIT)


def _bdot(a, b):
    return jnp.dot(a.astype(BF16), b.astype(BF16), preferred_element_type=F32)


def _layer_norm(x, g, b):
    mu = jnp.mean(x, axis=-1, keepdims=True)
    xc = x - mu
    var = jnp.mean(xc * xc, axis=-1, keepdims=True)
    return xc * lax.rsqrt(var + LN_EPS) * g + b


def _shift_rows(x, prev_row, next_row):
    n = x.shape[0]
    rows = lax.broadcasted_iota(jnp.int32, x.shape, 0)
    x_prev = jnp.where(rows == 0, prev_row, pltpu.roll(x, 1, axis=0))
    x_next = jnp.where(rows == n - 1, next_row, pltpu.roll(x, n - 1, axis=0))
    return x_prev, x_next


def _halo_rows(prev_ref, next_ref, row0, n_rows, seq):
    first = (row0 % seq) == 0
    last = ((row0 + n_rows) % seq) == 0
    hb = prev_ref.shape[0]
    prev_row = jnp.where(first, 0.0, prev_ref[hb - 1:hb, :].astype(F32))
    next_row = jnp.where(last, 0.0, next_ref[0:1, :].astype(F32))
    return prev_row, next_row


def _layer_spec(stacked, l, **kw):
    return pl.BlockSpec((None,) + stacked.shape[1:], lambda i: (l, 0, 0), **kw)


def _halo_specs(tile, width, n_rows_total, hb):
    per = tile // hb
    last_blk = n_rows_total // hb - 1
    prev = pl.BlockSpec((hb, width), lambda i: (jnp.maximum(i * per - 1, 0), 0))
    nxt = pl.BlockSpec((hb, width), lambda i: (jnp.minimum((i + 1) * per, last_blk), 0))
    return prev, nxt


def _inproj_body(seq, tile, entry_ln, *refs):
    h_ref, hp_ref, hn_ref = refs[:3]
    k = 5 if entry_ln else 3
    (win_ref, muw_ref, mua_ref, mug_ref, w1_ref, w2_ref, w0_ref, a1_ref, a2_ref, a0_ref, g1_ref,
     g2_ref) = refs[k:k + 12]
    pssm_ref, prkv_ref, pconv_ref, pfft_ref, logw_ref, agate_ref, g_ref = refs[k + 12:k + 19]
    i = pl.program_id(0)
    h, h_before, h_after = h_ref[...], hp_ref[...], hn_ref[...]
    if entry_ln:
        ln = lambda v: _layer_norm(v, refs[3][...], refs[4][...])
        h, h_before, h_after = ln(h), ln(h_before), ln(h_after)
        refs[k + 19][...] = h
    prev_row, next_row = _halo_rows(h_before, h_after, i * tile, tile, seq)
    x_prev, x_next = _shift_rows(h, prev_row, next_row)

    h16 = h.astype(BF16)

    def main():
        col = 0
        for ref in (pssm_ref, prkv_ref, pconv_ref, pfft_ref):
            width = ref.shape[1]
            ref[...] = jnp.dot(h16, win_ref[:, col:col + width], preferred_element_type=F32).astype(ref.dtype)
            col += width
            yield

    def lora(d, x_sh):
        dx = x_sh - h
        t1 = _bdot(h + dx * muw_ref[d:d + 1, :], w1_ref[d])
        yield
        w_lora = _bdot(jnp.tanh(t1), w2_ref[d])
        t2 = _bdot(h + dx * mua_ref[d:d + 1, :], a1_ref[d])
        yield
        logw_ref[d] = -RWKV_DECAY_SCALE * jax.nn.sigmoid(w0_ref[d:d + 1, :] + w_lora)
        a_lora = _bdot(t2, a2_ref[d])
        yield
        agate_ref[d] = jax.nn.sigmoid(a0_ref[d:d + 1, :] + a_lora).astype(agate_ref.dtype)

    def gate():
        xg = h + (0.5 * (x_prev + x_next) - h) * mug_ref[...]
        t3 = _bdot(xg, g1_ref[...])
        yield
        g_ref[...] = _bdot(jax.nn.sigmoid(t3), g2_ref[...]).astype(g_ref.dtype)

    _run_lockstep([main(), lora(0, x_prev), lora(1, x_next), gate()])


def _inproj(h, seq, tile, w_in_all, l, mu_w, mu_a, mu_g, w1, w2, w0, a1, a2, a0, g1, g2, entry_ln=None):
    n, d = h.shape
    ln_args = [] if entry_ln is None else [p.reshape(1, d) for p in entry_ln]
    hp_spec, hn_spec = _halo_specs(tile, d, n, SUBLANES)

    def full(a):
        nd = a.ndim
        return pl.BlockSpec(a.shape, lambda i: (0,) * nd)

    row = lambda w: pl.BlockSpec((tile, w), lambda i: (i, 0))
    row2 = lambda w: pl.BlockSpec((2, tile, w), lambda i: (0, i, 0))
    consts = (mu_w, mu_a, mu_g.reshape(1, d), w1.astype(BF16), w2.astype(BF16), w0,
              a1.astype(BF16), a2.astype(BF16), a0, g1.astype(BF16), g2.astype(BF16))
    return pl.pallas_call(
        functools.partial(_inproj_body, seq, tile, entry_ln is not None),
        grid=(n // tile,),
        in_specs=[pl.BlockSpec((tile, d), lambda i: (i, 0)), hp_spec, hn_spec] + [full(c) for c in ln_args]
                 + [_layer_spec(w_in_all, l)] + [full(c) for c in consts],
        out_specs=[row(W_SSM), row(3 * W_RWKV), row(3 * W_CONV), row(W_FFT),
                   row2(W_RWKV), row2(W_RWKV), row(W_RWKV)] + [row(d)] * len(ln_args[:1]),
        out_shape=[jax.ShapeDtypeStruct((n, W_SSM), BF16),
                   jax.ShapeDtypeStruct((n, 3 * W_RWKV), BF16),
                   jax.ShapeDtypeStruct((n, 3 * W_CONV), BF16),
                   jax.ShapeDtypeStruct((n, W_FFT), F32),
                   jax.ShapeDtypeStruct((2, n, W_RWKV), F32),
                   jax.ShapeDtypeStruct((2, n, W_RWKV), BF16),
                   jax.ShapeDtypeStruct((n, W_RWKV), BF16)] + [jax.ShapeDtypeStruct((n, d), F32)] * len(ln_args[:1]),
        compiler_params=_cparams(("parallel",)),
        name="inproj",
    )(h, h, h, *ln_args, w_in_all, *consts)


S5_SEGS = SUBLANES


def _cmul(ar, ai, br, bi):
    return ar * br - ai * bi, ar * bi + ai * br


class _S5Dir:
    def __init__(self, fwd, tl, ucur_ref, unext_ref, y_ref, bua_ref, bub_ref, st_ref, perm_ref, permt_ref, bb_ref,
                 a_ref, c_ref):
        self.fwd, self.tl = fwd, tl
        self.ucur_ref, self.unext_ref, self.y_ref = ucur_ref, unext_ref, y_ref
        self.bua_ref, self.bub_ref, self.st_ref = bua_ref, bub_ref, st_ref
        self.perm_ref, self.permt_ref, self.bb_ref, self.c_ref = perm_ref, permt_ref, bb_ref, c_ref
        self.half = SSM_GROUPS * SSM_STATE
        self.rows = tl * S5_SEGS
        self.n_parts = S5_SEGS
        self.part = 2 * self.half // self.n_parts
        shape = (S5_SEGS, self.half)
        self.ar = jnp.broadcast_to(a_ref[:, :self.half], shape)
        self.ai = jnp.broadcast_to(a_ref[:, self.half:], shape)
        self.off_a, self.off_b = (0, tl) if fwd else (tl, 0)

    def project(self, u_ref, off, buf_ref):
        u = u_ref[0, :, off:off + self.tl, :].reshape(self.rows, W_SSM).astype(BF16)
        up = jnp.dot(self.perm_ref[...], u, preferred_element_type=F32).astype(BF16)
        for k in range(self.n_parts):
            cols = slice(k * self.part, (k + 1) * self.part)
            buf_ref[:, cols] = jnp.dot(up, self.bb_ref[:, cols], preferred_element_type=F32)
            yield

    def scan(self, buf_ref, store):
        half, tl = self.half, self.tl
        xr, xi = self.st_ref[:, :half], self.st_ref[:, half:]
        for t in range(tl):
            row = (t if self.fwd else tl - 1 - t) * S5_SEGS
            nr = self.ar * xr - self.ai * xi + buf_ref[row:row + S5_SEGS, :half]
            ni = self.ar * xi + self.ai * xr + buf_ref[row:row + S5_SEGS, half:]
            if store:
                buf_ref[row:row + S5_SEGS, :half] = nr
                buf_ref[row:row + S5_SEGS, half:] = ni
            xr, xi = nr, ni
            if (t + 1) % (tl // self.n_parts) == 0:
                yield
        self.st_ref[:, :half] = xr
        self.st_ref[:, half:] = xi

    def emit(self, buf_ref, off):
        y = None
        for k in range(self.n_parts):
            cols = slice(k * self.part, (k + 1) * self.part)
            yk = _bdot(buf_ref[:, cols], self.c_ref[cols, :])
            y = yk if y is None else y + yk
            yield
        y = jnp.dot(self.permt_ref[...], y.astype(BF16), preferred_element_type=F32)
        self.y_ref[0, :, off:off + self.tl, :] = y.reshape(S5_SEGS, self.tl, W_SSM).astype(self.y_ref.dtype)

    def init_state(self, seg_len):
        half = self.half
        pr, pi = jnp.ones_like(self.ar), jnp.zeros_like(self.ar)
        br, bi = self.ar, self.ai
        e = seg_len
        while e:
            if e & 1:
                pr, pi = _cmul(pr, pi, br, bi)
            br, bi = _cmul(br, bi, br, bi)
            e >>= 1
        er, ei = self.st_ref[:, :half], self.st_ref[:, half:]
        zero = jnp.zeros((1, half), F32)
        order = range(S5_SEGS) if self.fwd else range(S5_SEGS - 1, -1, -1)
        cr, ci, out_r, out_i = zero, zero, {}, {}
        for j in order:
            out_r[j], out_i[j] = cr, ci
            nr, ni = _cmul(pr[0:1], pi[0:1], cr, ci)
            cr, ci = nr + er[j:j + 1], ni + ei[j:j + 1]
        self.st_ref[:, :half] = jnp.concatenate([out_r[j] for j in range(S5_SEGS)], axis=0)
        self.st_ref[:, half:] = jnp.concatenate([out_i[j] for j in range(S5_SEGS)], axis=0)


def _s5_body(tl, seg_len, ucf_ref, unf_ref, ucb_ref, unb_ref, perm_ref, permt_ref, bb_ref, a_ref, c_ref, yf_ref,
             yb_ref, baf_ref, bbf_ref, bab_ref, bbb_ref, st_ref):
    ps = pl.program_id(1)
    i = pl.program_id(2)
    dirs = [_S5Dir(True, tl, ucf_ref, unf_ref, yf_ref, baf_ref, bbf_ref, st_ref.at[0], perm_ref, permt_ref,
                   bb_ref.at[0], a_ref.at[0], c_ref.at[0]),
            _S5Dir(False, tl, ucb_ref, unb_ref, yb_ref, bab_ref, bbb_ref, st_ref.at[1], perm_ref, permt_ref,
                   bb_ref.at[1], a_ref.at[1], c_ref.at[1])]

    @pl.when(jnp.logical_and(i == 0, ps == 0))
    def _():
        st_ref[...] = jnp.zeros_like(st_ref)

    @pl.when(jnp.logical_and(i == 0, ps == 1))
    def _():
        for z in dirs:
            z.init_state(seg_len)

    @pl.when(i == 0)
    def _():
        _run_lockstep([z.project(z.ucur_ref, z.off_a, z.bua_ref) for z in dirs])

    @pl.when(ps == 0)
    def _():
        _run_lockstep([g for z in dirs for g in (z.scan(z.bua_ref, False), z.project(z.ucur_ref, z.off_b, z.bub_ref))])
        _run_lockstep([g for z in dirs for g in (z.scan(z.bub_ref, False), z.project(z.unext_ref, z.off_a, z.bua_ref))])

    @pl.when(ps == 1)
    def _():
        _run_lockstep([g for z in dirs for g in (z.scan(z.bua_ref, True), z.project(z.ucur_ref, z.off_b, z.bub_ref))])
        _run_lockstep([g for z in dirs for g in (z.scan(z.bub_ref, True), z.emit(z.bua_ref, z.off_a),
                                                 z.project(z.unext_ref, z.off_a, z.bua_ref))])
        _run_lockstep([z.emit(z.bub_ref, z.off_b) for z in dirs])


def _s5_params(lam_re, lam_im, log_dt, b_re, b_im, c_re, c_im):
    g, p, hch = SSM_GROUPS, SSM_STATE, SSM_CH
    dt = jnp.exp(log_dt)[..., None]
    mag = jnp.exp(lam_re * dt)
    lb_re = mag * jnp.cos(lam_im * dt)
    lb_im = mag * jnp.sin(lam_im * dt)
    den = lam_re * lam_re + lam_im * lam_im
    nr = lb_re - 1.0
    coef_re = (nr * lam_re + lb_im * lam_im) / den
    coef_im = (lb_im * lam_re - nr * lam_im) / den
    bb_re = coef_re[..., None] * b_re - coef_im[..., None] * b_im
    bb_im = coef_re[..., None] * b_im + coef_im[..., None] * b_re
    eye = jnp.eye(g, dtype=F32)
    bd_in = lambda m: jnp.einsum('dgph,gk->dghkp', m, eye).reshape(2, g * hch, g * p)
    bb = jnp.concatenate([bd_in(bb_re), bd_in(bb_im)], axis=-1)
    bd_out = lambda m: jnp.einsum('dghp,gk->dgpkh', m, eye).reshape(2, g * p, g * hch)
    cc = jnp.concatenate([bd_out(c_re), -bd_out(c_im)], axis=1)
    a = jnp.concatenate([lb_re.reshape(2, 1, g * p), lb_im.reshape(2, 1, g * p)], axis=-1)
    return bb, a, cc


def _s5_scan(p_ssm, bsz, seq, bb, a, cc, tl):
    seg_len = seq // S5_SEGS
    n_pairs = seg_len // (2 * tl)
    rows = tl * S5_SEGS
    w2 = 2 * SSM_GROUPS * SSM_STATE
    src = (np.arange(rows) % S5_SEGS) * tl + np.arange(rows) // S5_SEGS
    perm = np.zeros((rows, rows), np.float32)
    perm[np.arange(rows), src] = 1.0
    last = n_pairs - 1
    u4 = p_ssm.reshape(bsz, S5_SEGS, seg_len, W_SSM)
    ublk = (1, S5_SEGS, 2 * tl, W_SSM)
    const = lambda shape: pl.BlockSpec(shape, lambda b, ps, i: (0,) * len(shape))
    y_f, y_b = pl.pallas_call(
        functools.partial(_s5_body, tl, seg_len),
        grid=(bsz, 2, n_pairs),
        in_specs=[pl.BlockSpec(ublk, lambda b, ps, i: (b, 0, i, 0)),
                  pl.BlockSpec(ublk, lambda b, ps, i: (b, 0, jnp.minimum(i + 1, last), 0)),
                  pl.BlockSpec(ublk, lambda b, ps, i: (b, 0, last - i, 0)),
                  pl.BlockSpec(ublk, lambda b, ps, i: (b, 0, jnp.maximum(last - i - 1, 0), 0)),
                  const((rows, rows)), const((rows, rows)), const((2, W_SSM, w2)), const((2, 1, w2)),
                  const((2, w2, W_SSM))],
        out_specs=[pl.BlockSpec(ublk, lambda b, ps, i: (b, 0, jnp.where(ps == 0, 0, i), 0)),
                   pl.BlockSpec(ublk, lambda b, ps, i: (b, 0, jnp.where(ps == 0, last, last - i), 0))],
        out_shape=[jax.ShapeDtypeStruct((bsz, S5_SEGS, seg_len, W_SSM), BF16)] * 2,
        scratch_shapes=[pltpu.VMEM((rows, w2), F32)] * 4 + [pltpu.VMEM((2, S5_SEGS, w2), F32)],
        compiler_params=_cparams(("arbitrary",) * 3),
        name="s5_scan",
    )(u4, u4, u4, u4, jnp.asarray(perm).astype(BF16), jnp.asarray(perm.T).astype(BF16), bb.astype(BF16), a,
      cc.astype(BF16))
    return y_f.reshape(bsz * seq, W_SSM), y_b.reshape(bsz * seq, W_SSM)


def _gelu_tanh(x):
    c = math.sqrt(2.0 / math.pi)
    return 0.5 * x * (1.0 + jnp.tanh(c * (x + 0.044715 * (x * x * x))))


def _s5_post(y_f, y_b, u, dskip, glu_w, glu_b):
    y = _gelu_tanh(y_f.astype(F32) + y_b.astype(F32) + dskip * u)
    return y * jax.nn.sigmoid(_bdot(y, glu_w) + glu_b)


def _split_dot(x, w):
    hi = x.astype(BF16)
    lo = (x - hi.astype(F32)).astype(BF16)
    return (jnp.dot(hi, w, preferred_element_type=F32) + jnp.dot(lo, w, preferred_element_type=F32))


def _dot_nt(a, b):
    return lax.dot_general(a.astype(BF16), b.astype(BF16), (((1,), (1,)), ((), ())),
                           preferred_element_type=F32)


def _rwkv_masks(chunk):
    hh = RWKV_HEADS
    r = np.arange(hh * chunk)[:, None] // chunk
    bm_feat = (r == np.arange(W_RWKV)[None, :] // RWKV_HEAD).astype(np.float32)
    bm_time = (r == np.arange(hh * chunk)[None, :] // chunk).astype(np.float32)
    f = np.arange(W_RWKV)
    bm_head = (f[:, None] // RWKV_HEAD == f[None, :] // RWKV_HEAD).astype(np.float32)
    return bm_feat, bm_time, bm_head


def _rwkv_chunk(fwd, x, edge_row, logw, a, mu, k_k, k_a, r_k, bmf16, bmt16, bmh):
    ll = x.shape[0]
    w = W_RWKV
    rows = lax.broadcasted_iota(jnp.int32, x.shape, 0)
    if fwd:
        shifted = jnp.where(rows == 0, edge_row, pltpu.roll(x, 1, axis=0))
    else:
        shifted = jnp.where(rows == ll - 1, edge_row, pltpu.roll(x, ll - 1, axis=0))
    rkv = x + (shifted - x) * mu
    r, k, v = rkv[:, :w], rkv[:, w:2 * w], rkv[:, 2 * w:]
    bmh16 = bmh.astype(BF16)

    kk = k * k_k
    k2 = k * (1.0 + (a - 1.0) * k_a)
    both = jnp.concatenate([kk * kk, r * k2 * r_k], axis=0)
    hi = both.astype(BF16)
    lo = (both - hi.astype(F32)).astype(BF16)
    sums = jnp.dot(jnp.concatenate([hi, lo], axis=0), bmh16, preferred_element_type=F32)
    sums = sums[:2 * ll] + sums[2 * ll:]
    ksq = sums[:ll]
    bonus = sums[ll:] * v

    ti = lax.broadcasted_iota(jnp.int32, (ll, ll), 0)
    si = lax.broadcasted_iota(jnp.int32, (ll, ll), 1)
    tri = ((si <= ti) if fwd else (si >= ti)).astype(F32).astype(BF16)
    lhi = logw.astype(BF16)
    llo = (logw - lhi.astype(F32)).astype(BF16)
    cum2 = jnp.dot(tri, jnp.concatenate([lhi, llo], axis=1), preferred_element_type=F32)
    cum = cum2[:, :w] + cum2[:, w:]
    yield
    kk = kk * lax.rsqrt(ksq + 1e-12)
    ctot = jnp.sum(logw, axis=0, keepdims=True)
    e_neg = jnp.exp(-cum)
    e_rem = jnp.exp(ctot - cum)
    ah = -kk * jnp.exp(cum - logw)
    rh = r * jnp.exp(cum)
    bvec = kk * a
    bh, kh = bvec * e_neg, k2 * e_neg
    bt, kt = bvec * e_rem, k2 * e_rem

    def bd(m):
        m16 = m.astype(BF16)
        return jnp.concatenate([m16] * RWKV_HEADS, axis=0) * (bmf16 if m.shape[1] == w else bmt16)

    gram = _dot_nt(jnp.concatenate([ah, rh], axis=0), jnp.concatenate([bd(bh), bd(kh)], axis=0))
    yield
    l4 = RWKV_HEADS * ll
    tt = lax.broadcasted_iota(jnp.int32, (ll, l4), 0)
    ss = lax.broadcasted_iota(jnp.int32, (ll, l4), 1) % ll
    strict = (ss < tt) if fwd else (ss > tt)
    incl = (ss <= tt) if fwd else (ss >= tt)
    n_ab = jnp.where(strict, gram[:ll, :l4], 0.0)
    a_ak = jnp.where(strict, gram[:ll, l4:], 0.0)
    a_rb = jnp.where(incl, gram[ll:, :l4], 0.0)
    a_rk = jnp.where(incl, gram[ll:, l4:], 0.0)

    pw = n_ab
    tinv = jnp.where(ss == tt, 1.0, 0.0) + n_ab
    akv = _bdot(a_ak, bd(v))
    for _ in range(int(math.log2(ll)) - 1):
        pw = _bdot(pw, bd(pw))
        yield
        tinv = tinv + _bdot(tinv, bd(pw))
        yield

    ta = _bdot(tinv, jnp.concatenate([bd(ah), bd(akv)], axis=1))
    yield
    ap, wm = ta[:, :w], ta[:, w:]
    rp = rh + _bdot(a_rb, bd(ap))
    y0 = _bdot(jnp.concatenate([a_rb, a_rk], axis=1), jnp.concatenate([bd(wm), bd(v)], axis=0))
    btk = jnp.transpose(jnp.concatenate([bt, kt], axis=0))
    hp = RWKV_PAIR
    pqs = []
    for cs in (slice(0, hp), slice(hp, w)):
        rhs = jnp.concatenate([jnp.concatenate([ap[:, cs], wm[:, cs]], axis=1),
                               jnp.concatenate([jnp.zeros_like(v[:, cs]), v[:, cs]], axis=1)], axis=0)
        pqs.append(_bdot(btk[cs, :], rhs))
    yield
    eye = (lax.broadcasted_iota(jnp.int32, (hp, hp), 0) == lax.broadcasted_iota(jnp.int32, (hp, hp), 1))
    bmp = bmh[:hp, :hp]
    decay = jnp.exp(ctot)
    pm = [pq[:, :hp] * bmp + jnp.where(eye, decay[:, cs], 0.0) for pq, cs in zip(pqs, (slice(0, hp), slice(hp, w)))]
    qm = [pq[:, hp:] * bmp for pq in pqs]
    yield rp, y0, pm, qm, bonus


def _run_lockstep(gens):
    results = [None] * len(gens)
    live = list(range(len(gens)))
    while live:
        for s in list(live):
            try:
                out = next(gens[s])
            except StopIteration:
                live.remove(s)
            else:
                if out is not None:
                    results[s] = out
    return results


def _rwkv_body(chunk, nsub, nb, *refs):
    dir_refs = [refs[0:4], refs[4:8]]
    mu_ref, kk_ref, ka_ref, rk_ref, bmf_ref, bmt_ref, bmh_ref = refs[8:15]
    y_refs, bonus_refs, h_ref = refs[15:17], refs[17:19], refs[19]
    c = pl.program_id(0)

    @pl.when(c == 0)
    def _():
        h_ref[...] = jnp.zeros_like(h_ref)

    gens, where = [], []
    for d in range(2):
        x_ref, edge_ref, logw_ref, a_ref = dir_refs[d]
        order = range(nsub) if d == 0 else range(nsub - 1, -1, -1)
        for b in range(nb):
            x = x_ref[b].astype(F32)
            for n, j in enumerate(order):
                rows = slice(j * chunk, (j + 1) * chunk)
                if n == 0:
                    hb = edge_ref.shape[1]
                    edge = edge_ref[b, hb - 1:hb, :] if d == 0 else edge_ref[b, 0:1, :]
                    edge = jnp.where(c == 0, 0.0, edge.astype(F32))
                else:
                    e = j * chunk - 1 if d == 0 else (j + 1) * chunk
                    edge = x[e:e + 1, :]
                gens.append(_rwkv_chunk(d == 0, x[rows], edge, logw_ref[0, b, rows, :],
                                        a_ref[0, b, rows, :].astype(F32), mu_ref[d], kk_ref[...], ka_ref[...],
                                        rk_ref[...], bmf_ref[...], bmt_ref[...], bmh_ref[...]))
                where.append((d, b, rows))
    parts = _run_lockstep(gens)

    def carry(s):
        hp = RWKV_PAIR
        h = [h_ref[s, 0], h_ref[s, 1]]
        for n in range(nsub):
            rp, y0, pm, qm, bonus = parts[s * nsub + n]
            d, b, rows = where[s * nsub + n]
            y = jnp.concatenate([_bdot(rp[:, :hp], h[0]), _bdot(rp[:, hp:], h[1])], axis=1) + y0
            y_refs[d][b, rows, :] = y.astype(y_refs[d].dtype)
            bonus_refs[d][b, rows, :] = bonus.astype(bonus_refs[d].dtype)
            h = [_bdot(pm[0], h[0]) + qm[0], _bdot(pm[1], h[1]) + qm[1]]
            yield
        h_ref[s, 0] = h[0]
        h_ref[s, 1] = h[1]

    _run_lockstep([carry(s) for s in range(2 * nb)])


def _rwkv_scan(p_rkv, logw, agate, bsz, seq, chunk, nsub, mu_rkv, k_k, k_a, r_k):
    n = bsz * seq
    blk = chunk * nsub
    n_chunks = seq // blk
    hb = PACKED_ROWS
    per = blk // hb
    w3 = 3 * W_RWKV
    bmf, bmt, bmh = _rwkv_masks(chunk)
    consts = (mu_rkv.reshape(2, 1, w3), k_k.reshape(1, -1), k_a.reshape(1, -1), r_k.reshape(1, -1),
              jnp.asarray(bmf).astype(BF16), jnp.asarray(bmt).astype(BF16), jnp.asarray(bmh))
    x3 = p_rkv.reshape(bsz, seq, w3)
    logw4 = logw.reshape(2, bsz, seq, W_RWKV)
    a4 = agate.reshape(2, bsz, seq, W_RWKV)
    args, in_specs, out_specs = [], [], []
    for d in range(2):
        cidx = (lambda c: c) if d == 0 else (lambda c: n_chunks - 1 - c)
        if d == 0:
            edge = lambda c: (0, jnp.maximum(c * per - 1, 0), 0)
        else:
            edge = lambda c: (0, jnp.minimum((n_chunks - c) * per, seq // hb - 1), 0)
        args += [x3, x3, logw4, a4]
        in_specs += [pl.BlockSpec((bsz, blk, w3), lambda c, cidx=cidx: (0, cidx(c), 0)),
                     pl.BlockSpec((bsz, hb, w3), edge),
                     pl.BlockSpec((1, bsz, blk, W_RWKV), lambda c, d=d, cidx=cidx: (d, 0, cidx(c), 0)),
                     pl.BlockSpec((1, bsz, blk, W_RWKV), lambda c, d=d, cidx=cidx: (d, 0, cidx(c), 0))]
        out_specs.append(pl.BlockSpec((bsz, blk, W_RWKV), lambda c, cidx=cidx: (0, cidx(c), 0)))

    def full(a):
        nd = a.ndim
        return pl.BlockSpec(a.shape, lambda c: (0,) * nd)

    y_f, y_b, bonus_f, bonus_b = pl.pallas_call(
        functools.partial(_rwkv_body, chunk, nsub, bsz),
        grid=(n_chunks,),
        in_specs=in_specs + [full(a) for a in consts],
        out_specs=out_specs + out_specs,
        out_shape=[jax.ShapeDtypeStruct((bsz, seq, W_RWKV), BF16)] * 4,
        scratch_shapes=[pltpu.VMEM((2 * bsz, W_RWKV // RWKV_PAIR, RWKV_PAIR, RWKV_PAIR), F32)],
        compiler_params=_cparams(("arbitrary",)),
        name="rwkv_scan",
    )(*args, *consts)
    return [a.reshape(n, W_RWKV) for a in (y_f, y_b, bonus_f, bonus_b)]


def _head_mean(x, bmh16):
    return _split_dot(x, bmh16) * (1.0 / RWKV_HEAD)


def _rwkv_post(y, bonus, g, gn_g, gn_b, bmh16):
    mu = _head_mean(y, bmh16)
    yc = y - mu
    var = _head_mean(yc * yc, bmh16)
    yn = yc * lax.rsqrt(var + RWKV_GN_EPS) * gn_g + gn_b
    return (yn + bonus) * g


FFT_R1 = 64


def _fft_tables(seq):
    r1, r2 = FFT_R1, seq // FFT_R1
    i1 = np.arange(r1)
    ang1 = ((i1[:, None] * i1[None, :]) % r1) * (2.0 * math.pi / r1)
    stage1 = np.concatenate([np.cos(ang1), -np.sin(ang1)], axis=0)
    k1 = i1[:, None, None]
    k2 = np.arange(r2)[None, :, None]
    n2 = np.arange(r2)[None, None, :]
    ang2 = ((n2 * (k1 + r1 * k2)) % seq) * (2.0 * math.pi / seq)
    mr, mi = np.cos(ang2), -np.sin(ang2)
    stage2 = np.concatenate([np.concatenate([mr, -mi], axis=2),
                             np.concatenate([mi, mr], axis=2)], axis=1)
    c = np.arange(W_FFT)
    same = (c[:, None] // FFT_CH) == (c[None, :] // FFT_CH)
    angc = ((c[:, None] * c[None, :]) % FFT_CH) * (2.0 * math.pi / FFT_CH)
    scale = 1.0 / math.sqrt(seq * FFT_CH)
    chan = np.concatenate([np.where(same, np.cos(angc), 0.0), np.where(same, np.sin(angc), 0.0)],
                          axis=0) * scale
    return tuple(jnp.asarray(t, dtype=F32).astype(BF16) for t in (stage1, stage2, chan))


def _fft1_body(z_ref, m_ref, a_ref):
    r1 = FFT_R1
    for j in range(z_ref.shape[2]):
        a = _bdot(m_ref[...], z_ref[0, :, j, :])
        a_ref[0, 0, :, j, :] = a[:r1]
        a_ref[0, 1, :, j, :] = a[r1:]


def _fft2_body(kb, r2, a_ref, m_ref, chan_ref, o_ref):
    for j in range(kb):
        x = jnp.concatenate([a_ref[0, 0, j], a_ref[0, 1, j]], axis=0)
        f = _bdot(m_ref[j], x)
        fri = jnp.concatenate([f[:r2], f[r2:]], axis=1)
        o_ref[0, :, j * W_FFT:(j + 1) * W_FFT] = _bdot(fri, chan_ref[...]).astype(o_ref.dtype)


def _fourier(p_fft, bsz, seq, tables):
    r1, r2 = FFT_R1, seq // FFT_R1
    stage1, stage2, chan = tables
    tn = SUBLANES
    a = pl.pallas_call(
        _fft1_body,
        grid=(bsz, r2 // tn),
        in_specs=[pl.BlockSpec((1, r1, tn, W_FFT), lambda b, j: (b, 0, j, 0)),
                  pl.BlockSpec((2 * r1, r1), lambda b, j: (0, 0))],
        out_specs=pl.BlockSpec((1, 2, r1, tn, W_FFT), lambda b, j: (b, 0, 0, j, 0)),
        out_shape=jax.ShapeDtypeStruct((bsz, 2, r1, r2, W_FFT), F32),
        compiler_params=_cparams(("parallel", "parallel")),
        name="fft_stage1",
    )(p_fft.reshape(bsz, r1, r2, W_FFT), stage1)
    kb = 8
    out = pl.pallas_call(
        functools.partial(_fft2_body, kb, r2),
        grid=(bsz, r1 // kb),
        in_specs=[pl.BlockSpec((1, 2, kb, r2, W_FFT), lambda b, j: (b, 0, j, 0, 0)),
                  pl.BlockSpec((kb, 2 * r2, 2 * r2), lambda b, j: (j, 0, 0)),
                  pl.BlockSpec((2 * W_FFT, W_FFT), lambda b, j: (0, 0))],
        out_specs=pl.BlockSpec((1, r2, kb * W_FFT), lambda b, j: (b, 0, j)),
        out_shape=jax.ShapeDtypeStruct((bsz, r2, r1 * W_FFT), BF16),
        compiler_params=_cparams(("parallel", "parallel")),
        name="fft_stage2",
    )(a, stage2, chan)
    return out.reshape(bsz * seq, W_FFT)


def _mixout_body(seq, tile, h_ref, ysf_ref, ysb_ref, pssm_ref, yrf_ref, yrb_ref, bnf_ref, bnb_ref, g_ref, pc_ref,
                 pcp_ref, pcn_ref, yf_ref, wout_ref, dskip_ref, gluw_ref, glub_ref, gng_ref, gnb_ref, convw_ref,
                 bmh_ref, lng_ref, lnb_ref, o_ref):
    i = pl.program_id(0)
    f32 = lambda ref: ref[...].astype(F32)
    y_a = _s5_post(f32(ysf_ref), f32(ysb_ref), f32(pssm_ref), dskip_ref[...], gluw_ref[...], glub_ref[...])
    y_b = _rwkv_post(f32(yrf_ref) + f32(yrb_ref), f32(bnf_ref) + f32(bnb_ref), f32(g_ref), gng_ref[...],
                     gnb_ref[...], bmh_ref[...])
    pc = f32(pc_ref)
    wc = W_CONV
    prev_row, next_row = _halo_rows(pcp_ref, pcn_ref, i * tile, tile, seq)
    z = pc[:, wc:2 * wc] * pc[:, 2 * wc:]
    z_prev, z_next = _shift_rows(z, prev_row[:, wc:2 * wc] * prev_row[:, 2 * wc:],
                                 next_row[:, wc:2 * wc] * next_row[:, 2 * wc:])
    y_c = pc[:, :wc] * (convw_ref[0:1, :] * z_prev + convw_ref[1:2, :] * z + convw_ref[2:3, :] * z_next)
    mix = (_bdot(y_a, wout_ref[0:W_SSM, :]) + _bdot(y_b, wout_ref[W_SSM:W_SSM + W_RWKV, :])
           + _bdot(y_c, wout_ref[W_SSM + W_RWKV:W_SSM + W_RWKV + W_CONV, :])
           + _bdot(yf_ref[...], wout_ref[W_SSM + W_RWKV + W_CONV:, :]))
    o_ref[...] = _layer_norm(DEEPNORM_ALPHA * h_ref[...] + mix, lng_ref[...], lnb_ref[...])


def _mixout(h, seq, tile, y_s5, p_ssm, rwkv_outs, g, p_conv, y_fft, w_out_all, l, dskip, glu_w, glu_b, gn_g, gn_b,
            conv_w, ln_g, ln_b):
    n, d = h.shape
    row = lambda w: pl.BlockSpec((tile, w), lambda i: (i, 0))
    pcp, pcn = _halo_specs(tile, 3 * W_CONV, n, PACKED_ROWS)
    bmh16 = jnp.asarray(_rwkv_masks(RWKV_HEAD)[2]).astype(BF16)
    consts = (dskip.reshape(1, -1), glu_w.astype(BF16), glu_b.reshape(1, -1),
              gn_g.reshape(1, -1), gn_b.reshape(1, -1), conv_w, bmh16, ln_g.reshape(1, -1), ln_b.reshape(1, -1))

    def full(a):
        nd = a.ndim
        return pl.BlockSpec(a.shape, lambda i: (0,) * nd)

    return pl.pallas_call(
        functools.partial(_mixout_body, seq, tile),
        grid=(n // tile,),
        in_specs=[row(d)] + [row(W_SSM)] * 3 + [row(W_RWKV)] * 5
                 + [row(3 * W_CONV), pcp, pcn, row(W_FFT), _layer_spec(w_out_all, l)] + [full(c) for c in consts],
        out_specs=row(d),
        out_shape=jax.ShapeDtypeStruct((n, d), F32),
        compiler_params=_cparams(("parallel",)),
        name="mix_out",
    )(h, *y_s5, p_ssm, *rwkv_outs, g, p_conv, p_conv, p_conv, y_fft, w_out_all, *consts)


def _ffn_body(cols, h_ref, w1_ref, w3_ref, w2_ref, lng_ref, lnb_ref, o_ref, u_ref):
    x = h_ref[...].astype(BF16)
    dff = w1_ref.shape[1]
    for c0 in range(0, dff, cols):
        cs = slice(c0, min(c0 + cols, dff))
        a1 = jnp.dot(x, w1_ref[:, cs], preferred_element_type=F32)
        a3 = jnp.dot(x, w3_ref[:, cs], preferred_element_type=F32)
        u_ref[:, cs] = (jax.nn.silu(a1) * a3).astype(BF16)
    f = jnp.dot(u_ref[...], w2_ref[...], preferred_element_type=F32)
    o_ref[...] = _layer_norm(DEEPNORM_ALPHA * h_ref[...] + f, lng_ref[...], lnb_ref[...])


def _ffn(h, tile, cols, w1_all, w3_all, w2_all, l, ln_g, ln_b):
    n, d = h.shape
    dff = w1_all.shape[2]
    once = lambda shape: pl.BlockSpec(shape, lambda i: (0, 0), pipeline_mode=pl.Buffered(1))
    layer = lambda w: _layer_spec(w, l, pipeline_mode=pl.Buffered(1))
    return pl.pallas_call(
        functools.partial(_ffn_body, cols),
        grid=(n // tile,),
        in_specs=[pl.BlockSpec((tile, d), lambda i: (i, 0)), layer(w1_all), layer(w3_all), layer(w2_all),
                  once((1, d)), once((1, d))],
        out_specs=pl.BlockSpec((tile, d), lambda i: (i, 0)),
        out_shape=jax.ShapeDtypeStruct((n, d), F32),
        scratch_shapes=[pltpu.VMEM((tile, dff), BF16)],
        compiler_params=_cparams(("parallel",)),
        name="ffn",
    )(h, w1_all, w3_all, w2_all, ln_g.reshape(1, d), ln_b.reshape(1, d))


ROW_TILE = 512
S5_TILE_STEPS = 64
RWKV_CHUNK = 64
RWKV_CHUNKS_PER_STEP = 4
FFN_COLS = 256


def kernel(x, ln0_g, ln0_b, w_in, s5_lambda_re, s5_lambda_im, s5_log_dt, s5_b_re, s5_b_im, s5_c_re, s5_c_im, s5_d,
           s5_glu_w, s5_glu_b, rwkv_mu_rkv, rwkv_mu_w, rwkv_mu_a, rwkv_mu_g, rwkv_w0, rwkv_w1, rwkv_w2, rwkv_a0,
           rwkv_a1, rwkv_a2, rwkv_g1, rwkv_g2, rwkv_k_k, rwkv_k_a, rwkv_r_k, rwkv_gn_g, rwkv_gn_b, conv_w, w_out,
           ln1_g, ln1_b, ffn_w1, ffn_w3, ffn_w2, ln2_g, ln2_b):
    bsz, seq, d = x.shape
    n = bsz * seq
    tile = min(ROW_TILE, seq)
    s5_tl = min(S5_TILE_STEPS, seq // S5_SEGS // 2)
    fft_tables = _fft_tables(seq)
    w_in16, w_out16 = w_in.astype(BF16), w_out.astype(BF16)
    ffn_w116, ffn_w316, ffn_w216 = ffn_w1.astype(BF16), ffn_w3.astype(BF16), ffn_w2.astype(BF16)
    h = x.reshape(n, d)
    for l in range(w_in.shape[0]):
        outs = _inproj(h, seq, tile, w_in16, l, rwkv_mu_w[l], rwkv_mu_a[l], rwkv_mu_g[l], rwkv_w1[l], rwkv_w2[l],
                       rwkv_w0[l], rwkv_a1[l], rwkv_a2[l], rwkv_a0[l], rwkv_g1[l], rwkv_g2[l],
                       entry_ln=(ln0_g, ln0_b) if l == 0 else None)
        p_ssm, p_rkv, p_conv, p_fft, logw, agate, g = outs[:7]
        if l == 0:
            h = outs[7]
        bb, a, cc = _s5_params(s5_lambda_re[l], s5_lambda_im[l], s5_log_dt[l], s5_b_re[l], s5_b_im[l],
                               s5_c_re[l], s5_c_im[l])
        y_s5 = _s5_scan(p_ssm, bsz, seq, bb, a, cc, s5_tl)
        rwkv_outs = _rwkv_scan(p_rkv, logw, agate, bsz, seq, RWKV_CHUNK, RWKV_CHUNKS_PER_STEP, rwkv_mu_rkv[l], rwkv_k_k[l],
                               rwkv_k_a[l], rwkv_r_k[l].reshape(-1))
        y_fft = _fourier(p_fft, bsz, seq, fft_tables)
        h = _mixout(h, seq, tile, y_s5, p_ssm, rwkv_outs, g, p_conv, y_fft, w_out16, l, s5_d[l], s5_glu_w[l],
                    s5_glu_b[l], rwkv_gn_g[l], rwkv_gn_b[l], conv_w[l], ln1_g[l], ln1_b[l])
        h = _ffn(h, tile, FFN_COLS, ffn_w116, ffn_w316, ffn_w216, l, ln2_g[l], ln2_b[l])
    return h.reshape(bsz, seq, d)
```

```python
import functools
import math

import jax
import jax.numpy as jnp
import numpy as np
from jax import lax
from jax.experimental import pallas as pl
from jax.experimental.pallas import tpu as pltpu

W_SSM = 256
W_RWKV = 256
W_CONV = 256
W_FFT = 256
SSM_CH = 16
SSM_GROUPS = 16
SSM_STATE = 64
RWKV_HEAD = 64
RWKV_HEADS = 4
RWKV_PAIR = 2 * RWKV_HEAD
FFT_GROUPS = 4
FFT_CH = 64
RWKV_DECAY_SCALE = math.exp(-0.5)
RWKV_GN_EPS = 64e-5
LN_EPS = 1e-5
DEPTH = 2
DEEPNORM_ALPHA = (2 * DEPTH) ** 0.25

SUBLANES = 8
PACKED_ROWS = 16
VMEM_LIMIT = 48 * 1024 * 1024

BF16 = jnp.bfloat16
F32 = jnp.float32


def _cparams(sem):
    return pltpu.CompilerParams(dimension_semantics=sem, vmem_limit_bytes=VMEM_LIMIT)


def _bdot(a, b):
    return jnp.dot(a.astype(BF16), b.astype(BF16), preferred_element_type=F32)


def _layer_norm(x, g, b):
    mu = jnp.mean(x, axis=-1, keepdims=True)
    xc = x - mu
    var = jnp.mean(xc * xc, axis=-1, keepdims=True)
    return xc * lax.rsqrt(var + LN_EPS) * g + b


def _shift_rows(x, prev_row, next_row):
    n = x.shape[0]
    rows = lax.broadcasted_iota(jnp.int32, x.shape, 0)
    x_prev = jnp.where(rows == 0, prev_row, pltpu.roll(x, 1, axis=0))
    x_next = jnp.where(rows == n - 1, next_row, pltpu.roll(x, n - 1, axis=0))
    return x_prev, x_next


def _halo_rows(prev_ref, next_ref, row0, n_rows, seq):
    first = (row0 % seq) == 0
    last = ((row0 + n_rows) % seq) == 0
    hb = prev_ref.shape[0]
    prev_row = jnp.where(first, 0.0, prev_ref[hb - 1:hb, :].astype(F32))
    next_row = jnp.where(last, 0.0, next_ref[0:1, :].astype(F32))
    return prev_row, next_row


def _layer_spec(stacked, l, **kw):
    return pl.BlockSpec((None,) + stacked.shape[1:], lambda i: (l, 0, 0), **kw)


def _halo_specs(tile, width, n_rows_total, hb):
    per = tile // hb
    last_blk = n_rows_total // hb - 1
    prev = pl.BlockSpec((hb, width), lambda i: (jnp.maximum(i * per - 1, 0), 0))
    nxt = pl.BlockSpec((hb, width), lambda i: (jnp.minimum((i + 1) * per, last_blk), 0))
    return prev, nxt


def _inproj_body(seq, tile, entry_ln, *refs):
    h_ref, hp_ref, hn_ref = refs[:3]
    k = 5 if entry_ln else 3
    (win_ref, muw_ref, mua_ref, mug_ref, w1_ref, w2_ref, w0_ref, a1_ref, a2_ref, a0_ref, g1_ref,
     g2_ref) = refs[k:k + 12]
    pssm_ref, prkv_ref, pconv_ref, pfft_ref, logw_ref, agate_ref, g_ref = refs[k + 12:k + 19]
    i = pl.program_id(0)
    h, h_before, h_after = h_ref[...], hp_ref[...], hn_ref[...]
    if entry_ln:
        ln = lambda v: _layer_norm(v, refs[3][...], refs[4][...])
        h, h_before, h_after = ln(h), ln(h_before), ln(h_after)
        refs[k + 19][...] = h
    prev_row, next_row = _halo_rows(h_before, h_after, i * tile, tile, seq)
    x_prev, x_next = _shift_rows(h, prev_row, next_row)

    h16 = h.astype(BF16)

    def main():
        col = 0
        for ref in (pssm_ref, prkv_ref, pconv_ref, pfft_ref):
            width = ref.shape[1]
            ref[...] = jnp.dot(h16, win_ref[:, col:col + width], preferred_element_type=F32).astype(ref.dtype)
            col += width
            yield

    def lora(d, x_sh):
        dx = x_sh - h
        t1 = _bdot(h + dx * muw_ref[d:d + 1, :], w1_ref[d])
        yield
        w_lora = _bdot(jnp.tanh(t1), w2_ref[d])
        t2 = _bdot(h + dx * mua_ref[d:d + 1, :], a1_ref[d])
        yield
        logw_ref[d] = -RWKV_DECAY_SCALE * jax.nn.sigmoid(w0_ref[d:d + 1, :] + w_lora)
        a_lora = _bdot(t2, a2_ref[d])
        yield
        agate_ref[d] = jax.nn.sigmoid(a0_ref[d:d + 1, :] + a_lora).astype(agate_ref.dtype)

    def gate():
        xg = h + (0.5 * (x_prev + x_next) - h) * mug_ref[...]
        t3 = _bdot(xg, g1_ref[...])
        yield
        g_ref[...] = _bdot(jax.nn.sigmoid(t3), g2_ref[...]).astype(g_ref.dtype)

    _run_lockstep([main(), lora(0, x_prev), lora(1, x_next), gate()])


def _inproj(h, seq, tile, w_in_all, l, mu_w, mu_a, mu_g, w1, w2, w0, a1, a2, a0, g1, g2, entry_ln=None):
    n, d = h.shape
    ln_args = [] if entry_ln is None else [p.reshape(1, d) for p in entry_ln]
    hp_spec, hn_spec = _halo_specs(tile, d, n, SUBLANES)

    def full(a):
        nd = a.ndim
        return pl.BlockSpec(a.shape, lambda i: (0,) * nd)

    row = lambda w: pl.BlockSpec((tile, w), lambda i: (i, 0))
    row2 = lambda w: pl.BlockSpec((2, tile, w), lambda i: (0, i, 0))
    consts = (mu_w, mu_a, mu_g.reshape(1, d), w1.astype(BF16), w2.astype(BF16), w0,
              a1.astype(BF16), a2.astype(BF16), a0, g1.astype(BF16), g2.astype(BF16))
    return pl.pallas_call(
        functools.partial(_inproj_body, seq, tile, entry_ln is not None),
        grid=(n // tile,),
        in_specs=[pl.BlockSpec((tile, d), lambda i: (i, 0)), hp_spec, hn_spec] + [full(c) for c in ln_args]
                 + [_layer_spec(w_in_all, l)] + [full(c) for c in consts],
        out_specs=[row(W_SSM), row(3 * W_RWKV), row(3 * W_CONV), row(W_FFT),
                   row2(W_RWKV), row2(W_RWKV), row(W_RWKV)] + [row(d)] * len(ln_args[:1]),
        out_shape=[jax.ShapeDtypeStruct((n, W_SSM), BF16),
                   jax.ShapeDtypeStruct((n, 3 * W_RWKV), BF16),
                   jax.ShapeDtypeStruct((n, 3 * W_CONV), BF16),
                   jax.ShapeDtypeStruct((n, W_FFT), F32),
                   jax.ShapeDtypeStruct((2, n, W_RWKV), F32),
                   jax.ShapeDtypeStruct((2, n, W_RWKV), BF16),
                   jax.ShapeDtypeStruct((n, W_RWKV), BF16)] + [jax.ShapeDtypeStruct((n, d), F32)] * len(ln_args[:1]),
        compiler_params=_cparams(("parallel",)),
        name="inproj",
    )(h, h, h, *ln_args, w_in_all, *consts)


S5_SEGS = SUBLANES


def _cmul(ar, ai, br, bi):
    return ar * br - ai * bi, ar * bi + ai * br


class _S5Dir:
    def __init__(self, fwd, tl, ucur_ref, unext_ref, y_ref, bua_ref, bub_ref, st_ref, perm_ref, permt_ref, bb_ref,
                 a_ref, c_ref):
        self.fwd, self.tl = fwd, tl
        self.ucur_ref, self.unext_ref, self.y_ref = ucur_ref, unext_ref, y_ref
        self.bua_ref, self.bub_ref, self.st_ref = bua_ref, bub_ref, st_ref
        self.perm_ref, self.permt_ref, self.bb_ref, self.c_ref = perm_ref, permt_ref, bb_ref, c_ref
        self.half = SSM_GROUPS * SSM_STATE
        self.rows = tl * S5_SEGS
        self.n_parts = S5_SEGS
        self.part = 2 * self.half // self.n_parts
        shape = (S5_SEGS, self.half)
        self.ar = jnp.broadcast_to(a_ref[:, :self.half], shape)
        self.ai = jnp.broadcast_to(a_ref[:, self.half:], shape)
        self.off_a, self.off_b = (0, tl) if fwd else (tl, 0)

    def project(self, u_ref, off, buf_ref):
        u = u_ref[0, :, off:off + self.tl, :].reshape(self.rows, W_SSM).astype(BF16)
        up = jnp.dot(self.perm_ref[...], u, preferred_element_type=F32).astype(BF16)
        for k in range(self.n_parts):
            cols = slice(k * self.part, (k + 1) * self.part)
            buf_ref[:, cols] = jnp.dot(up, self.bb_ref[:, cols], preferred_element_type=F32)
            yield

    def scan(self, buf_ref, store):
        half, tl = self.half, self.tl
        xr, xi = self.st_ref[:, :half], self.st_ref[:, half:]
        for t in range(tl):
            row = (t if self.fwd else tl - 1 - t) * S5_SEGS
            nr = self.ar * xr - self.ai * xi + buf_ref[row:row + S5_SEGS, :half]
            ni = self.ar * xi + self.ai * xr + buf_ref[row:row + S5_SEGS, half:]
            if store:
                buf_ref[row:row + S5_SEGS, :half] = nr
                buf_ref[row:row + S5_SEGS, half:] = ni
            xr, xi = nr, ni
            if (t + 1) % (tl // self.n_parts) == 0:
                yield
        self.st_ref[:, :half] = xr
        self.st_ref[:, half:] = xi

    def emit(self, buf_ref, off):
        y = None
        for k in range(self.n_parts):
            cols = slice(k * self.part, (k + 1) * self.part)
            yk = _bdot(buf_ref[:, cols], self.c_ref[cols, :])
            y = yk if y is None else y + yk
            yield
        y = jnp.dot(self.permt_ref[...], y.astype(BF16), preferred_element_type=F32)
        self.y_ref[0, :, off:off + self.tl, :] = y.reshape(S5_SEGS, self.tl, W_SSM).astype(self.y_ref.dtype)

    def init_state(self, seg_len):
        half = self.half
        pr, pi = jnp.ones_like(self.ar), jnp.zeros_like(self.ar)
        br, bi = self.ar, self.ai
        e = seg_len
        while e:
            if e & 1:
                pr, pi = _cmul(pr, pi, br, bi)
            br, bi = _cmul(br, bi, br, bi)
            e >>= 1
        er, ei = self.st_ref[:, :half], self.st_ref[:, half:]
        zero = jnp.zeros((1, half), F32)
        order = range(S5_SEGS) if self.fwd else range(S5_SEGS - 1, -1, -1)
        cr, ci, out_r, out_i = zero, zero, {}, {}
        for j in order:
            out_r[j], out_i[j] = cr, ci
            nr, ni = _cmul(pr[0:1], pi[0:1], cr, ci)
            cr, ci = nr + er[j:j + 1], ni + ei[j:j + 1]
        self.st_ref[:, :half] = jnp.concatenate([out_r[j] for j in range(S5_SEGS)], axis=0)
        self.st_ref[:, half:] = jnp.concatenate([out_i[j] for j in range(S5_SEGS)], axis=0)


def _s5_body(tl, seg_len, ucf_ref, unf_ref, ucb_ref, unb_ref, perm_ref, permt_ref, bb_ref, a_ref, c_ref, yf_ref,
             yb_ref, baf_ref, bbf_ref, bab_ref, bbb_ref, st_ref):
    ps = pl.program_id(1)
    i = pl.program_id(2)
    dirs = [_S5Dir(True, tl, ucf_ref, unf_ref, yf_ref, baf_ref, bbf_ref, st_ref.at[0], perm_ref, permt_ref,
                   bb_ref.at[0], a_ref.at[0], c_ref.at[0]),
            _S5Dir(False, tl, ucb_ref, unb_ref, yb_ref, bab_ref, bbb_ref, st_ref.at[1], perm_ref, permt_ref,
                   bb_ref.at[1], a_ref.at[1], c_ref.at[1])]

    @pl.when(jnp.logical_and(i == 0, ps == 0))
    def _():
        st_ref[...] = jnp.zeros_like(st_ref)

    @pl.when(jnp.logical_and(i == 0, ps == 1))
    def _():
        for z in dirs:
            z.init_state(seg_len)

    @pl.when(i == 0)
    def _():
        _run_lockstep([z.project(z.ucur_ref, z.off_a, z.bua_ref) for z in dirs])

    @pl.when(ps == 0)
    def _():
        _run_lockstep([g for z in dirs for g in (z.scan(z.bua_ref, False), z.project(z.ucur_ref, z.off_b, z.bub_ref))])
        _run_lockstep([g for z in dirs for g in (z.scan(z.bub_ref, False), z.project(z.unext_ref, z.off_a, z.bua_ref))])

    @pl.when(ps == 1)
    def _():
        _run_lockstep([g for z in dirs for g in (z.scan(z.bua_ref, True), z.project(z.ucur_ref, z.off_b, z.bub_ref))])
        _run_lockstep([g for z in dirs for g in (z.scan(z.bub_ref, True), z.emit(z.bua_ref, z.off_a),
                                                 z.project(z.unext_ref, z.off_a, z.bua_ref))])
        _run_lockstep([z.emit(z.bub_ref, z.off_b) for z in dirs])


def _s5_params(lam_re, lam_im, log_dt, b_re, b_im, c_re, c_im):
    g, p, hch = SSM_GROUPS, SSM_STATE, SSM_CH
    dt = jnp.exp(log_dt)[..., None]
    mag = jnp.exp(lam_re * dt)
    lb_re = mag * jnp.cos(lam_im * dt)
    lb_im = mag * jnp.sin(lam_im * dt)
    den = lam_re * lam_re + lam_im * lam_im
    nr = lb_re - 1.0
    coef_re = (nr * lam_re + lb_im * lam_im) / den
    coef_im = (lb_im * lam_re - nr * lam_im) / den
    bb_re = coef_re[..., None] * b_re - coef_im[..., None] * b_im
    bb_im = coef_re[..., None] * b_im + coef_im[..., None] * b_re
    eye = jnp.eye(g, dtype=F32)
    bd_in = lambda m: jnp.einsum('dgph,gk->dghkp', m, eye).reshape(2, g * hch, g * p)
    bb = jnp.concatenate([bd_in(bb_re), bd_in(bb_im)], axis=-1)
    bd_out = lambda m: jnp.einsum('dghp,gk->dgpkh', m, eye).reshape(2, g * p, g * hch)
    cc = jnp.concatenate([bd_out(c_re), -bd_out(c_im)], axis=1)
    a = jnp.concatenate([lb_re.reshape(2, 1, g * p), lb_im.reshape(2, 1, g * p)], axis=-1)
    return bb, a, cc


def _s5_scan(p_ssm, bsz, seq, bb, a, cc, tl):
    seg_len = seq // S5_SEGS
    n_pairs = seg_len // (2 * tl)
    rows = tl * S5_SEGS
    w2 = 2 * SSM_GROUPS * SSM_STATE
    src = (np.arange(rows) % S5_SEGS) * tl + np.arange(rows) // S5_SEGS
    perm = np.zeros((rows, rows), np.float32)
    perm[np.arange(rows), src] = 1.0
    last = n_pairs - 1
    u4 = p_ssm.reshape(bsz, S5_SEGS, seg_len, W_SSM)
    ublk = (1, S5_SEGS, 2 * tl, W_SSM)
    const = lambda shape: pl.BlockSpec(shape, lambda b, ps, i: (0,) * len(shape))
    y_f, y_b = pl.pallas_call(
        functools.partial(_s5_body, tl, seg_len),
        grid=(bsz, 2, n_pairs),
        in_specs=[pl.BlockSpec(ublk, lambda b, ps, i: (b, 0, i, 0)),
                  pl.BlockSpec(ublk, lambda b, ps, i: (b, 0, jnp.minimum(i + 1, last), 0)),
                  pl.BlockSpec(ublk, lambda b, ps, i: (b, 0, last - i, 0)),
                  pl.BlockSpec(ublk, lambda b, ps, i: (b, 0, jnp.maximum(last - i - 1, 0), 0)),
                  const((rows, rows)), const((rows, rows)), const((2, W_SSM, w2)), const((2, 1, w2)),
                  const((2, w2, W_SSM))],
        out_specs=[pl.BlockSpec(ublk, lambda b, ps, i: (b, 0, jnp.where(ps == 0, 0, i), 0)),
                   pl.BlockSpec(ublk, lambda b, ps, i: (b, 0, jnp.where(ps == 0, last, last - i), 0))],
        out_shape=[jax.ShapeDtypeStruct((bsz, S5_SEGS, seg_len, W_SSM), BF16)] * 2,
        scratch_shapes=[pltpu.VMEM((rows, w2), F32)] * 4 + [pltpu.VMEM((2, S5_SEGS, w2), F32)],
        compiler_params=_cparams(("arbitrary",) * 3),
        name="s5_scan",
    )(u4, u4, u4, u4, jnp.asarray(perm).astype(BF16), jnp.asarray(perm.T).astype(BF16), bb.astype(BF16), a,
      cc.astype(BF16))
    return y_f.reshape(bsz * seq, W_SSM), y_b.reshape(bsz * seq, W_SSM)


def _gelu_tanh(x):
    c = math.sqrt(2.0 / math.pi)
    return 0.5 * x * (1.0 + jnp.tanh(c * (x + 0.044715 * (x * x * x))))


def _s5_post(y_f, y_b, u, dskip, glu_w, glu_b):
    y = _gelu_tanh(y_f.astype(F32) + y_b.astype(F32) + dskip * u)
    return y * jax.nn.sigmoid(_bdot(y, glu_w) + glu_b)


def _split_dot(x, w):
    hi = x.astype(BF16)
    lo = (x - hi.astype(F32)).astype(BF16)
    return (jnp.dot(hi, w, preferred_element_type=F32) + jnp.dot(lo, w, preferred_element_type=F32))


def _dot_nt(a, b):
    return lax.dot_general(a.astype(BF16), b.astype(BF16), (((1,), (1,)), ((), ())),
                           preferred_element_type=F32)


def _rwkv_masks(chunk):
    hh = RWKV_HEADS
    r = np.arange(hh * chunk)[:, None] // chunk
    bm_feat = (r == np.arange(W_RWKV)[None, :] // RWKV_HEAD).astype(np.float32)
    bm_time = (r == np.arange(hh * chunk)[None, :] // chunk).astype(np.float32)
    f = np.arange(W_RWKV)
    bm_head = (f[:, None] // RWKV_HEAD == f[None, :] // RWKV_HEAD).astype(np.float32)
    return bm_feat, bm_time, bm_head


def _rwkv_chunk(fwd, x, edge_row, logw, a, mu, k_k, k_a, r_k, bmf16, bmt16, bmh):
    ll = x.shape[0]
    w = W_RWKV
    rows = lax.broadcasted_iota(jnp.int32, x.shape, 0)
    if fwd:
        shifted = jnp.where(rows == 0, edge_row, pltpu.roll(x, 1, axis=0))
    else:
        shifted = jnp.where(rows == ll - 1, edge_row, pltpu.roll(x, ll - 1, axis=0))
    rkv = x + (shifted - x) * mu
    r, k, v = rkv[:, :w], rkv[:, w:2 * w], rkv[:, 2 * w:]
    bmh16 = bmh.astype(BF16)

    kk = k * k_k
    k2 = k * (1.0 + (a - 1.0) * k_a)
    both = jnp.concatenate([kk * kk, r * k2 * r_k], axis=0)
    hi = both.astype(BF16)
    lo = (both - hi.astype(F32)).astype(BF16)
    sums = jnp.dot(jnp.concatenate([hi, lo], axis=0), bmh16, preferred_element_type=F32)
    sums = sums[:2 * ll] + sums[2 * ll:]
    ksq = sums[:ll]
    bonus = sums[ll:] * v

    ti = lax.broadcasted_iota(jnp.int32, (ll, ll), 0)
    si = lax.broadcasted_iota(jnp.int32, (ll, ll), 1)
    tri = ((si <= ti) if fwd else (si >= ti)).astype(F32).astype(BF16)
    lhi = logw.astype(BF16)
    llo = (logw - lhi.astype(F32)).astype(BF16)
    cum2 = jnp.dot(tri, jnp.concatenate([lhi, llo], axis=1), preferred_element_type=F32)
    cum = cum2[:, :w] + cum2[:, w:]
    yield
    kk = kk * lax.rsqrt(ksq + 1e-12)
    ctot = jnp.sum(logw, axis=0, keepdims=True)
    e_neg = jnp.exp(-cum)
    e_rem = jnp.exp(ctot - cum)
    ah = -kk * jnp.exp(cum - logw)
    rh = r * jnp.exp(cum)
    bvec = kk * a
    bh, kh = bvec * e_neg, k2 * e_neg
    bt, kt = bvec * e_rem, k2 * e_rem

    def bd(m):
        m16 = m.astype(BF16)
        return jnp.concatenate([m16] * RWKV_HEADS, axis=0) * (bmf16 if m.shape[1] == w else bmt16)

    gram = _dot_nt(jnp.concatenate([ah, rh], axis=0), jnp.concatenate([bd(bh), bd(kh)], axis=0))
    yield
    l4 = RWKV_HEADS * ll
    tt = lax.broadcasted_iota(jnp.int32, (ll, l4), 0)
    ss = lax.broadcasted_iota(jnp.int32, (ll, l4), 1) % ll
    strict = (ss < tt) if fwd else (ss > tt)
    incl = (ss <= tt) if fwd else (ss >= tt)
    n_ab = jnp.where(strict, gram[:ll, :l4], 0.0)
    a_ak = jnp.where(strict, gram[:ll, l4:], 0.0)
    a_rb = jnp.where(incl, gram[ll:, :l4], 0.0)
    a_rk = jnp.where(incl, gram[ll:, l4:], 0.0)

    pw = n_ab
    tinv = jnp.where(ss == tt, 1.0, 0.0) + n_ab
    akv = _bdot(a_ak, bd(v))
    for _ in range(int(math.log2(ll)) - 1):
        pw = _bdot(pw, bd(pw))
        yield
        tinv = tinv + _bdot(tinv, bd(pw))
        yield

    ta = _bdot(tinv, jnp.concatenate([bd(ah), bd(akv)], axis=1))
    yield
    ap, wm = ta[:, :w], ta[:, w:]
    rp = rh + _bdot(a_rb, bd(ap))
    y0 = _bdot(jnp.concatenate([a_rb, a_rk], axis=1), jnp.concatenate([bd(wm), bd(v)], axis=0))
    btk = jnp.transpose(jnp.concatenate([bt, kt], axis=0))
    hp = RWKV_PAIR
    pqs = []
    for cs in (slice(0, hp), slice(hp, w)):
        rhs = jnp.concatenate([jnp.concatenate([ap[:, cs], wm[:, cs]], axis=1),
                               jnp.concatenate([jnp.zeros_like(v[:, cs]), v[:, cs]], axis=1)], axis=0)
        pqs.append(_bdot(btk[cs, :], rhs))
    yield
    eye = (lax.broadcasted_iota(jnp.int32, (hp, hp), 0) == lax.broadcasted_iota(jnp.int32, (hp, hp), 1))
    bmp = bmh[:hp, :hp]
    decay = jnp.exp(ctot)
    pm = [pq[:, :hp] * bmp + jnp.where(eye, decay[:, cs], 0.0) for pq, cs in zip(pqs, (slice(0, hp), slice(hp, w)))]
    qm = [pq[:, hp:] * bmp for pq in pqs]
    yield rp, y0, pm, qm, bonus


def _run_lockstep(gens):
    results = [None] * len(gens)
    live = list(range(len(gens)))
    while live:
        for s in list(live):
            try:
                out = next(gens[s])
            except StopIteration:
                live.remove(s)
            else:
                if out is not None:
                    results[s] = out
    return results


def _rwkv_body(chunk, nsub, nb, *refs):
    dir_refs = [refs[0:4], refs[4:8]]
    mu_ref, kk_ref, ka_ref, rk_ref, bmf_ref, bmt_ref, bmh_ref = refs[8:15]
    y_refs, bonus_refs, h_ref = refs[15:17], refs[17:19], refs[19]
    c = pl.program_id(0)

    @pl.when(c == 0)
    def _():
        h_ref[...] = jnp.zeros_like(h_ref)

    gens, where = [], []
    for d in range(2):
        x_ref, edge_ref, logw_ref, a_ref = dir_refs[d]
        order = range(nsub) if d == 0 else range(nsub - 1, -1, -1)
        for b in range(nb):
            x = x_ref[b].astype(F32)
            for n, j in enumerate(order):
                rows = slice(j * chunk, (j + 1) * chunk)
                if n == 0:
                    hb = edge_ref.shape[1]
                    edge = edge_ref[b, hb - 1:hb, :] if d == 0 else edge_ref[b, 0:1, :]
                    edge = jnp.where(c == 0, 0.0, edge.astype(F32))
                else:
                    e = j * chunk - 1 if d == 0 else (j + 1) * chunk
                    edge = x[e:e + 1, :]
                gens.append(_rwkv_chunk(d == 0, x[rows], edge, logw_ref[0, b, rows, :],
                                        a_ref[0, b, rows, :].astype(F32), mu_ref[d], kk_ref[...], ka_ref[...],
                                        rk_ref[...], bmf_ref[...], bmt_ref[...], bmh_ref[...]))
                where.append((d, b, rows))
    parts = _run_lockstep(gens)

    def carry(s):
        hp = RWKV_PAIR
        h = [h_ref[s, 0], h_ref[s, 1]]
        for n in range(nsub):
            rp, y0, pm, qm, bonus = parts[s * nsub + n]
            d, b, rows = where[s * nsub + n]
            y = jnp.concatenate([_bdot(rp[:, :hp], h[0]), _bdot(rp[:, hp:], h[1])], axis=1) + y0
            y_refs[d][b, rows, :] = y.astype(y_refs[d].dtype)
            bonus_refs[d][b, rows, :] = bonus.astype(bonus_refs[d].dtype)
            h = [_bdot(pm[0], h[0]) + qm[0], _bdot(pm[1], h[1]) + qm[1]]
            yield
        h_ref[s, 0] = h[0]
        h_ref[s, 1] = h[1]

    _run_lockstep([carry(s) for s in range(2 * nb)])


def _rwkv_scan(p_rkv, logw, agate, bsz, seq, chunk, nsub, mu_rkv, k_k, k_a, r_k):
    n = bsz * seq
    blk = chunk * nsub
    n_chunks = seq // blk
    hb = PACKED_ROWS
    per = blk // hb
    w3 = 3 * W_RWKV
    bmf, bmt, bmh = _rwkv_masks(chunk)
    consts = (mu_rkv.reshape(2, 1, w3), k_k.reshape(1, -1), k_a.reshape(1, -1), r_k.reshape(1, -1),
              jnp.asarray(bmf).astype(BF16), jnp.asarray(bmt).astype(BF16), jnp.asarray(bmh))
    x3 = p_rkv.reshape(bsz, seq, w3)
    logw4 = logw.reshape(2, bsz, seq, W_RWKV)
    a4 = agate.reshape(2, bsz, seq, W_RWKV)
    args, in_specs, out_specs = [], [], []
    for d in range(2):
        cidx = (lambda c: c) if d == 0 else (lambda c: n_chunks - 1 - c)
        if d == 0:
            edge = lambda c: (0, jnp.maximum(c * per - 1, 0), 0)
        else:
            edge = lambda c: (0, jnp.minimum((n_chunks - c) * per, seq // hb - 1), 0)
        args += [x3, x3, logw4, a4]
        in_specs += [pl.BlockSpec((bsz, blk, w3), lambda c, cidx=cidx: (0, cidx(c), 0)),
                     pl.BlockSpec((bsz, hb, w3), edge),
                     pl.BlockSpec((1, bsz, blk, W_RWKV), lambda c, d=d, cidx=cidx: (d, 0, cidx(c), 0)),
                     pl.BlockSpec((1, bsz, blk, W_RWKV), lambda c, d=d, cidx=cidx: (d, 0, cidx(c), 0))]
        out_specs.append(pl.BlockSpec((bsz, blk, W_RWKV), lambda c, cidx=cidx: (0, cidx(c), 0)))

    def full(a):
        nd = a.ndim
        return pl.BlockSpec(a.shape, lambda c: (0,) * nd)

    y_f, y_b, bonus_f, bonus_b = pl.pallas_call(
        functools.partial(_rwkv_body, chunk, nsub, bsz),
        grid=(n_chunks,),
        in_specs=in_specs + [full(a) for a in consts],
        out_specs=out_specs + out_specs,
        out_shape=[jax.ShapeDtypeStruct((bsz, seq, W_RWKV), BF16)] * 4,
        scratch_shapes=[pltpu.VMEM((2 * bsz, W_RWKV // RWKV_PAIR, RWKV_PAIR, RWKV_PAIR), F32)],
        compiler_params=_cparams(("arbitrary",)),
        name="rwkv_scan",
    )(*args, *consts)
    return [a.reshape(n, W_RWKV) for a in (y_f, y_b, bonus_f, bonus_b)]


def _head_mean(x, bmh16):
    return _split_dot(x, bmh16) * (1.0 / RWKV_HEAD)


def _rwkv_post(y, bonus, g, gn_g, gn_b, bmh16):
    mu = _head_mean(y, bmh16)
    yc = y - mu
    var = _head_mean(yc * yc, bmh16)
    yn = yc * lax.rsqrt(var + RWKV_GN_EPS) * gn_g + gn_b
    return (yn + bonus) * g


FFT_R1 = 64


def _fft_tables(seq):
    r1, r2 = FFT_R1, seq // FFT_R1
    i1 = np.arange(r1)
    ang1 = ((i1[:, None] * i1[None, :]) % r1) * (2.0 * math.pi / r1)
    stage1 = np.concatenate([np.cos(ang1), -np.sin(ang1)], axis=0)
    k1 = i1[:, None, None]
    k2 = np.arange(r2)[None, :, None]
    n2 = np.arange(r2)[None, None, :]
    ang2 = ((n2 * (k1 + r1 * k2)) % seq) * (2.0 * math.pi / seq)
    mr, mi = np.cos(ang2), -np.sin(ang2)
    stage2 = np.concatenate([np.concatenate([mr, -mi], axis=2),
                             np.concatenate([mi, mr], axis=2)], axis=1)
    c = np.arange(W_FFT)
    same = (c[:, None] // FFT_CH) == (c[None, :] // FFT_CH)
    angc = ((c[:, None] * c[None, :]) % FFT_CH) * (2.0 * math.pi / FFT_CH)
    scale = 1.0 / math.sqrt(seq * FFT_CH)
    chan = np.concatenate([np.where(same, np.cos(angc), 0.0), np.where(same, np.sin(angc), 0.0)],
                          axis=0) * scale
    tn = SUBLANES
    big = np.zeros((tn, 2 * r1, r1, tn))
    for j in range(tn):
        big[j, :, :, j] = stage1
    stage1 = big.reshape(tn * 2 * r1, r1 * tn)
    return tuple(jnp.asarray(t, dtype=F32).astype(BF16) for t in (stage1, stage2, chan))


def _fft1_body(z_ref, m_ref, a_ref):
    r1, tn = FFT_R1, z_ref.shape[2]
    a = _bdot(m_ref[...], z_ref[0].reshape(r1 * tn, W_FFT))
    for j in range(tn):
        a_ref[0, 0, :, j, :] = a[(2 * j) * r1:(2 * j + 1) * r1]
        a_ref[0, 1, :, j, :] = a[(2 * j + 1) * r1:(2 * j + 2) * r1]


def _fft2_body(kb, r2, a_ref, m_ref, chan_ref, o_ref):
    for j in range(kb):
        x = jnp.concatenate([a_ref[0, 0, j], a_ref[0, 1, j]], axis=0)
        f = _bdot(m_ref[j], x)
        fri = jnp.concatenate([f[:r2], f[r2:]], axis=1)
        o_ref[0, :, j * W_FFT:(j + 1) * W_FFT] = _bdot(fri, chan_ref[...]).astype(o_ref.dtype)


def _fourier(p_fft, bsz, seq, tables):
    r1, r2 = FFT_R1, seq // FFT_R1
    stage1, stage2, chan = tables
    tn = SUBLANES
    a = pl.pallas_call(
        _fft1_body,
        grid=(bsz, r2 // tn),
        in_specs=[pl.BlockSpec((1, r1, tn, W_FFT), lambda b, j: (b, 0, j, 0)),
                  pl.BlockSpec((tn * 2 * r1, r1 * tn), lambda b, j: (0, 0))],
        out_specs=pl.BlockSpec((1, 2, r1, tn, W_FFT), lambda b, j: (b, 0, 0, j, 0)),
        out_shape=jax.ShapeDtypeStruct((bsz, 2, r1, r2, W_FFT), F32),
        compiler_params=_cparams(("parallel", "parallel")),
        name="fft_stage1",
    )(p_fft.reshape(bsz, r1, r2, W_FFT), stage1)
    kb = 8
    out = pl.pallas_call(
        functools.partial(_fft2_body, kb, r2),
        grid=(bsz, r1 // kb),
        in_specs=[pl.BlockSpec((1, 2, kb, r2, W_FFT), lambda b, j: (b, 0, j, 0, 0)),
                  pl.BlockSpec((kb, 2 * r2, 2 * r2), lambda b, j: (j, 0, 0)),
                  pl.BlockSpec((2 * W_FFT, W_FFT), lambda b, j: (0, 0))],
        out_specs=pl.BlockSpec((1, r2, kb * W_FFT), lambda b, j: (b, 0, j)),
        out_shape=jax.ShapeDtypeStruct((bsz, r2, r1 * W_FFT), BF16),
        compiler_params=_cparams(("parallel", "parallel")),
        name="fft_stage2",
    )(a, stage2, chan)
    return out.reshape(bsz * seq, W_FFT)


def _mixout_body(seq, tile, h_ref, ysf_ref, ysb_ref, pssm_ref, yrf_ref, yrb_ref, bnf_ref, bnb_ref, g_ref, pc_ref,
                 pcp_ref, pcn_ref, yf_ref, wout_ref, dskip_ref, gluw_ref, glub_ref, gng_ref, gnb_ref, convw_ref,
                 bmh_ref, lng_ref, lnb_ref, o_ref):
    i = pl.program_id(0)
    f32 = lambda ref: ref[...].astype(F32)
    y_a = _s5_post(f32(ysf_ref), f32(ysb_ref), f32(pssm_ref), dskip_ref[...], gluw_ref[...], glub_ref[...])
    y_b = _rwkv_post(f32(yrf_ref) + f32(yrb_ref), f32(bnf_ref) + f32(bnb_ref), f32(g_ref), gng_ref[...],
                     gnb_ref[...], bmh_ref[...])
    pc = f32(pc_ref)
    wc = W_CONV
    prev_row, next_row = _halo_rows(pcp_ref, pcn_ref, i * tile, tile, seq)
    z = pc[:, wc:2 * wc] * pc[:, 2 * wc:]
    z_prev, z_next = _shift_rows(z, prev_row[:, wc:2 * wc] * prev_row[:, 2 * wc:],
                                 next_row[:, wc:2 * wc] * next_row[:, 2 * wc:])
    y_c = pc[:, :wc] * (convw_ref[0:1, :] * z_prev + convw_ref[1:2, :] * z + convw_ref[2:3, :] * z_next)
    mix = (_bdot(y_a, wout_ref[0:W_SSM, :]) + _bdot(y_b, wout_ref[W_SSM:W_SSM + W_RWKV, :])
           + _bdot(y_c, wout_ref[W_SSM + W_RWKV:W_SSM + W_RWKV + W_CONV, :])
           + _bdot(yf_ref[...], wout_ref[W_SSM + W_RWKV + W_CONV:, :]))
    o_ref[...] = _layer_norm(DEEPNORM_ALPHA * h_ref[...] + mix, lng_ref[...], lnb_ref[...])


def _mixout(h, seq, tile, y_s5, p_ssm, rwkv_outs, g, p_conv, y_fft, w_out_all, l, dskip, glu_w, glu_b, gn_g, gn_b,
            conv_w, ln_g, ln_b):
    n, d = h.shape
    row = lambda w: pl.BlockSpec((tile, w), lambda i: (i, 0))
    pcp, pcn = _halo_specs(tile, 3 * W_CONV, n, PACKED_ROWS)
    bmh16 = jnp.asarray(_rwkv_masks(RWKV_HEAD)[2]).astype(BF16)
    consts = (dskip.reshape(1, -1), glu_w.astype(BF16), glu_b.reshape(1, -1),
              gn_g.reshape(1, -1), gn_b.reshape(1, -1), conv_w, bmh16, ln_g.reshape(1, -1), ln_b.reshape(1, -1))

    def full(a):
        nd = a.ndim
        return pl.BlockSpec(a.shape, lambda i: (0,) * nd)

    return pl.pallas_call(
        functools.partial(_mixout_body, seq, tile),
        grid=(n // tile,),
        in_specs=[row(d)] + [row(W_SSM)] * 3 + [row(W_RWKV)] * 5
                 + [row(3 * W_CONV), pcp, pcn, row(W_FFT), _layer_spec(w_out_all, l)] + [full(c) for c in consts],
        out_specs=row(d),
        out_shape=jax.ShapeDtypeStruct((n, d), F32),
        compiler_params=_cparams(("parallel",)),
        name="mix_out",
    )(h, *y_s5, p_ssm, *rwkv_outs, g, p_conv, p_conv, p_conv, y_fft, w_out_all, *consts)


def _ffn_body(cols, h_ref, w1_ref, w3_ref, w2_ref, lng_ref, lnb_ref, o_ref, u_ref):
    x = h_ref[...].astype(BF16)
    dff = w1_ref.shape[1]
    for c0 in range(0, dff, cols):
        cs = slice(c0, min(c0 + cols, dff))
        a1 = jnp.dot(x, w1_ref[:, cs], preferred_element_type=F32)
        a3 = jnp.dot(x, w3_ref[:, cs], preferred_element_type=F32)
        u_ref[:, cs] = (jax.nn.silu(a1) * a3).astype(BF16)
    f = jnp.dot(u_ref[...], w2_ref[...], preferred_element_type=F32)
    o_ref[...] = _layer_norm(DEEPNORM_ALPHA * h_ref[...] + f, lng_ref[...], lnb_ref[...])


def _ffn(h, tile, cols, w1_all, w3_all, w2_all, l, ln_g, ln_b):
    n, d = h.shape
    dff = w1_all.shape[2]
    once = lambda shape: pl.BlockSpec(shape, lambda i: (0, 0), pipeline_mode=pl.Buffered(1))
    layer = lambda w: _layer_spec(w, l, pipeline_mode=pl.Buffered(1))
    return pl.pallas_call(
        functools.partial(_ffn_body, cols),
        grid=(n // tile,),
        in_specs=[pl.BlockSpec((tile, d), lambda i: (i, 0)), layer(w1_all), layer(w3_all), layer(w2_all),
                  once((1, d)), once((1, d))],
        out_specs=pl.BlockSpec((tile, d), lambda i: (i, 0)),
        out_shape=jax.ShapeDtypeStruct((n, d), F32),
        scratch_shapes=[pltpu.VMEM((tile, dff), BF16)],
        compiler_params=_cparams(("parallel",)),
        name="ffn",
    )(h, w1_all, w3_all, w2_all, ln_g.reshape(1, d), ln_b.reshape(1, d))


ROW_TILE = 512
S5_TILE_STEPS = 64
RWKV_CHUNK = 64
RWKV_CHUNKS_PER_STEP = 4
FFN_COLS = 256


def kernel(x, ln0_g, ln0_b, w_in, s5_lambda_re, s5_lambda_im, s5_log_dt, s5_b_re, s5_b_im, s5_c_re, s5_c_im, s5_d,
           s5_glu_w, s5_glu_b, rwkv_mu_rkv, rwkv_mu_w, rwkv_mu_a, rwkv_mu_g, rwkv_w0, rwkv_w1, rwkv_w2, rwkv_a0,
           rwkv_a1, rwkv_a2, rwkv_g1, rwkv_g2, rwkv_k_k, rwkv_k_a, rwkv_r_k, rwkv_gn_g, rwkv_gn_b, conv_w, w_out,
           ln1_g, ln1_b, ffn_w1, ffn_w3, ffn_w2, ln2_g, ln2_b):
    bsz, seq, d = x.shape
    n = bsz * seq
    tile = min(ROW_TILE, seq)
    s5_tl = min(S5_TILE_STEPS, seq // S5_SEGS // 2)
    fft_tables = _fft_tables(seq)
    w_in16, w_out16 = w_in.astype(BF16), w_out.astype(BF16)
    ffn_w116, ffn_w316, ffn_w216 = ffn_w1.astype(BF16), ffn_w3.astype(BF16), ffn_w2.astype(BF16)
    h = x.reshape(n, d)
    for l in range(w_in.shape[0]):
        outs = _inproj(h, seq, tile, w_in16, l, rwkv_mu_w[l], rwkv_mu_a[l], rwkv_mu_g[l], rwkv_w1[l], rwkv_w2[l],
                       rwkv_w0[l], rwkv_a1[l], rwkv_a2[l], rwkv_a0[l], rwkv_g1[l], rwkv_g2[l],
                       entry_ln=(ln0_g, ln0_b) if l == 0 else None)
        p_ssm, p_rkv, p_conv, p_fft, logw, agate, g = outs[:7]
        if l == 0:
            h = outs[7]
        bb, a, cc = _s5_params(s5_lambda_re[l], s5_lambda_im[l], s5_log_dt[l], s5_b_re[l], s5_b_im[l],
                               s5_c_re[l], s5_c_im[l])
        y_s5 = _s5_scan(p_ssm, bsz, seq, bb, a, cc, s5_tl)
        rwkv_outs = _rwkv_scan(p_rkv, logw, agate, bsz, seq, RWKV_CHUNK, RWKV_CHUNKS_PER_STEP, rwkv_mu_rkv[l], rwkv_k_k[l],
                               rwkv_k_a[l], rwkv_r_k[l].reshape(-1))
        y_fft = _fourier(p_fft, bsz, seq, fft_tables)
        h = _mixout(h, seq, tile, y_s5, p_ssm, rwkv_outs, g, p_conv, y_fft, w_out16, l, s5_d[l], s5_glu_w[l],
                    s5_glu_b[l], rwkv_gn_g[l], rwkv_gn_b[l], conv_w[l], ln1_g[l], ln1_b[l])
        h = _ffn(h, tile, FFN_COLS, ffn_w116, ffn_w316, ffn_w216, l, ln2_g[l], ln2_b[l])
    return h.reshape(bsz, seq, d)
```

```python
import functools
import math

import jax
import jax.numpy as jnp
import numpy as np
from jax import lax
from jax.experimental import pallas as pl
from jax.experimental.pallas import tpu as pltpu

W_SSM = 256
W_RWKV = 256
W_CONV = 256
W_FFT = 256
SSM_CH = 16
SSM_GROUPS = 16
SSM_STATE = 64
RWKV_HEAD = 64
RWKV_HEADS = 4
RWKV_PAIR = 2 * RWKV_HEAD
FFT_CH = 64
RWKV_DECAY_SCALE = math.exp(-0.5)
RWKV_GN_EPS = 64e-5
LN_EPS = 1e-5
DEPTH = 2
DEEPNORM_ALPHA = (2 * DEPTH) ** 0.25

SUBLANES = 8
PACKED_ROWS = 16
VMEM_LIMIT = 48 * 1024 * 1024

BF16 = jnp.bfloat16
F32 = jnp.float32


def _cparams(sem):
    return pltpu.CompilerParams(dimension_semantics=sem, vmem_limit_bytes=VMEM_LIMIT)


def _bdot(a, b):
    return jnp.dot(a.astype(BF16), b.astype(BF16), preferred_element_type=F32)


def _layer_norm(x, g, b):
    mu = jnp.mean(x, axis=-1, keepdims=True)
    xc = x - mu
    var = jnp.mean(xc * xc, axis=-1, keepdims=True)
    return xc * lax.rsqrt(var + LN_EPS) * g + b


def _shift_rows(x, prev_row, next_row):
    n = x.shape[0]
    rows = lax.broadcasted_iota(jnp.int32, x.shape, 0)
    x_prev = jnp.where(rows == 0, prev_row, pltpu.roll(x, 1, axis=0))
    x_next = jnp.where(rows == n - 1, next_row, pltpu.roll(x, n - 1, axis=0))
    return x_prev, x_next


def _halo_rows(prev_ref, next_ref, row0, n_rows, seq):
    first = (row0 % seq) == 0
    last = ((row0 + n_rows) % seq) == 0
    hb = prev_ref.shape[0]
    prev_row = jnp.where(first, 0.0, prev_ref[hb - 1:hb, :].astype(F32))
    next_row = jnp.where(last, 0.0, next_ref[0:1, :].astype(F32))
    return prev_row, next_row


def _layer_spec(stacked, l, **kw):
    return pl.BlockSpec((None,) + stacked.shape[1:], lambda i: (l, 0, 0), **kw)


def _halo_specs(tile, width, n_rows_total, hb):
    per = tile // hb
    last_blk = n_rows_total // hb - 1
    prev = pl.BlockSpec((hb, width), lambda i: (jnp.maximum(i * per - 1, 0), 0))
    nxt = pl.BlockSpec((hb, width), lambda i: (jnp.minimum((i + 1) * per, last_blk), 0))
    return prev, nxt


def _inproj_body(seq, tile, entry_ln, *refs):
    h_ref, hp_ref, hn_ref = refs[:3]
    k = 5 if entry_ln else 3
    (win_ref, muw_ref, mua_ref, mug_ref, w1_ref, w2_ref, w0_ref, a1_ref, a2_ref, a0_ref, g1_ref,
     g2_ref) = refs[k:k + 12]
    pssm_ref, prkv_ref, pconv_ref, pfft_ref, logw_ref, agate_ref, g_ref = refs[k + 12:k + 19]
    i = pl.program_id(0)
    h, h_before, h_after = h_ref[...], hp_ref[...], hn_ref[...]
    if entry_ln:
        ln = lambda v: _layer_norm(v, refs[3][...], refs[4][...])
        h, h_before, h_after = ln(h), ln(h_before), ln(h_after)
        refs[k + 19][...] = h
    prev_row, next_row = _halo_rows(h_before, h_after, i * tile, tile, seq)
    x_prev, x_next = _shift_rows(h, prev_row, next_row)

    h16 = h.astype(BF16)

    def main():
        col = 0
        for ref in (pssm_ref, prkv_ref, pconv_ref, pfft_ref):
            width = ref.shape[1]
            ref[...] = jnp.dot(h16, win_ref[:, col:col + width], preferred_element_type=F32).astype(ref.dtype)
            col += width
            yield

    def lora(d, x_sh):
        dx = x_sh - h
        t1 = _bdot(h + dx * muw_ref[d:d + 1, :], w1_ref[d])
        yield
        w_lora = _bdot(jnp.tanh(t1), w2_ref[d])
        t2 = _bdot(h + dx * mua_ref[d:d + 1, :], a1_ref[d])
        yield
        logw_ref[d] = -RWKV_DECAY_SCALE * jax.nn.sigmoid(w0_ref[d:d + 1, :] + w_lora)
        a_lora = _bdot(t2, a2_ref[d])
        yield
        agate_ref[d] = jax.nn.sigmoid(a0_ref[d:d + 1, :] + a_lora).astype(agate_ref.dtype)

    def gate():
        xg = h + (0.5 * (x_prev + x_next) - h) * mug_ref[...]
        t3 = _bdot(xg, g1_ref[...])
        yield
        g_ref[...] = _bdot(jax.nn.sigmoid(t3), g2_ref[...]).astype(g_ref.dtype)

    _run_lockstep([main(), lora(0, x_prev), lora(1, x_next), gate()])


def _inproj(h, seq, tile, w_in_all, l, mu_w, mu_a, mu_g, w1, w2, w0, a1, a2, a0, g1, g2, entry_ln=None):
    n, d = h.shape
    ln_args = [] if entry_ln is None else [p.reshape(1, d) for p in entry_ln]
    hp_spec, hn_spec = _halo_specs(tile, d, n, SUBLANES)

    def full(a):
        nd = a.ndim
        return pl.BlockSpec(a.shape, lambda i: (0,) * nd)

    row = lambda w: pl.BlockSpec((tile, w), lambda i: (i, 0))
    row2 = lambda w: pl.BlockSpec((2, tile, w), lambda i: (0, i, 0))
    consts = (mu_w, mu_a, mu_g.reshape(1, d), w1.astype(BF16), w2.astype(BF16), w0,
              a1.astype(BF16), a2.astype(BF16), a0, g1.astype(BF16), g2.astype(BF16))
    return pl.pallas_call(
        functools.partial(_inproj_body, seq, tile, entry_ln is not None),
        grid=(n // tile,),
        in_specs=[pl.BlockSpec((tile, d), lambda i: (i, 0)), hp_spec, hn_spec] + [full(c) for c in ln_args]
                 + [_layer_spec(w_in_all, l)] + [full(c) for c in consts],
        out_specs=[row(W_SSM), row(3 * W_RWKV), row(3 * W_CONV), row(W_FFT),
                   row2(W_RWKV), row2(W_RWKV), row(W_RWKV)] + [row(d)] * len(ln_args[:1]),
        out_shape=[jax.ShapeDtypeStruct((n, W_SSM), BF16),
                   jax.ShapeDtypeStruct((n, 3 * W_RWKV), BF16),
                   jax.ShapeDtypeStruct((n, 3 * W_CONV), BF16),
                   jax.ShapeDtypeStruct((n, W_FFT), F32),
                   jax.ShapeDtypeStruct((2, n, W_RWKV), F32),
                   jax.ShapeDtypeStruct((2, n, W_RWKV), BF16),
                   jax.ShapeDtypeStruct((n, W_RWKV), BF16)] + [jax.ShapeDtypeStruct((n, d), F32)] * len(ln_args[:1]),
        compiler_params=_cparams(("parallel",)),
        name="inproj",
    )(h, h, h, *ln_args, w_in_all, *consts)


S5_SEGS = SUBLANES


def _cmul(ar, ai, br, bi):
    return ar * br - ai * bi, ar * bi + ai * br


class _S5Dir:
    def __init__(self, fwd, tl, ucur_ref, unext_ref, y_ref, bua_ref, bub_ref, st_ref, perm_ref, permt_ref, bb_ref,
                 a_ref, c_ref):
        self.fwd, self.tl = fwd, tl
        self.ucur_ref, self.unext_ref, self.y_ref = ucur_ref, unext_ref, y_ref
        self.bua_ref, self.bub_ref, self.st_ref = bua_ref, bub_ref, st_ref
        self.perm_ref, self.permt_ref, self.bb_ref, self.c_ref = perm_ref, permt_ref, bb_ref, c_ref
        self.half = SSM_GROUPS * SSM_STATE
        self.rows = tl * S5_SEGS
        self.n_parts = S5_SEGS
        self.part = 2 * self.half // self.n_parts
        shape = (S5_SEGS, self.half)
        self.ar = jnp.broadcast_to(a_ref[:, :self.half], shape)
        self.ai = jnp.broadcast_to(a_ref[:, self.half:], shape)
        self.off_a, self.off_b = (0, tl) if fwd else (tl, 0)

    def project(self, u_ref, off, buf_ref):
        u = u_ref[0, :, off:off + self.tl, :].reshape(self.rows, W_SSM).astype(BF16)
        up = jnp.dot(self.perm_ref[...], u, preferred_element_type=F32).astype(BF16)
        for k in range(self.n_parts):
            cols = slice(k * self.part, (k + 1) * self.part)
            buf_ref[:, cols] = jnp.dot(up, self.bb_ref[:, cols], preferred_element_type=F32)
            yield

    def scan(self, buf_ref, store):
        half, tl = self.half, self.tl
        xr, xi = self.st_ref[:, :half], self.st_ref[:, half:]
        for t in range(tl):
            row = (t if self.fwd else tl - 1 - t) * S5_SEGS
            nr = self.ar * xr - self.ai * xi + buf_ref[row:row + S5_SEGS, :half]
            ni = self.ar * xi + self.ai * xr + buf_ref[row:row + S5_SEGS, half:]
            if store:
                buf_ref[row:row + S5_SEGS, :half] = nr
                buf_ref[row:row + S5_SEGS, half:] = ni
            xr, xi = nr, ni
            if (t + 1) % (tl // self.n_parts) == 0:
                yield
        self.st_ref[:, :half] = xr
        self.st_ref[:, half:] = xi

    def emit(self, buf_ref, off):
        y = None
        for k in range(self.n_parts):
            cols = slice(k * self.part, (k + 1) * self.part)
            yk = _bdot(buf_ref[:, cols], self.c_ref[cols, :])
            y = yk if y is None else y + yk
            yield
        y = jnp.dot(self.permt_ref[...], y.astype(BF16), preferred_element_type=F32)
        self.y_ref[0, :, off:off + self.tl, :] = y.reshape(S5_SEGS, self.tl, W_SSM).astype(self.y_ref.dtype)

    def init_state(self, seg_len):
        half = self.half
        pr, pi = jnp.ones_like(self.ar), jnp.zeros_like(self.ar)
        br, bi = self.ar, self.ai
        e = seg_len
        while e:
            if e & 1:
                pr, pi = _cmul(pr, pi, br, bi)
            br, bi = _cmul(br, bi, br, bi)
            e >>= 1
        er, ei = self.st_ref[:, :half], self.st_ref[:, half:]
        zero = jnp.zeros((1, half), F32)
        order = range(S5_SEGS) if self.fwd else range(S5_SEGS - 1, -1, -1)
        cr, ci, out_r, out_i = zero, zero, {}, {}
        for j in order:
            out_r[j], out_i[j] = cr, ci
            nr, ni = _cmul(pr[0:1], pi[0:1], cr, ci)
            cr, ci = nr + er[j:j + 1], ni + ei[j:j + 1]
        self.st_ref[:, :half] = jnp.concatenate([out_r[j] for j in range(S5_SEGS)], axis=0)
        self.st_ref[:, half:] = jnp.concatenate([out_i[j] for j in range(S5_SEGS)], axis=0)


def _s5_body(tl, seg_len, ucf_ref, unf_ref, ucb_ref, unb_ref, perm_ref, permt_ref, bb_ref, a_ref, c_ref, yf_ref,
             yb_ref, baf_ref, bbf_ref, bab_ref, bbb_ref, st_ref):
    ps = pl.program_id(1)
    i = pl.program_id(2)
    dirs = [_S5Dir(True, tl, ucf_ref, unf_ref, yf_ref, baf_ref, bbf_ref, st_ref.at[0], perm_ref, permt_ref,
                   bb_ref.at[0], a_ref.at[0], c_ref.at[0]),
            _S5Dir(False, tl, ucb_ref, unb_ref, yb_ref, bab_ref, bbb_ref, st_ref.at[1], perm_ref, permt_ref,
                   bb_ref.at[1], a_ref.at[1], c_ref.at[1])]

    @pl.when(jnp.logical_and(i == 0, ps == 0))
    def _():
        st_ref[...] = jnp.zeros_like(st_ref)

    @pl.when(jnp.logical_and(i == 0, ps == 1))
    def _():
        for z in dirs:
            z.init_state(seg_len)

    @pl.when(i == 0)
    def _():
        _run_lockstep([z.project(z.ucur_ref, z.off_a, z.bua_ref) for z in dirs])

    @pl.when(ps == 0)
    def _():
        _run_lockstep([g for z in dirs for g in (z.scan(z.bua_ref, False), z.project(z.ucur_ref, z.off_b, z.bub_ref))])
        _run_lockstep([g for z in dirs for g in (z.scan(z.bub_ref, False), z.project(z.unext_ref, z.off_a, z.bua_ref))])

    @pl.when(ps == 1)
    def _():
        _run_lockstep([g for z in dirs for g in (z.scan(z.bua_ref, True), z.project(z.ucur_ref, z.off_b, z.bub_ref))])
        _run_lockstep([g for z in dirs for g in (z.scan(z.bub_ref, True), z.emit(z.bua_ref, z.off_a),
                                                 z.project(z.unext_ref, z.off_a, z.bua_ref))])
        _run_lockstep([z.emit(z.bub_ref, z.off_b) for z in dirs])


def _s5_params(lam_re, lam_im, log_dt, b_re, b_im, c_re, c_im):
    g, p, hch = SSM_GROUPS, SSM_STATE, SSM_CH
    dt = jnp.exp(log_dt)[..., None]
    mag = jnp.exp(lam_re * dt)
    lb_re = mag * jnp.cos(lam_im * dt)
    lb_im = mag * jnp.sin(lam_im * dt)
    den = lam_re * lam_re + lam_im * lam_im
    nr = lb_re - 1.0
    coef_re = (nr * lam_re + lb_im * lam_im) / den
    coef_im = (lb_im * lam_re - nr * lam_im) / den
    bb_re = coef_re[..., None] * b_re - coef_im[..., None] * b_im
    bb_im = coef_re[..., None] * b_im + coef_im[..., None] * b_re
    eye = jnp.eye(g, dtype=F32)
    bd_in = lambda m: jnp.einsum('dgph,gk->dghkp', m, eye).reshape(2, g * hch, g * p)
    bb = jnp.concatenate([bd_in(bb_re), bd_in(bb_im)], axis=-1)
    bd_out = lambda m: jnp.einsum('dghp,gk->dgpkh', m, eye).reshape(2, g * p, g * hch)
    cc = jnp.concatenate([bd_out(c_re), -bd_out(c_im)], axis=1)
    a = jnp.concatenate([lb_re.reshape(2, 1, g * p), lb_im.reshape(2, 1, g * p)], axis=-1)
    return bb, a, cc


def _s5_scan(p_ssm, bsz, seq, bb, a, cc, tl):
    seg_len = seq // S5_SEGS
    n_pairs = seg_len // (2 * tl)
    rows = tl * S5_SEGS
    w2 = 2 * SSM_GROUPS * SSM_STATE
    src = (np.arange(rows) % S5_SEGS) * tl + np.arange(rows) // S5_SEGS
    perm = np.zeros((rows, rows), np.float32)
    perm[np.arange(rows), src] = 1.0
    last = n_pairs - 1
    u4 = p_ssm.reshape(bsz, S5_SEGS, seg_len, W_SSM)
    ublk = (1, S5_SEGS, 2 * tl, W_SSM)
    const = lambda shape: pl.BlockSpec(shape, lambda b, ps, i: (0,) * len(shape))
    y_f, y_b = pl.pallas_call(
        functools.partial(_s5_body, tl, seg_len),
        grid=(bsz, 2, n_pairs),
        in_specs=[pl.BlockSpec(ublk, lambda b, ps, i: (b, 0, i, 0)),
                  pl.BlockSpec(ublk, lambda b, ps, i: (b, 0, jnp.minimum(i + 1, last), 0)),
                  pl.BlockSpec(ublk, lambda b, ps, i: (b, 0, last - i, 0)),
                  pl.BlockSpec(ublk, lambda b, ps, i: (b, 0, jnp.maximum(last - i - 1, 0), 0)),
                  const((rows, rows)), const((rows, rows)), const((2, W_SSM, w2)), const((2, 1, w2)),
                  const((2, w2, W_SSM))],
        out_specs=[pl.BlockSpec(ublk, lambda b, ps, i: (b, 0, jnp.where(ps == 0, 0, i), 0)),
                   pl.BlockSpec(ublk, lambda b, ps, i: (b, 0, jnp.where(ps == 0, last, last - i), 0))],
        out_shape=[jax.ShapeDtypeStruct((bsz, S5_SEGS, seg_len, W_SSM), BF16)] * 2,
        scratch_shapes=[pltpu.VMEM((rows, w2), F32)] * 4 + [pltpu.VMEM((2, S5_SEGS, w2), F32)],
        compiler_params=_cparams(("arbitrary",) * 3),
        name="s5_scan",
    )(u4, u4, u4, u4, jnp.asarray(perm).astype(BF16), jnp.asarray(perm.T).astype(BF16), bb.astype(BF16), a,
      cc.astype(BF16))
    return y_f.reshape(bsz * seq, W_SSM), y_b.reshape(bsz * seq, W_SSM)


def _gelu_tanh(x):
    c = math.sqrt(2.0 / math.pi)
    return 0.5 * x * (1.0 + jnp.tanh(c * (x + 0.044715 * (x * x * x))))


def _s5_post(y_f, y_b, u, dskip, glu_w, glu_b):
    y = _gelu_tanh(y_f.astype(F32) + y_b.astype(F32) + dskip * u)
    return y * jax.nn.sigmoid(_bdot(y, glu_w) + glu_b)


def _dot_nt(a, b):
    return lax.dot_general(a.astype(BF16), b.astype(BF16), (((1,), (1,)), ((), ())),
                           preferred_element_type=F32)


def _rwkv_masks(chunk):
    hh = RWKV_HEADS
    r = np.arange(hh * chunk)[:, None] // chunk
    bm_feat = (r == np.arange(W_RWKV)[None, :] // RWKV_HEAD).astype(np.float32)
    bm_time = (r == np.arange(hh * chunk)[None, :] // chunk).astype(np.float32)
    f = np.arange(W_RWKV)
    bm_head = (f[:, None] // RWKV_HEAD == f[None, :] // RWKV_HEAD).astype(np.float32)
    return bm_feat, bm_time, bm_head


def _rwkv_chunk(fwd, x, edge_row, logw, a, mu, k_k, k_a, r_k, bmf16, bmt16, bmh):
    ll = x.shape[0]
    w = W_RWKV
    rows = lax.broadcasted_iota(jnp.int32, x.shape, 0)
    if fwd:
        shifted = jnp.where(rows == 0, edge_row, pltpu.roll(x, 1, axis=0))
    else:
        shifted = jnp.where(rows == ll - 1, edge_row, pltpu.roll(x, ll - 1, axis=0))
    rkv = x + (shifted - x) * mu
    r, k, v = rkv[:, :w], rkv[:, w:2 * w], rkv[:, 2 * w:]
    bmh16 = bmh.astype(BF16)

    kk = k * k_k
    k2 = k * (1.0 + (a - 1.0) * k_a)
    both = jnp.concatenate([kk * kk, r * k2 * r_k], axis=0)
    hi = both.astype(BF16)
    lo = (both - hi.astype(F32)).astype(BF16)
    sums = jnp.dot(jnp.concatenate([hi, lo], axis=0), bmh16, preferred_element_type=F32)
    sums = sums[:2 * ll] + sums[2 * ll:]
    ksq = sums[:ll]
    bonus = sums[ll:] * v

    ti = lax.broadcasted_iota(jnp.int32, (ll, ll), 0)
    si = lax.broadcasted_iota(jnp.int32, (ll, ll), 1)
    tri = ((si <= ti) if fwd else (si >= ti)).astype(F32).astype(BF16)
    lhi = logw.astype(BF16)
    llo = (logw - lhi.astype(F32)).astype(BF16)
    cum2 = jnp.dot(tri, jnp.concatenate([lhi, llo], axis=1), preferred_element_type=F32)
    cum = cum2[:, :w] + cum2[:, w:]
    yield
    kk = kk * lax.rsqrt(ksq + 1e-12)
    ctot = jnp.sum(logw, axis=0, keepdims=True)
    e_neg = jnp.exp(-cum)
    e_rem = jnp.exp(ctot - cum)
    ah = -kk * jnp.exp(cum - logw)
    rh = r * jnp.exp(cum)
    bvec = kk * a
    bh, kh = bvec * e_neg, k2 * e_neg
    bt, kt = bvec * e_rem, k2 * e_rem

    def bd(m):
        m16 = m.astype(BF16)
        return jnp.concatenate([m16] * RWKV_HEADS, axis=0) * (bmf16 if m.shape[1] == w else bmt16)

    gram = _dot_nt(jnp.concatenate([ah, rh], axis=0), jnp.concatenate([bd(bh), bd(kh)], axis=0))
    yield
    l4 = RWKV_HEADS * ll
    tt = lax.broadcasted_iota(jnp.int32, (ll, l4), 0)
    ss = lax.broadcasted_iota(jnp.int32, (ll, l4), 1) % ll
    strict = (ss < tt) if fwd else (ss > tt)
    incl = (ss <= tt) if fwd else (ss >= tt)
    n_ab = jnp.where(strict, gram[:ll, :l4], 0.0)
    a_ak = jnp.where(strict, gram[:ll, l4:], 0.0)
    a_rb = jnp.where(incl, gram[ll:, :l4], 0.0)
    a_rk = jnp.where(incl, gram[ll:, l4:], 0.0)

    pw = n_ab
    tinv = jnp.where(ss == tt, 1.0, 0.0) + n_ab
    akv = _bdot(a_ak, bd(v))
    for _ in range(int(math.log2(ll)) - 1):
        pw = _bdot(pw, bd(pw))
        yield
        tinv = tinv + _bdot(tinv, bd(pw))
        yield

    ta = _bdot(tinv, jnp.concatenate([bd(ah), bd(akv)], axis=1))
    yield
    ap, wm = ta[:, :w], ta[:, w:]
    rp = rh + _bdot(a_rb, bd(ap))
    y0 = _bdot(jnp.concatenate([a_rb, a_rk], axis=1), jnp.concatenate([bd(wm), bd(v)], axis=0))
    btk = jnp.transpose(jnp.concatenate([bt, kt], axis=0))
    hp = RWKV_PAIR
    pqs = []
    for cs in (slice(0, hp), slice(hp, w)):
        rhs = jnp.concatenate([jnp.concatenate([ap[:, cs], wm[:, cs]], axis=1),
                               jnp.concatenate([jnp.zeros_like(v[:, cs]), v[:, cs]], axis=1)], axis=0)
        pqs.append(_bdot(btk[cs, :], rhs))
    yield
    eye = (lax.broadcasted_iota(jnp.int32, (hp, hp), 0) == lax.broadcasted_iota(jnp.int32, (hp, hp), 1))
    bmp = bmh[:hp, :hp]
    decay = jnp.exp(ctot)
    pm = [pq[:, :hp] * bmp + jnp.where(eye, decay[:, cs], 0.0) for pq, cs in zip(pqs, (slice(0, hp), slice(hp, w)))]
    qm = [pq[:, hp:] * bmp for pq in pqs]
    yield rp, y0, pm, qm, bonus


def _run_lockstep(gens):
    results = [None] * len(gens)
    live = list(range(len(gens)))
    while live:
        for s in list(live):
            try:
                out = next(gens[s])
            except StopIteration:
                live.remove(s)
            else:
                if out is not None:
                    results[s] = out
    return results


def _rwkv_body(chunk, nsub, nb, *refs):
    dir_refs = [refs[0:4], refs[4:8]]
    mu_ref, kk_ref, ka_ref, rk_ref, bmf_ref, bmt_ref, bmh_ref = refs[8:15]
    y_refs, bonus_refs, h_ref = refs[15:17], refs[17:19], refs[19]
    c = pl.program_id(0)

    @pl.when(c == 0)
    def _():
        h_ref[...] = jnp.zeros_like(h_ref)

    gens, where = [], []
    for d in range(2):
        x_ref, edge_ref, logw_ref, a_ref = dir_refs[d]
        order = range(nsub) if d == 0 else range(nsub - 1, -1, -1)
        for b in range(nb):
            x = x_ref[b].astype(F32)
            for n, j in enumerate(order):
                rows = slice(j * chunk, (j + 1) * chunk)
                if n == 0:
                    hb = edge_ref.shape[1]
                    edge = edge_ref[b, hb - 1:hb, :] if d == 0 else edge_ref[b, 0:1, :]
                    edge = jnp.where(c == 0, 0.0, edge.astype(F32))
                else:
                    e = j * chunk - 1 if d == 0 else (j + 1) * chunk
                    edge = x[e:e + 1, :]
                gens.append(_rwkv_chunk(d == 0, x[rows], edge, logw_ref[0, b, rows, :],
                                        a_ref[0, b, rows, :].astype(F32), mu_ref[d], kk_ref[...], ka_ref[...],
                                        rk_ref[...], bmf_ref[...], bmt_ref[...], bmh_ref[...]))
                where.append((d, b, rows))
    parts = _run_lockstep(gens)

    def carry(s):
        hp = RWKV_PAIR
        h = [h_ref[s, 0], h_ref[s, 1]]
        for n in range(nsub):
            rp, y0, pm, qm, bonus = parts[s * nsub + n]
            d, b, rows = where[s * nsub + n]
            y = jnp.concatenate([_bdot(rp[:, :hp], h[0]), _bdot(rp[:, hp:], h[1])], axis=1) + y0
            y_refs[d][b, rows, :] = y.astype(y_refs[d].dtype)
            bonus_refs[d][b, rows, :] = bonus.astype(bonus_refs[d].dtype)
            h = [_bdot(pm[0], h[0]) + qm[0], _bdot(pm[1], h[1]) + qm[1]]
            yield
        h_ref[s, 0] = h[0]
        h_ref[s, 1] = h[1]

    _run_lockstep([carry(s) for s in range(2 * nb)])


def _rwkv_scan(p_rkv, logw, agate, bsz, seq, chunk, nsub, mu_rkv, k_k, k_a, r_k):
    n = bsz * seq
    blk = chunk * nsub
    n_chunks = seq // blk
    hb = PACKED_ROWS
    per = blk // hb
    w3 = 3 * W_RWKV
    bmf, bmt, bmh = _rwkv_masks(chunk)
    consts = (mu_rkv.reshape(2, 1, w3), k_k.reshape(1, -1), k_a.reshape(1, -1), r_k.reshape(1, -1),
              jnp.asarray(bmf).astype(BF16), jnp.asarray(bmt).astype(BF16), jnp.asarray(bmh))
    x3 = p_rkv.reshape(bsz, seq, w3)
    logw4 = logw.reshape(2, bsz, seq, W_RWKV)
    a4 = agate.reshape(2, bsz, seq, W_RWKV)
    args, in_specs, out_specs = [], [], []
    for d in range(2):
        cidx = (lambda c: c) if d == 0 else (lambda c: n_chunks - 1 - c)
        if d == 0:
            edge = lambda c: (0, jnp.maximum(c * per - 1, 0), 0)
        else:
            edge = lambda c: (0, jnp.minimum((n_chunks - c) * per, seq // hb - 1), 0)
        args += [x3, x3, logw4, a4]
        in_specs += [pl.BlockSpec((bsz, blk, w3), lambda c, cidx=cidx: (0, cidx(c), 0)),
                     pl.BlockSpec((bsz, hb, w3), edge),
                     pl.BlockSpec((1, bsz, blk, W_RWKV), lambda c, d=d, cidx=cidx: (d, 0, cidx(c), 0)),
                     pl.BlockSpec((1, bsz, blk, W_RWKV), lambda c, d=d, cidx=cidx: (d, 0, cidx(c), 0))]
        out_specs.append(pl.BlockSpec((bsz, blk, W_RWKV), lambda c, cidx=cidx: (0, cidx(c), 0)))

    def full(a):
        nd = a.ndim
        return pl.BlockSpec(a.shape, lambda c: (0,) * nd)

    y_f, y_b, bonus_f, bonus_b = pl.pallas_call(
        functools.partial(_rwkv_body, chunk, nsub, bsz),
        grid=(n_chunks,),
        in_specs=in_specs + [full(a) for a in consts],
        out_specs=out_specs + out_specs,
        out_shape=[jax.ShapeDtypeStruct((bsz, seq, W_RWKV), BF16)] * 4,
        scratch_shapes=[pltpu.VMEM((2 * bsz, W_RWKV // RWKV_PAIR, RWKV_PAIR, RWKV_PAIR), F32)],
        compiler_params=_cparams(("arbitrary",)),
        name="rwkv_scan",
    )(*args, *consts)
    return [a.reshape(n, W_RWKV) for a in (y_f, y_b, bonus_f, bonus_b)]


def _head_mean(x, bmh16):
    return _bdot(x, bmh16) * (1.0 / RWKV_HEAD)


def _rwkv_post(y, bonus, g, gn_g, gn_b, bmh16):
    mu = _head_mean(y, bmh16)
    yc = y - mu
    var = _head_mean(yc * yc, bmh16)
    yn = yc * lax.rsqrt(var + RWKV_GN_EPS) * gn_g + gn_b
    return (yn + bonus) * g


FFT_R1 = 64


def _fft_tables(seq):
    r1, r2 = FFT_R1, seq // FFT_R1
    i1 = np.arange(r1)
    ang1 = ((i1[:, None] * i1[None, :]) % r1) * (2.0 * math.pi / r1)
    stage1 = np.concatenate([np.cos(ang1), -np.sin(ang1)], axis=0)
    k1 = i1[:, None, None]
    k2 = np.arange(r2)[None, :, None]
    n2 = np.arange(r2)[None, None, :]
    ang2 = ((n2 * (k1 + r1 * k2)) % seq) * (2.0 * math.pi / seq)
    mr, mi = np.cos(ang2), -np.sin(ang2)
    stage2 = np.concatenate([np.concatenate([mr, -mi], axis=2),
                             np.concatenate([mi, mr], axis=2)], axis=1)
    c = np.arange(W_FFT)
    same = (c[:, None] // FFT_CH) == (c[None, :] // FFT_CH)
    angc = ((c[:, None] * c[None, :]) % FFT_CH) * (2.0 * math.pi / FFT_CH)
    scale = 1.0 / math.sqrt(seq * FFT_CH)
    chan = np.concatenate([np.where(same, np.cos(angc), 0.0), np.where(same, np.sin(angc), 0.0)],
                          axis=0) * scale
    tn = SUBLANES
    big = np.zeros((tn, 2 * r1, r1, tn))
    for j in range(tn):
        big[j, :, :, j] = stage1
    stage1 = big.reshape(tn * 2 * r1, r1 * tn)
    return tuple(jnp.asarray(t, dtype=F32).astype(BF16) for t in (stage1, stage2, chan))


def _fft1_body(z_ref, m_ref, a_ref):
    r1, tn = FFT_R1, z_ref.shape[2]
    a = _bdot(m_ref[...], z_ref[0].reshape(r1 * tn, W_FFT))
    for j in range(tn):
        a_ref[0, 0, :, j, :] = a[(2 * j) * r1:(2 * j + 1) * r1]
        a_ref[0, 1, :, j, :] = a[(2 * j + 1) * r1:(2 * j + 2) * r1]


def _fft2_body(kb, r2, a_ref, m_ref, chan_ref, o_ref):
    for j in range(kb):
        x = jnp.concatenate([a_ref[0, 0, j], a_ref[0, 1, j]], axis=0)
        f = _bdot(m_ref[j], x)
        fri = jnp.concatenate([f[:r2], f[r2:]], axis=1)
        o_ref[0, :, j * W_FFT:(j + 1) * W_FFT] = _bdot(fri, chan_ref[...]).astype(o_ref.dtype)


def _fourier(p_fft, bsz, seq, tables):
    r1, r2 = FFT_R1, seq // FFT_R1
    stage1, stage2, chan = tables
    tn = SUBLANES
    a = pl.pallas_call(
        _fft1_body,
        grid=(bsz, r2 // tn),
        in_specs=[pl.BlockSpec((1, r1, tn, W_FFT), lambda b, j: (b, 0, j, 0)),
                  pl.BlockSpec((tn * 2 * r1, r1 * tn), lambda b, j: (0, 0))],
        out_specs=pl.BlockSpec((1, 2, r1, tn, W_FFT), lambda b, j: (b, 0, 0, j, 0)),
        out_shape=jax.ShapeDtypeStruct((bsz, 2, r1, r2, W_FFT), F32),
        compiler_params=_cparams(("parallel", "parallel")),
        name="fft_stage1",
    )(p_fft.reshape(bsz, r1, r2, W_FFT), stage1)
    kb = 8
    out = pl.pallas_call(
        functools.partial(_fft2_body, kb, r2),
        grid=(bsz, r1 // kb),
        in_specs=[pl.BlockSpec((1, 2, kb, r2, W_FFT), lambda b, j: (b, 0, j, 0, 0)),
                  pl.BlockSpec((kb, 2 * r2, 2 * r2), lambda b, j: (j, 0, 0)),
                  pl.BlockSpec((2 * W_FFT, W_FFT), lambda b, j: (0, 0))],
        out_specs=pl.BlockSpec((1, r2, kb * W_FFT), lambda b, j: (b, 0, j)),
        out_shape=jax.ShapeDtypeStruct((bsz, r2, r1 * W_FFT), BF16),
        compiler_params=_cparams(("parallel", "parallel")),
        name="fft_stage2",
    )(a, stage2, chan)
    return out.reshape(bsz * seq, W_FFT)


def _mixout_body(seq, tile, h_ref, ysf_ref, ysb_ref, pssm_ref, yrf_ref, yrb_ref, bnf_ref, bnb_ref, g_ref, pc_ref,
                 pcp_ref, pcn_ref, yf_ref, wout_ref, dskip_ref, gluw_ref, glub_ref, gng_ref, gnb_ref, convw_ref,
                 bmh_ref, lng_ref, lnb_ref, o_ref):
    i = pl.program_id(0)
    f32 = lambda ref: ref[...].astype(F32)
    y_a = _s5_post(f32(ysf_ref), f32(ysb_ref), f32(pssm_ref), dskip_ref[...], gluw_ref[...], glub_ref[...])
    y_b = _rwkv_post(f32(yrf_ref) + f32(yrb_ref), f32(bnf_ref) + f32(bnb_ref), f32(g_ref), gng_ref[...],
                     gnb_ref[...], bmh_ref[...])
    pc = f32(pc_ref)
    wc = W_CONV
    prev_row, next_row = _halo_rows(pcp_ref, pcn_ref, i * tile, tile, seq)
    z = pc[:, wc:2 * wc] * pc[:, 2 * wc:]
    z_prev, z_next = _shift_rows(z, prev_row[:, wc:2 * wc] * prev_row[:, 2 * wc:],
                                 next_row[:, wc:2 * wc] * next_row[:, 2 * wc:])
    y_c = pc[:, :wc] * (convw_ref[0:1, :] * z_prev + convw_ref[1:2, :] * z + convw_ref[2:3, :] * z_next)
    mix = (_bdot(y_a, wout_ref[0:W_SSM, :]) + _bdot(y_b, wout_ref[W_SSM:W_SSM + W_RWKV, :])
           + _bdot(y_c, wout_ref[W_SSM + W_RWKV:W_SSM + W_RWKV + W_CONV, :])
           + _bdot(yf_ref[...], wout_ref[W_SSM + W_RWKV + W_CONV:, :]))
    o_ref[...] = _layer_norm(DEEPNORM_ALPHA * h_ref[...] + mix, lng_ref[...], lnb_ref[...])


def _mixout(h, seq, tile, y_s5, p_ssm, rwkv_outs, g, p_conv, y_fft, w_out_all, l, dskip, glu_w, glu_b, gn_g, gn_b,
            conv_w, ln_g, ln_b):
    n, d = h.shape
    row = lambda w: pl.BlockSpec((tile, w), lambda i: (i, 0))
    pcp, pcn = _halo_specs(tile, 3 * W_CONV, n, PACKED_ROWS)
    bmh16 = jnp.asarray(_rwkv_masks(RWKV_HEAD)[2]).astype(BF16)
    consts = (dskip.reshape(1, -1), glu_w.astype(BF16), glu_b.reshape(1, -1),
              gn_g.reshape(1, -1), gn_b.reshape(1, -1), conv_w, bmh16, ln_g.reshape(1, -1), ln_b.reshape(1, -1))

    def full(a):
        nd = a.ndim
        return pl.BlockSpec(a.shape, lambda i: (0,) * nd)

    return pl.pallas_call(
        functools.partial(_mixout_body, seq, tile),
        grid=(n // tile,),
        in_specs=[row(d)] + [row(W_SSM)] * 3 + [row(W_RWKV)] * 5
                 + [row(3 * W_CONV), pcp, pcn, row(W_FFT), _layer_spec(w_out_all, l)] + [full(c) for c in consts],
        out_specs=row(d),
        out_shape=jax.ShapeDtypeStruct((n, d), F32),
        compiler_params=_cparams(("parallel",)),
        name="mix_out",
    )(h, *y_s5, p_ssm, *rwkv_outs, g, p_conv, p_conv, p_conv, y_fft, w_out_all, *consts)


def _ffn_body(cols, h_ref, w1_ref, w3_ref, w2_ref, lng_ref, lnb_ref, o_ref, u_ref):
    x = h_ref[...].astype(BF16)
    dff = w1_ref.shape[1]
    for c0 in range(0, dff, cols):
        cs = slice(c0, min(c0 + cols, dff))
        a1 = jnp.dot(x, w1_ref[:, cs], preferred_element_type=F32)
        a3 = jnp.dot(x, w3_ref[:, cs], preferred_element_type=F32)
        u_ref[:, cs] = (jax.nn.silu(a1) * a3).astype(BF16)
    f = jnp.dot(u_ref[...], w2_ref[...], preferred_element_type=F32)
    o_ref[...] = _layer_norm(DEEPNORM_ALPHA * h_ref[...] + f, lng_ref[...], lnb_ref[...])


def _ffn(h, tile, cols, w1_all, w3_all, w2_all, l, ln_g, ln_b):
    n, d = h.shape
    dff = w1_all.shape[2]
    once = lambda shape: pl.BlockSpec(shape, lambda i: (0, 0), pipeline_mode=pl.Buffered(1))
    layer = lambda w: _layer_spec(w, l, pipeline_mode=pl.Buffered(1))
    return pl.pallas_call(
        functools.partial(_ffn_body, cols),
        grid=(n // tile,),
        in_specs=[pl.BlockSpec((tile, d), lambda i: (i, 0)), layer(w1_all), layer(w3_all), layer(w2_all),
                  once((1, d)), once((1, d))],
        out_specs=pl.BlockSpec((tile, d), lambda i: (i, 0)),
        out_shape=jax.ShapeDtypeStruct((n, d), F32),
        scratch_shapes=[pltpu.VMEM((tile, dff), BF16)],
        compiler_params=_cparams(("parallel",)),
        name="ffn",
    )(h, w1_all, w3_all, w2_all, ln_g.reshape(1, d), ln_b.reshape(1, d))


ROW_TILE = 512
S5_TILE_STEPS = 64
RWKV_CHUNK = 64
RWKV_CHUNKS_PER_STEP = 4
FFN_COLS = 256


def kernel(x, ln0_g, ln0_b, w_in, s5_lambda_re, s5_lambda_im, s5_log_dt, s5_b_re, s5_b_im, s5_c_re, s5_c_im, s5_d,
           s5_glu_w, s5_glu_b, rwkv_mu_rkv, rwkv_mu_w, rwkv_mu_a, rwkv_mu_g, rwkv_w0, rwkv_w1, rwkv_w2, rwkv_a0,
           rwkv_a1, rwkv_a2, rwkv_g1, rwkv_g2, rwkv_k_k, rwkv_k_a, rwkv_r_k, rwkv_gn_g, rwkv_gn_b, conv_w, w_out,
           ln1_g, ln1_b, ffn_w1, ffn_w3, ffn_w2, ln2_g, ln2_b):
    bsz, seq, d = x.shape
    n = bsz * seq
    tile = min(ROW_TILE, seq)
    s5_tl = min(S5_TILE_STEPS, seq // S5_SEGS // 2)
    fft_tables = _fft_tables(seq)
    w_in16, w_out16 = w_in.astype(BF16), w_out.astype(BF16)
    ffn_w116, ffn_w316, ffn_w216 = ffn_w1.astype(BF16), ffn_w3.astype(BF16), ffn_w2.astype(BF16)
    h = x.reshape(n, d)
    for l in range(w_in.shape[0]):
        outs = _inproj(h, seq, tile, w_in16, l, rwkv_mu_w[l], rwkv_mu_a[l], rwkv_mu_g[l], rwkv_w1[l], rwkv_w2[l],
                       rwkv_w0[l], rwkv_a1[l], rwkv_a2[l], rwkv_a0[l], rwkv_g1[l], rwkv_g2[l],
                       entry_ln=(ln0_g, ln0_b) if l == 0 else None)
        p_ssm, p_rkv, p_conv, p_fft, logw, agate, g = outs[:7]
        if l == 0:
            h = outs[7]
        bb, a, cc = _s5_params(s5_lambda_re[l], s5_lambda_im[l], s5_log_dt[l], s5_b_re[l], s5_b_im[l],
                               s5_c_re[l], s5_c_im[l])
        y_s5 = _s5_scan(p_ssm, bsz, seq, bb, a, cc, s5_tl)
        rwkv_outs = _rwkv_scan(p_rkv, logw, agate, bsz, seq, RWKV_CHUNK, RWKV_CHUNKS_PER_STEP, rwkv_mu_rkv[l], rwkv_k_k[l],
                               rwkv_k_a[l], rwkv_r_k[l].reshape(-1))
        y_fft = _fourier(p_fft, bsz, seq, fft_tables)
        h = _mixout(h, seq, tile, y_s5, p_ssm, rwkv_outs, g, p_conv, y_fft, w_out16, l, s5_d[l], s5_glu_w[l],
                    s5_glu_b[l], rwkv_gn_g[l], rwkv_gn_b[l], conv_w[l], ln1_g[l], ln1_b[l])
        h = _ffn(h, tile, FFN_COLS, ffn_w116, ffn_w316, ffn_w216, l, ln2_g[l], ln2_b[l])
    return h.reshape(bsz, seq, d)
```
